```python
import jax, jax.numpy as jnp
from jax import lax
import numpy as np

D_MODEL = 2048
BATCH = 8
SEQ = 2048
DEPTH = 4

GRID_W = 64
CTX_LEN = 256
HEAD_DIM = 128
D_BRANCH = D_MODEL // 2
N_Q_HEADS = D_BRANCH // HEAD_DIM
N_KV_HEADS = N_Q_HEADS // 4
Q_PER_KV = N_Q_HEADS // N_KV_HEADS
D_KV = N_KV_HEADS * HEAD_DIM
ROPE_THETA = 10000.0
ROPE_PAIRS_PER_AXIS = HEAD_DIM // 4
Q_BLOCK = 128
D_LRU = D_BRANCH
LRU_BLOCKS = 16
LRU_BLOCK_W = D_LRU // LRU_BLOCKS
LRU_C = 8.0
CONV_WIDTH = 4
CONV_LEFT = CONV_WIDTH // 2
D_FOURIER = D_BRANCH
FOURIER_GROUP = 128
N_FOURIER_GROUPS = D_FOURIER // FOURIER_GROUP
N_BRANCHES = 3
D_FF = -(-8 * D_MODEL // (3 * 256)) * 256
D_IN = D_BRANCH + 2 * D_KV + 2 * D_LRU + D_FOURIER
SPLITS = (D_BRANCH, D_BRANCH + D_KV, D_BRANCH + 2 * D_KV,
          D_BRANCH + 2 * D_KV + D_LRU, D_BRANCH + 2 * D_KV + 2 * D_LRU)
NORM_EPS = 1e-6

kernel_name = "hybrid_gated_gqa_rglru_fourier_dit"


def rmsnorm(x, g):
    xf = x.astype(jnp.float32)
    y = xf * lax.rsqrt(jnp.mean(xf * xf, axis=-1, keepdims=True) + NORM_EPS)
    return (y * g.astype(jnp.float32)).astype(x.dtype)


def axial_rope_tables(n):
    rows = n // GRID_W
    row = jnp.repeat(jnp.arange(rows, dtype=jnp.float32), GRID_W)
    col = jnp.tile(jnp.arange(GRID_W, dtype=jnp.float32), rows)
    inv = ROPE_THETA ** (-jnp.arange(ROPE_PAIRS_PER_AXIS, dtype=jnp.float32) / ROPE_PAIRS_PER_AXIS)
    ang = jnp.concatenate([row[:, None] * inv, col[:, None] * inv], axis=-1)
    return jnp.cos(ang), jnp.sin(ang)


def apply_rope(x, cos, sin):
    xf = x.astype(jnp.float32)
    x1, x2 = xf[..., :HEAD_DIM // 2], xf[..., HEAD_DIM // 2:]
    cb, sb = cos[None, :, None, :], sin[None, :, None, :]
    return jnp.concatenate([x1 * cb - x2 * sb, x2 * cb + x1 * sb], axis=-1).astype(x.dtype)


def attend(q, k, v):
    s = jnp.einsum('bqhgd,bkhd->bhgqk', q, k).astype(jnp.float32) * (HEAD_DIM ** -0.5)
    p = jax.nn.softmax(s, axis=-1).astype(v.dtype)
    return jnp.einsum('bhgqk,bkhd->bqhgd', p, v)


def attend_blocks(q, k, v):
    b, t = q.shape[:2]
    nb = t // Q_BLOCK
    qb = jnp.moveaxis(q.reshape(b, nb, Q_BLOCK, N_KV_HEADS, Q_PER_KV, HEAD_DIM), 1, 0)
    ob = lax.map(lambda qq: attend(qq, k, v), qb)
    return jnp.moveaxis(ob, 0, 1).reshape(b, t, D_BRANCH)


def short_conv(u, w, b):
    t = u.shape[1]
    up = jnp.pad(u, ((0, 0), (CONV_LEFT, CONV_WIDTH - 1 - CONV_LEFT), (0, 0)))
    y = b
    for j in range(CONV_WIDTH):
        y = y + up[:, j:j + t] * w[j]
    return y


def rglru_coeffs(u, w_r, b_r, w_i, b_i, lam):
    b, t, _ = u.shape
    ub = u.reshape(b, t, LRU_BLOCKS, LRU_BLOCK_W)
    r = jax.nn.sigmoid((jnp.einsum('btnc,ncd->btnd', ub, w_r).reshape(b, t, D_LRU) + b_r).astype(jnp.float32))
    i = jax.nn.sigmoid((jnp.einsum('btnc,ncd->btnd', ub, w_i).reshape(b, t, D_LRU) + b_i).astype(jnp.float32))
    log_a = -LRU_C * r * jax.nn.softplus(-lam.astype(jnp.float32))
    a = jnp.exp(log_a)
    bx = jnp.sqrt(-jnp.expm1(2.0 * log_a)) * i * u.astype(jnp.float32)
    return a, bx


def linear_scan(a, bx, h0, reverse):
    def step(h, ab):
        h = ab[0] * h + ab[1]
        return h, h
    h_last, hs = lax.scan(step, h0, (jnp.swapaxes(a, 0, 1), jnp.swapaxes(bx, 0, 1)), reverse=reverse)
    return h_last, jnp.swapaxes(hs, 0, 1)


def rglru_bidir(ul, uc, w_r, b_r, w_i, b_i, lam):
    yl = jnp.zeros(ul.shape, jnp.float32)
    yc = jnp.zeros(uc.shape, jnp.float32)
    for d, rev in enumerate((False, True)):
        ac, bc = rglru_coeffs(uc, w_r[d], b_r[d], w_i[d], b_i[d], lam[d])
        al, bl = rglru_coeffs(ul, w_r[d], b_r[d], w_i[d], b_i[d], lam[d])
        h0 = jnp.zeros((uc.shape[0], D_LRU), jnp.float32)
        hc_last, hc = linear_scan(ac, bc, h0, rev)
        _, hl = linear_scan(al, bl, hc_last, rev)
        yl = yl + hl
        yc = yc + hc
    return yl.astype(ul.dtype), yc.astype(uc.dtype)


def fourier_mix(u):
    b, t, _ = u.shape
    uf = u.astype(jnp.float32).reshape(b, t, N_FOURIER_GROUPS, FOURIER_GROUP)
    y = jnp.fft.fft2(uf, axes=(1, 3), norm='ortho').real
    return y.reshape(b, t, D_FOURIER).astype(u.dtype)


def gated_merge(h, y_att, y_rec, y_fou, w_branch, w_gate, b_gate, w_out):
    ys = jnp.stack([y_att, y_rec, y_fou], axis=2)
    br = jnp.einsum('btkc,kcd->btkd', ys, w_branch)
    g = jax.nn.sigmoid(h @ w_gate + b_gate).reshape(h.shape[0], h.shape[1], N_BRANCHES, D_MODEL)
    return jnp.sum(g * br, axis=2) @ w_out


def swiglu(h, w_in, w_out):
    gt, up = jnp.split(h @ w_in, 2, axis=-1)
    return (jax.nn.silu(gt) * up) @ w_out


def setup_inputs(seed: int = 0) -> dict:
    key = jax.random.key(seed)
    ks = jax.random.split(key, 26)
    f32 = jnp.float32

    def nrm(k, shape, scale):
        return scale * jax.random.normal(k, shape, f32)

    u = jax.random.uniform(ks[17], (DEPTH, 2, D_LRU), f32, minval=0.9, maxval=0.999)
    return {
        'x': nrm(ks[0], (BATCH, SEQ, D_MODEL), 1.0),
        'c': nrm(ks[1], (BATCH, D_MODEL), 1.0),
        'ctx': nrm(ks[2], (BATCH, CTX_LEN, D_MODEL), 1.0),
        'c_ctx': nrm(ks[3], (D_MODEL,), 1.0),
        'w_mod': nrm(ks[4], (DEPTH, D_MODEL, 6 * D_MODEL), D_MODEL ** -0.5),
        'b_mod': nrm(ks[5], (DEPTH, 6 * D_MODEL), 0.01),
        'g_norm1': 1.0 + nrm(ks[6], (DEPTH, D_MODEL), 0.02),
        'g_norm2': 1.0 + nrm(ks[7], (DEPTH, D_MODEL), 0.02),
        'w_in': nrm(ks[8], (DEPTH, D_MODEL, D_IN), D_MODEL ** -0.5),
        'q_gain': 1.0 + nrm(ks[9], (DEPTH, HEAD_DIM), 0.02),
        'k_gain': 1.0 + nrm(ks[10], (DEPTH, HEAD_DIM), 0.02),
        'conv_w': nrm(ks[11], (DEPTH, CONV_WIDTH, D_LRU), CONV_WIDTH ** -0.5),
        'conv_b': nrm(ks[12], (DEPTH, D_LRU), 0.01),
        'lru_w_r': nrm(ks[13], (DEPTH, 2, LRU_BLOCKS, LRU_BLOCK_W, LRU_BLOCK_W), LRU_BLOCK_W ** -0.5),
        'lru_b_r': nrm(ks[14], (DEPTH, 2, D_LRU), 0.01),
        'lru_w_i': nrm(ks[15], (DEPTH, 2, LRU_BLOCKS, LRU_BLOCK_W, LRU_BLOCK_W), LRU_BLOCK_W ** -0.5),
        'lru_b_i': nrm(ks[16], (DEPTH, 2, D_LRU), 0.01),
        'lru_lambda': jnp.log(u) - jnp.log1p(-u),
        'w_branch': nrm(ks[18], (DEPTH, N_BRANCHES, D_BRANCH, D_MODEL), D_BRANCH ** -0.5),
        'w_gate': nrm(ks[19], (DEPTH, D_MODEL, N_BRANCHES * D_MODEL), D_MODEL ** -0.5),
        'b_gate': nrm(ks[20], (DEPTH, N_BRANCHES * D_MODEL), 0.01),
        'w_out': nrm(ks[21], (DEPTH, D_MODEL, D_MODEL), D_MODEL ** -0.5),
        'w_ffn_in': nrm(ks[22], (DEPTH, D_MODEL, 2 * D_FF), D_MODEL ** -0.5),
        'w_ffn_out': nrm(ks[23], (DEPTH, D_FF, D_MODEL), D_FF ** -0.5),
        'g_final': 1.0 + nrm(ks[24], (D_MODEL,), 0.02),
    }


def reference(x, c, ctx, c_ctx, w_mod, b_mod, g_norm1, g_norm2, w_in, q_gain, k_gain,
              conv_w, conv_b, lru_w_r, lru_b_r, lru_w_i, lru_b_i, lru_lambda,
              w_branch, w_gate, b_gate, w_out, w_ffn_in, w_ffn_out, g_final):
    b, t = x.shape[0], x.shape[1]
    tc = ctx.shape[1]
    cos, sin = axial_rope_tables(t)
    xc = ctx
    for l in range(DEPTH):
        update_ctx = l < DEPTH - 1
        s1l, sc1l, g1l, s2l, sc2l, g2l = jnp.split(
            (jax.nn.silu(c) @ w_mod[l] + b_mod[l])[:, None, :], 6, axis=-1)
        s1c, sc1c, g1c, s2c, sc2c, g2c = jnp.split(
            jax.nn.silu(c_ctx) @ w_mod[l] + b_mod[l], 6, axis=-1)

        hl = rmsnorm(x, g_norm1[l]) * (1 + sc1l) + s1l
        hc = rmsnorm(xc, g_norm1[l]) * (1 + sc1c) + s1c
        ql, kl, vl, uxl, ugl, ufl = jnp.split(hl @ w_in[l], SPLITS, axis=-1)
        qc, kc, vc, uxc, ugc, ufc = jnp.split(hc @ w_in[l], SPLITS, axis=-1)

        kl = apply_rope(rmsnorm(kl.reshape(b, t, N_KV_HEADS, HEAD_DIM), k_gain[l]), cos, sin)
        kc = rmsnorm(kc.reshape(b, tc, N_KV_HEADS, HEAD_DIM), k_gain[l])
        vl = vl.reshape(b, t, N_KV_HEADS, HEAD_DIM)
        vc = vc.reshape(b, tc, N_KV_HEADS, HEAD_DIM)
        ql = apply_rope(rmsnorm(ql.reshape(b, t, N_Q_HEADS, HEAD_DIM), q_gain[l]), cos, sin)
        ql = ql.reshape(b, t, N_KV_HEADS, Q_PER_KV, HEAD_DIM)
        att_l = attend_blocks(ql, jnp.concatenate([kl, kc], axis=1), jnp.concatenate([vl, vc], axis=1))

        rec_l, rec_c = rglru_bidir(short_conv(uxl, conv_w[l], conv_b[l]),
                                   short_conv(uxc, conv_w[l], conv_b[l]),
                                   lru_w_r[l], lru_b_r[l], lru_w_i[l], lru_b_i[l], lru_lambda[l])
        rec_l = rec_l * jax.nn.gelu(ugl, approximate=True)

        fou_l = fourier_mix(ufl)

        x_mixed = x + g1l * gated_merge(hl, att_l, rec_l, fou_l, w_branch[l], w_gate[l], b_gate[l], w_out[l])
        if update_ctx:
            qc = rmsnorm(qc.reshape(b, tc, N_Q_HEADS, HEAD_DIM), q_gain[l])
            att_c = attend(qc.reshape(b, tc, N_KV_HEADS, Q_PER_KV, HEAD_DIM), kc, vc).reshape(b, tc, D_BRANCH)
            rec_c = rec_c * jax.nn.gelu(ugc, approximate=True)
            xc = xc + g1c * gated_merge(hc, att_c, rec_c, fourier_mix(ufc),
                                        w_branch[l], w_gate[l], b_gate[l], w_out[l])
        x = x_mixed

        x = x + g2l * swiglu(rmsnorm(x, g_norm2[l]) * (1 + sc2l) + s2l, w_ffn_in[l], w_ffn_out[l])
        if update_ctx:
            xc = xc + g2c * swiglu(rmsnorm(xc, g_norm2[l]) * (1 + sc2c) + s2c, w_ffn_in[l], w_ffn_out[l])
    return rmsnorm(x, g_final)
```

```python
import functools
import math

import jax
import jax.numpy as jnp
from jax import lax
from jax.experimental import pallas as pl
from jax.experimental.pallas import tpu as pltpu

F32 = jnp.float32
BF16 = jnp.bfloat16

HEAD_DIM = 128
Q_PER_KV = 4
GRID_W = 64
ROPE_THETA = 10000.0
NORM_EPS = 1e-6
LRU_C = 8.0
FOURIER_GROUP = 128

LANES = 128
SUBLANES = 8
VMEM_LIMIT_BYTES = 56 * 1024 * 1024

ROW_TILE = 512
ATTN_Q_TILE = 256
LRU_CHUNK = 256
LRU_PITCH = LRU_CHUNK + SUBLANES


def _params(*semantics):
    return pltpu.CompilerParams(dimension_semantics=semantics,
                                vmem_limit_bytes=VMEM_LIMIT_BYTES)


def _dot(a, b):
    return jnp.dot(a, b, preferred_element_type=F32)


def _dot_nt(a, b):
    return lax.dot_general(a, b, (((1,), (1,)), ((), ())), preferred_element_type=F32)


def _rms(x):
    return x * lax.rsqrt(jnp.mean(x * x, axis=-1, keepdims=True) + NORM_EPS)


def _mod_kernel(c_ref, w_ref, b_ref, o_ref):
    c = c_ref[...]
    s = (c * jax.nn.sigmoid(c)).astype(BF16)
    o_ref[...] = _dot(s, w_ref[...].astype(BF16)) + b_ref[...]


def _modulation(cc, w_mod, b_mod):
    depth, d, n = w_mod.shape
    rows = cc.shape[0]
    tn = 1024
    return pl.pallas_call(
        _mod_kernel,
        grid=(depth, n // tn),
        in_specs=[
            pl.BlockSpec((rows, d), lambda l, j: (0, 0)),
            pl.BlockSpec((None, d, tn), lambda l, j: (l, 0, j)),
            pl.BlockSpec((None, 1, tn), lambda l, j: (l, 0, j)),
        ],
        out_specs=pl.BlockSpec((None, rows, tn), lambda l, j: (l, 0, j)),
        out_shape=jax.ShapeDtypeStruct((depth, rows, n), F32),
        compiler_params=_params("parallel", "parallel"),
        name="modulation",
    )(cc, w_mod, b_mod.reshape(depth, 1, n))


def _in_proj_kernel(x_ref, mod_ref, g_ref, w_ref, cos_ref, sin_ref, qg_ref, kg_ref,
                    proj_ref, h_ref, h_scr, *, n_q_heads, n_kv_heads):
    j = pl.program_id(1)

    @pl.when(j == 0)
    def _():
        h = _rms(x_ref[...]) * g_ref[...]
        h = (h * (1.0 + mod_ref[1:2, :]) + mod_ref[0:1, :]).astype(BF16)
        h_scr[...] = h
        h_ref[...] = h

    acc = _dot(h_scr[...], w_ref[...])

    @pl.when(j == 0)
    def _():
        cos = cos_ref[...]
        sin = sin_ref[...]
        q_gain = qg_ref[...] * (HEAD_DIM ** -0.5)
        k_gain = kg_ref[...]
        n_rot = n_q_heads + n_kv_heads
        for hh in range(n_rot):
            cols = slice(hh * HEAD_DIM, (hh + 1) * HEAD_DIM)
            y = _rms(acc[:, cols]) * (q_gain if hh < n_q_heads else k_gain)
            y = y * cos + pltpu.roll(y, HEAD_DIM // 2, axis=1) * sin
            proj_ref[:, cols] = y.astype(BF16)
        proj_ref[:, n_rot * HEAD_DIM:] = acc[:, n_rot * HEAD_DIM:].astype(BF16)

    @pl.when(j > 0)
    def _():
        proj_ref[...] = acc.astype(BF16)


def _in_proj(xall, mod_l, g_norm, w_in, cosf, sinf, q_gain, k_gain, *, cfg):
    rows, d = xall.shape
    d_in = w_in.shape[1]
    tm, tn = ROW_TILE, cfg["tn_in"]
    kern = functools.partial(_in_proj_kernel, n_q_heads=cfg["n_q_heads"],
                             n_kv_heads=cfg["n_kv_heads"])
    return pl.pallas_call(
        kern,
        grid=(rows // tm, d_in // tn),
        in_specs=[
            pl.BlockSpec((tm, d), lambda i, j: (i, 0)),
            pl.BlockSpec((None, 6, d), lambda i, j: (cfg["mod_row"](i), 0, 0)),
            pl.BlockSpec((1, d), lambda i, j: (0, 0)),
            pl.BlockSpec((d, tn), lambda i, j: (0, j)),
            pl.BlockSpec((tm, HEAD_DIM), lambda i, j: (cfg["rope_block"](i), 0)),
            pl.BlockSpec((tm, HEAD_DIM), lambda i, j: (cfg["rope_block"](i), 0)),
            pl.BlockSpec((1, HEAD_DIM), lambda i, j: (0, 0)),
            pl.BlockSpec((1, HEAD_DIM), lambda i, j: (0, 0)),
        ],
        out_specs=[
            pl.BlockSpec((tm, tn), lambda i, j: (i, j)),
            pl.BlockSpec((tm, d), lambda i, j: (i, 0)),
        ],
        out_shape=[
            jax.ShapeDtypeStruct((rows, d_in), BF16),
            jax.ShapeDtypeStruct((rows, d), BF16),
        ],
        scratch_shapes=[pltpu.VMEM((tm, d), BF16)],
        compiler_params=_params("parallel", "arbitrary"),
        name="in_proj",
    )(xall, mod_l, g_norm.reshape(1, d), w_in, cosf, sinf,
      q_gain.reshape(1, HEAD_DIM), k_gain.reshape(1, HEAD_DIM))


def _attn_kernel(q_ref, kl_ref, kc_ref, vl_ref, vc_ref, o_ref, *, n_lat_tiles, with_ctx):
    def attend(latent):
        kc = kc_ref[...]
        vc = vc_ref[...]
        for g in range(Q_PER_KV):
            cols = slice(g * HEAD_DIM, (g + 1) * HEAD_DIM)
            q = q_ref[:, cols]
            sc = _dot_nt(q, kc)
            m = jnp.max(sc, axis=-1, keepdims=True)
            if latent:
                sl = _dot_nt(q, kl_ref[...])
                m = jnp.maximum(m, jnp.max(sl, axis=-1, keepdims=True))
                pl_ = jnp.exp(sl - m)
            pc = jnp.exp(sc - m)
            den = jnp.sum(pc, axis=-1, keepdims=True)
            o = _dot(pc.astype(BF16), vc)
            if latent:
                den = den + jnp.sum(pl_, axis=-1, keepdims=True)
                o = o + _dot(pl_.astype(BF16), vl_ref[...])
            o_ref[:, cols] = (o / den).astype(BF16)

    if with_ctx:
        qi = pl.program_id(2)
        pl.when(qi < n_lat_tiles)(lambda: attend(True))
        pl.when(qi == n_lat_tiles)(lambda: attend(False))
    else:
        attend(True)


def _attention(proj, *, cfg, with_ctx):
    b, t, tc = cfg["b"], cfg["t"], cfg["tc"]
    tq = ATTN_Q_TILE
    assert tc == tq and t % tq == 0
    n_lat_tiles = t // tq
    n_kv = cfg["n_kv_heads"]
    gw = Q_PER_KV * HEAD_DIM
    k_col = cfg["n_q_heads"]
    v_col = k_col + n_kv
    ctx_blk0 = b * t // tc
    rows_out = b * t + (b * tc if with_ctx else 0)

    def q_map(bi, h, qi):
        return (jnp.where(qi < n_lat_tiles, bi * n_lat_tiles + qi, ctx_blk0 + bi), h)

    kern = functools.partial(_attn_kernel, n_lat_tiles=n_lat_tiles, with_ctx=with_ctx)
    return pl.pallas_call(
        kern,
        grid=(b, n_kv, n_lat_tiles + (1 if with_ctx else 0)),
        in_specs=[
            pl.BlockSpec((tq, gw), q_map),
            pl.BlockSpec((t, HEAD_DIM), lambda bi, h, qi: (bi, k_col + h)),
            pl.BlockSpec((tc, HEAD_DIM), lambda bi, h, qi: (ctx_blk0 + bi, k_col + h)),
            pl.BlockSpec((t, HEAD_DIM), lambda bi, h, qi: (bi, v_col + h)),
            pl.BlockSpec((tc, HEAD_DIM), lambda bi, h, qi: (ctx_blk0 + bi, v_col + h)),
        ],
        out_specs=pl.BlockSpec((tq, gw), q_map),
        out_shape=jax.ShapeDtypeStruct((rows_out, cfg["n_q_heads"] * HEAD_DIM), BF16),
        compiler_params=_params("parallel", "parallel", "arbitrary"),
        name="attention",
    )(proj, proj, proj, proj, proj)


def _lru_kernel(ux_ref, ug_ref, cw_ref, cb_ref, w_ref, br_ref, bi_ref, lam_ref, o_ref,
                uc_scr, acc_scr, a_scr, b_scr, y_scr, *, b, t, tc):
    ch = LRU_CHUNK
    n_lat = b * t

    def conv_seq(row0, n):
        u = ux_ref[row0:row0 + n, :].astype(F32)
        r = lax.broadcasted_iota(jnp.int32, (n, LANES), 0)
        um2 = jnp.where(r >= 2, pltpu.roll(u, 2, axis=0), 0.0)
        um1 = jnp.where(r >= 1, pltpu.roll(u, 1, axis=0), 0.0)
        up1 = jnp.where(r < n - 1, pltpu.roll(u, n - 1, axis=0), 0.0)
        uc_scr[row0:row0 + n, :] = (cb_ref[...] + cw_ref[0:1, :] * um2 + cw_ref[1:2, :] * um1
                                    + cw_ref[2:3, :] * u + cw_ref[3:4, :] * up1)

    for bi in range(b):
        conv_seq(bi * t, t)
        conv_seq(n_lat + bi * tc, tc)

    def chunk_rows(base0, stride_b, bi):
        start = base0 + bi * stride_b
        if not isinstance(start, int):
            start = pl.multiple_of(start, ch)
        return pl.ds(start, ch)

    def chunk(base0, stride_b, d, h):
        lam = lam_ref[d:d + 1, :]
        softplus_neg_lam = jnp.maximum(-lam, 0.0) + jnp.log1p(jnp.exp(-jnp.abs(lam)))
        rate = -LRU_C * softplus_neg_lam
        w = w_ref[:, d * 2 * LANES:(d + 1) * 2 * LANES]
        for bi in range(b):
            rows = chunk_rows(base0, stride_b, bi)
            u = uc_scr[rows, :]
            gates = _dot(u.astype(BF16), w)
            r = jax.nn.sigmoid(gates[:, :LANES] + br_ref[d:d + 1, :])
            i = jax.nn.sigmoid(gates[:, LANES:] + bi_ref[d:d + 1, :])
            log_a = r * rate
            a = jnp.exp(log_a)
            one_minus_a2 = -jnp.tanh(log_a) * (a * a + 1.0)
            a_scr[bi * LRU_PITCH:bi * LRU_PITCH + ch, :] = a
            b_scr[bi * LRU_PITCH:bi * LRU_PITCH + ch, :] = jnp.sqrt(one_minus_a2) * i * u

        def step(s, hh):
            ts = s if d == 0 else ch - 1 - s
            a = a_scr[pl.ds(ts, b, stride=LRU_PITCH), :]
            bx = b_scr[pl.ds(ts, b, stride=LRU_PITCH), :]
            hh = a * hh + bx
            y_scr[pl.ds(ts, b, stride=LRU_PITCH), :] = hh
            return hh

        h = lax.fori_loop(0, ch, step, h, unroll=8)

        for bi in range(b):
            rows = chunk_rows(base0, stride_b, bi)
            y = y_scr[bi * LRU_PITCH:bi * LRU_PITCH + ch, :]
            if d == 0:
                acc_scr[rows, :] = y
            else:
                gate = jax.nn.gelu(ug_ref[rows, :].astype(F32), approximate=True)
                o_ref[rows, :] = ((acc_scr[rows, :] + y) * gate).astype(BF16)
        return h

    n_ctx_chunks = tc // ch
    n_lat_chunks = t // ch
    for d in range(2):
        h = jnp.zeros((b, LANES), F32)
        ctx_order = range(n_ctx_chunks) if d == 0 else reversed(range(n_ctx_chunks))
        for kc in ctx_order:
            h = chunk(n_lat + kc * ch, tc, d, h)

        def lat_body(k, hh, d=d):
            kk = k if d == 0 else n_lat_chunks - 1 - k
            return chunk(kk * ch, t, d, hh)

        lax.fori_loop(0, n_lat_chunks, lat_body, h)


def _lru(proj, conv_w, conv_b, w_bd, b_r, b_i, lam, *, cfg):
    rows = proj.shape[0]
    d_lru = conv_w.shape[1]
    ux_blk = cfg["ux_col"] // LANES
    ug_blk = cfg["ug_col"] // LANES
    kern = functools.partial(_lru_kernel, b=cfg["b"], t=cfg["t"], tc=cfg["tc"])
    vec = lambda n: pl.BlockSpec((n, LANES), lambda c: (0, c))
    return pl.pallas_call(
        kern,
        grid=(d_lru // LANES,),
        in_specs=[
            pl.BlockSpec((rows, LANES), lambda c: (0, ux_blk + c), pipeline_mode=pl.Buffered(1)),
            pl.BlockSpec((rows, LANES), lambda c: (0, ug_blk + c), pipeline_mode=pl.Buffered(1)),
            vec(conv_w.shape[0]), vec(1),
            pl.BlockSpec((None, LANES, 4 * LANES), lambda c: (c, 0, 0)),
            vec(2), vec(2), vec(2),
        ],
        out_specs=pl.BlockSpec((rows, LANES), lambda c: (0, c)),
        out_shape=jax.ShapeDtypeStruct((rows, d_lru), BF16),
        scratch_shapes=[
            pltpu.VMEM((rows, LANES), F32),
            pltpu.VMEM((rows, LANES), F32),
            pltpu.VMEM((cfg["b"] * LRU_PITCH, LANES), F32),
            pltpu.VMEM((cfg["b"] * LRU_PITCH, LANES), F32),
            pltpu.VMEM((cfg["b"] * LRU_PITCH, LANES), F32),
        ],
        compiler_params=_params("parallel"),
        name="rglru",
    )(proj, proj, conv_w, conv_b.reshape(1, d_lru), w_bd, b_r, b_i, lam)


def _lru_block_diag(w_r, w_i):
    _, nb, bw, _ = w_r.shape
    per = LANES // bw
    eye = jnp.eye(per, dtype=w_r.dtype)

    def bd(w):
        w = w.reshape(nb // per, per, bw, bw)
        return jnp.einsum("cipq,ij->cipjq", w, eye).reshape(nb // per, LANES, LANES)

    return jnp.concatenate([bd(w_r[0]), bd(w_i[0]), bd(w_r[1]), bd(w_i[1])], axis=-1).astype(BF16)


def _fourier_kernel(u_ref, cs_ref, w2_ref, *rest, t, n_groups, scale):
    o_ref, pq_scr = rest[-2], rest[-1]
    for g in range(n_groups):
        cols = slice(g * FOURIER_GROUP, (g + 1) * FOURIER_GROUP)
        pq = _dot(u_ref[:, cols], w2_ref[...])
        pq_scr[0:t, cols] = pq[:, :FOURIER_GROUP].astype(BF16)
        pq_scr[t:2 * t, cols] = pq[:, FOURIER_GROUP:].astype(BF16)
    o_ref[...] = (_dot(cs_ref[...], pq_scr[...]) * scale).astype(BF16)


def _dft_tables(t):
    def angles(n):
        k = jnp.arange(n, dtype=jnp.int32)
        return ((k[:, None] * k[None, :]) % n).astype(F32) * (2.0 * math.pi / n)
    at = angles(t)
    ac = angles(FOURIER_GROUP)
    cs = jnp.concatenate([jnp.cos(at), -jnp.sin(at)], axis=1).astype(BF16)
    w2 = jnp.concatenate([jnp.cos(ac), jnp.sin(ac)], axis=1).astype(BF16)
    return cs, w2


def _fourier(proj, cs, w2, *, cfg, seq_len, row_blk0, rows_out, prev=None):
    d_f = cfg["d_branch"]
    wcols = min(d_f, 512)
    uf_blk = cfg["uf_col"] // wcols
    kern = functools.partial(_fourier_kernel, t=seq_len, n_groups=wcols // FOURIER_GROUP,
                             scale=(seq_len * FOURIER_GROUP) ** -0.5)
    in_specs = [
        pl.BlockSpec((seq_len, wcols), lambda bi, hf: (row_blk0 + bi, uf_blk + hf)),
        pl.BlockSpec((seq_len, 2 * seq_len), lambda bi, hf: (0, 0), pipeline_mode=pl.Buffered(1)),
        pl.BlockSpec((FOURIER_GROUP, 2 * FOURIER_GROUP), lambda bi, hf: (0, 0)),
    ]
    args = [proj, cs, w2]
    aliases = {}
    if prev is not None:
        in_specs.append(pl.BlockSpec(memory_space=pl.ANY))
        args.append(prev)
        aliases = {3: 0}
    return pl.pallas_call(
        kern,
        grid=(cfg["b"], d_f // wcols),
        in_specs=in_specs,
        out_specs=pl.BlockSpec((seq_len, wcols), lambda bi, hf: (row_blk0 + bi, hf)),
        out_shape=jax.ShapeDtypeStruct((rows_out, d_f), BF16),
        scratch_shapes=[pltpu.VMEM((2 * seq_len, wcols), BF16)],
        input_output_aliases=aliases,
        compiler_params=_params("parallel", "parallel"),
        name="fourier",
    )(*args)


def _merge_kernel(h_ref, ya_ref, yr_ref, yf_ref, wg0, wg1, wg2, bg0, bg1, bg2,
                  wb0, wb1, wb2, o_ref):
    h = h_ref[...]
    acc = None
    for y_ref, wg, bg, wb in ((ya_ref, wg0, bg0, wb0), (yr_ref, wg1, bg1, wb1),
                              (yf_ref, wg2, bg2, wb2)):
        gate = jax.nn.sigmoid(_dot(h, wg[...]) + bg[...])
        term = gate * _dot(y_ref[...], wb[...])
        acc = term if acc is None else acc + term
    o_ref[...] = acc.astype(BF16)


def _merge(h, att, rec, fou, w_gate, b_gate, w_branch, *, rows):
    d = h.shape[1]
    d_b = att.shape[1]
    tm, tn = ROW_TILE, 512
    nj = d // tn
    y_spec = pl.BlockSpec((tm, d_b), lambda i, j: (i, 0))
    wg_spec = lambda k: pl.BlockSpec((d, tn), lambda i, j: (0, k * nj + j))
    bg_spec = lambda k: pl.BlockSpec((1, tn), lambda i, j: (0, k * nj + j))
    wb_spec = lambda k: pl.BlockSpec((None, d_b, tn), lambda i, j: (k, 0, j))
    bg = b_gate.reshape(1, -1)
    return pl.pallas_call(
        _merge_kernel,
        grid=(rows // tm, nj),
        in_specs=[pl.BlockSpec((tm, d), lambda i, j: (i, 0)), y_spec, y_spec, y_spec,
                  wg_spec(0), wg_spec(1), wg_spec(2), bg_spec(0), bg_spec(1), bg_spec(2),
                  wb_spec(0), wb_spec(1), wb_spec(2)],
        out_specs=pl.BlockSpec((tm, tn), lambda i, j: (i, j)),
        out_shape=jax.ShapeDtypeStruct((rows, d), BF16),
        compiler_params=_params("parallel", "arbitrary"),
        name="gated_merge",
    )(h, att, rec, fou, w_gate, w_gate, w_gate, bg, bg, bg, w_branch, w_branch, w_branch)


def _out_proj_kernel(x_ref, m_ref, w_ref, mod_ref, o_ref):
    o_ref[...] = x_ref[...] + mod_ref[2:3, :] * _dot(m_ref[...], w_ref[...])


def _out_proj(xall, m, w_out, mod_l, *, cfg, rows):
    d = xall.shape[1]
    tm, tn = ROW_TILE, 1024
    return pl.pallas_call(
        _out_proj_kernel,
        grid=(rows // tm, d // tn),
        in_specs=[
            pl.BlockSpec((tm, tn), lambda i, j: (i, j)),
            pl.BlockSpec((tm, d), lambda i, j: (i, 0)),
            pl.BlockSpec((d, tn), lambda i, j: (0, j)),
            pl.BlockSpec((None, 6, tn), lambda i, j: (cfg["mod_row"](i), 0, j)),
        ],
        out_specs=pl.BlockSpec((tm, tn), lambda i, j: (i, j)),
        out_shape=jax.ShapeDtypeStruct((rows, d), F32),
        compiler_params=_params("parallel", "arbitrary"),
        name="out_proj",
    )(xall, m, w_out, mod_l)


def _ffn_kernel(x_ref, mod_ref, g_ref, wg_ref, wu_ref, wo_ref, gf_ref, o_ref, h_scr, *, final):
    k = pl.program_id(1)

    @pl.when(k == 0)
    def _():
        h = _rms(x_ref[...]) * g_ref[...]
        h_scr[...] = (h * (1.0 + mod_ref[4:5, :]) + mod_ref[3:4, :]).astype(BF16)

    h = h_scr[...]
    gt = _dot(h, wg_ref[...])
    up = _dot(h, wu_ref[...])
    part = _dot((gt * jax.nn.sigmoid(gt) * up).astype(BF16), wo_ref[...])

    @pl.when(k == 0)
    def _():
        o_ref[...] = part

    @pl.when(k > 0)
    def _():
        o_ref[...] += part

    @pl.when(k == pl.num_programs(1) - 1)
    def _():
        y = x_ref[...] + mod_ref[5:6, :] * o_ref[...]
        if final:
            y = _rms(y) * gf_ref[...]
        o_ref[...] = y


def _ffn(xmid, mod_l, g_norm, w_ffn_in, w_ffn_out, g_final, *, cfg, rows, final):
    d = xmid.shape[1]
    d_ff = w_ffn_out.shape[0]
    tm, tf = ROW_TILE, cfg["tf"]
    nk = d_ff // tf
    return pl.pallas_call(
        functools.partial(_ffn_kernel, final=final),
        grid=(rows // tm, nk),
        in_specs=[
            pl.BlockSpec((tm, d), lambda i, k: (i, 0)),
            pl.BlockSpec((None, 6, d), lambda i, k: (cfg["mod_row"](i), 0, 0)),
            pl.BlockSpec((1, d), lambda i, k: (0, 0)),
            pl.BlockSpec((d, tf), lambda i, k: (0, k)),
            pl.BlockSpec((d, tf), lambda i, k: (0, nk + k)),
            pl.BlockSpec((tf, d), lambda i, k: (k, 0)),
            pl.BlockSpec((1, d), lambda i, k: (0, 0)),
        ],
        out_specs=pl.BlockSpec((tm, d), lambda i, k: (i, 0)),
        out_shape=jax.ShapeDtypeStruct((rows, d), F32),
        scratch_shapes=[pltpu.VMEM((tm, d), BF16)],
        compiler_params=_params("parallel", "arbitrary"),
        name="swiglu_ffn",
    )(xmid, mod_l, g_norm.reshape(1, d), w_ffn_in, w_ffn_in, w_ffn_out, g_final.reshape(1, d))


def _rope_tables(t, tile_rows):
    pairs = HEAD_DIM // 4
    rows = t // GRID_W
    row = jnp.repeat(jnp.arange(rows, dtype=F32), GRID_W)
    col = jnp.tile(jnp.arange(GRID_W, dtype=F32), rows)
    inv = ROPE_THETA ** (-jnp.arange(pairs, dtype=F32) / pairs)
    ang = jnp.concatenate([row[:, None] * inv, col[:, None] * inv], axis=-1)
    cos, sin = jnp.cos(ang), jnp.sin(ang)
    cosf = jnp.concatenate([cos, cos], axis=-1)
    sinf = jnp.concatenate([-sin, sin], axis=-1)
    cosf = jnp.concatenate([cosf, jnp.ones((tile_rows, HEAD_DIM), F32)], axis=0)
    sinf = jnp.concatenate([sinf, jnp.zeros((tile_rows, HEAD_DIM), F32)], axis=0)
    return cosf, sinf


def kernel(x, c, ctx, c_ctx, w_mod, b_mod, g_norm1, g_norm2, w_in, q_gain, k_gain, conv_w, conv_b, lru_w_r, lru_b_r, lru_w_i, lru_b_i, lru_lambda, w_branch, w_gate, b_gate, w_out, w_ffn_in, w_ffn_out, g_final):
    b, t, d = x.shape
    tc = ctx.shape[1]
    depth = w_in.shape[0]
    d_branch = w_branch.shape[2]
    n_q_heads = d_branch // HEAD_DIM
    n_kv_heads = n_q_heads // Q_PER_KV
    d_kv = n_kv_heads * HEAD_DIM
    d_ff = w_ffn_out.shape[1]
    n_lat, n_ctx = b * t, b * tc
    assert b == SUBLANES and t % ROW_TILE == 0 and n_ctx % ROW_TILE == 0
    assert tc % LRU_CHUNK == 0 and t % LRU_CHUNK == 0
    tiles_per_batch = t // ROW_TILE
    n_lat_tiles = n_lat // ROW_TILE

    cfg = dict(
        b=b, t=t, tc=tc, d_branch=d_branch, n_q_heads=n_q_heads, n_kv_heads=n_kv_heads,
        tn_in=d_branch + 2 * d_kv,
        ux_col=d_branch + 2 * d_kv, ug_col=2 * d_branch + 2 * d_kv, uf_col=3 * d_branch + 2 * d_kv,
        tf=512 if d_ff % 512 == 0 else 256,
        mod_row=lambda i: jnp.minimum(i // tiles_per_batch, b),
        rope_block=lambda i: jnp.where(i < n_lat_tiles, i % tiles_per_batch, tiles_per_batch),
    )

    xall = jnp.concatenate([x.reshape(n_lat, d), ctx.reshape(n_ctx, d)], axis=0)
    cc = jnp.concatenate([c, c_ctx[None, :], jnp.zeros((2 * SUBLANES - b - 1, d), F32)], axis=0)
    mod = _modulation(cc, w_mod, b_mod).reshape(depth, 2 * SUBLANES, 6, d)

    w_in_b = w_in.astype(BF16)
    w_gate_b = w_gate.astype(BF16)
    w_branch_b = w_branch.astype(BF16)
    w_out_b = w_out.astype(BF16)
    w_ffn_in_b = w_ffn_in.astype(BF16)
    w_ffn_out_b = w_ffn_out.astype(BF16)

    cosf, sinf = _rope_tables(t, ROW_TILE)
    cs_lat, w2 = _dft_tables(t)
    cs_ctx, _ = _dft_tables(tc)

    out = None
    for l in range(depth):
        last = l == depth - 1
        rows = n_lat if last else n_lat + n_ctx
        proj, h = _in_proj(xall, mod[l], g_norm1[l], w_in_b[l], cosf, sinf, q_gain[l], k_gain[l],
                           cfg=cfg)
        att = _attention(proj, cfg=cfg, with_ctx=not last)
        rec = _lru(proj, conv_w[l], conv_b[l], _lru_block_diag(lru_w_r[l], lru_w_i[l]),
                   lru_b_r[l], lru_b_i[l], lru_lambda[l], cfg=cfg)
        fou = _fourier(proj, cs_lat, w2, cfg=cfg, seq_len=t, row_blk0=0, rows_out=rows)
        if not last:
            fou = _fourier(proj, cs_ctx, w2, cfg=cfg, seq_len=tc, row_blk0=n_lat // tc,
                           rows_out=rows, prev=fou)
        m = _merge(h, att, rec, fou, w_gate_b[l], b_gate[l], w_branch_b[l], rows=rows)
        xmid = _out_proj(xall, m, w_out_b[l], mod[l], cfg=cfg, rows=rows)
        out = _ffn(xmid, mod[l], g_norm2[l], w_ffn_in_b[l], w_ffn_out_b[l], g_final,
                   cfg=cfg, rows=rows, final=last)
        xall = out
    return out.reshape(b, t, d)
```

```python
import functools
import math

import jax
import jax.numpy as jnp
from jax import lax
from jax.experimental import pallas as pl
from jax.experimental.pallas import tpu as pltpu

F32 = jnp.float32
BF16 = jnp.bfloat16

HEAD_DIM = 128
Q_PER_KV = 4
GRID_W = 64
ROPE_THETA = 10000.0
NORM_EPS = 1e-6
LRU_C = 8.0
FOURIER_GROUP = 128

LANES = 128
SUBLANES = 8
VMEM_LIMIT_BYTES = 56 * 1024 * 1024

ROW_TILE = 512
FFN_UP_ROW_TILE = 1024
ATTN_Q_TILE = 512
LRU_CHUNK = 256
LRU_PITCH = LRU_CHUNK + SUBLANES
LRU_GAP = SUBLANES


def _params(*semantics):
    return pltpu.CompilerParams(dimension_semantics=semantics,
                                vmem_limit_bytes=VMEM_LIMIT_BYTES)


def _dot(a, b):
    return jnp.dot(a, b, preferred_element_type=F32)


def _dot_nt(a, b):
    return lax.dot_general(a, b, (((1,), (1,)), ((), ())), preferred_element_type=F32)


def _rms(x):
    return x * lax.rsqrt(jnp.mean(x * x, axis=-1, keepdims=True) + NORM_EPS)


def _ada_norm(x, gain, shift, scale):
    return (_rms(x) * gain * (1.0 + scale) + shift).astype(BF16)


def _mod_kernel(c_ref, w_ref, b_ref, o_ref):
    c = c_ref[...]
    s = (c * jax.nn.sigmoid(c)).astype(BF16)
    o_ref[...] = _dot(s, w_ref[...].astype(BF16)) + b_ref[...]


def _modulation(cc, w_mod, b_mod):
    depth, d, n = w_mod.shape
    rows = cc.shape[0]
    tn = 1024
    return pl.pallas_call(
        _mod_kernel,
        grid=(depth, n // tn),
        in_specs=[
            pl.BlockSpec((rows, d), lambda l, j: (0, 0)),
            pl.BlockSpec((None, d, tn), lambda l, j: (l, 0, j)),
            pl.BlockSpec((None, 1, tn), lambda l, j: (l, 0, j)),
        ],
        out_specs=pl.BlockSpec((None, rows, tn), lambda l, j: (l, 0, j)),
        out_shape=jax.ShapeDtypeStruct((depth, rows, n), F32),
        compiler_params=_params("parallel", "parallel"),
        name="modulation",
    )(cc, w_mod, b_mod.reshape(depth, 1, n))


def _prenorm_kernel(x_ref, mod_ref, g_ref, h_ref):
    h_ref[...] = _ada_norm(x_ref[...], g_ref[...], mod_ref[0:1, :], mod_ref[1:2, :])


def _prenorm(xall, mod_l, g_norm, *, cfg):
    rows, d = xall.shape
    tm = ROW_TILE
    return pl.pallas_call(
        _prenorm_kernel,
        grid=(rows // tm,),
        in_specs=[
            pl.BlockSpec((tm, d), lambda i: (i, 0)),
            pl.BlockSpec((None, 6, d), lambda i: (cfg["mod_row"](i), 0, 0)),
            pl.BlockSpec((1, d), lambda i: (0, 0)),
        ],
        out_specs=pl.BlockSpec((tm, d), lambda i: (i, 0)),
        out_shape=jax.ShapeDtypeStruct((rows, d), BF16),
        compiler_params=_params("parallel"),
        name="prenorm",
    )(xall, mod_l, g_norm.reshape(1, d))


def _in_proj_kernel(h_ref, w_ref, cos_ref, sin_ref, qg_ref, kg_ref, qkv_ref, rest_ref, qkv_scr,
                    *, n_q_heads, n_kv_heads):
    j = pl.program_id(1)

    @pl.when(j == 0)
    def _():
        qkv_scr[...] = _dot(h_ref[...], w_ref[...])

    @pl.when(j == 1)
    def _():
        rest_ref[...] = _dot(h_ref[...], w_ref[...]).astype(BF16)
        cos = cos_ref[...]
        sin = sin_ref[...]
        q_gain = qg_ref[...] * (HEAD_DIM ** -0.5)
        k_gain = kg_ref[...]
        n_rot = n_q_heads + n_kv_heads
        for hh in range(n_rot):
            cols = slice(hh * HEAD_DIM, (hh + 1) * HEAD_DIM)
            y = _rms(qkv_scr[:, cols]) * (q_gain if hh < n_q_heads else k_gain)
            y = y * cos + pltpu.roll(y, HEAD_DIM // 2, axis=1) * sin
            qkv_ref[:, cols] = y.astype(BF16)
        qkv_ref[:, n_rot * HEAD_DIM:] = qkv_scr[:, n_rot * HEAD_DIM:].astype(BF16)

    @pl.when(j > 1)
    def _():
        rest_ref[...] = _dot(h_ref[...], w_ref[...]).astype(BF16)


def _in_proj(h1, w_in, l, cosf, sinf, q_gain, k_gain, *, cfg):
    rows, d = h1.shape
    d_in = w_in.shape[2]
    tm, tn = ROW_TILE, cfg["tn_in"]
    assert d_in == 3 * tn
    kern = functools.partial(_in_proj_kernel, n_q_heads=cfg["n_q_heads"],
                             n_kv_heads=cfg["n_kv_heads"])
    return pl.pallas_call(
        kern,
        grid=(rows // tm, d_in // tn),
        in_specs=[
            pl.BlockSpec((tm, d), lambda i, j: (i, 0)),
            pl.BlockSpec((None, d, tn), lambda i, j: (l, 0, j)),
            pl.BlockSpec((tm, HEAD_DIM), lambda i, j: (cfg["rope_block"](i), 0)),
            pl.BlockSpec((tm, HEAD_DIM), lambda i, j: (cfg["rope_block"](i), 0)),
            pl.BlockSpec((1, HEAD_DIM), lambda i, j: (0, 0)),
            pl.BlockSpec((1, HEAD_DIM), lambda i, j: (0, 0)),
        ],
        out_specs=[
            pl.BlockSpec((tm, tn), lambda i, j: (i, 0)),
            pl.BlockSpec((tm, tn), lambda i, j: (i, jnp.maximum(j - 1, 0))),
        ],
        out_shape=[
            jax.ShapeDtypeStruct((rows, tn), BF16),
            jax.ShapeDtypeStruct((rows, d_in - tn), BF16),
        ],
        scratch_shapes=[pltpu.VMEM((tm, tn), F32)],
        compiler_params=_params("parallel", "arbitrary"),
        name="in_proj",
    )(h1, w_in, cosf, sinf, q_gain.reshape(1, HEAD_DIM), k_gain.reshape(1, HEAD_DIM))


def _attn_kernel(q_ref, *refs, latent):
    if latent:
        kl_ref, kc_ref, vl_ref, vc_ref = refs[:4]
    else:
        kc_ref, vc_ref = refs[:2]
    o_ref = refs[-1]
    kc = kc_ref[...]
    vc = vc_ref[...]
    for g in range(Q_PER_KV):
        cols = slice(g * HEAD_DIM, (g + 1) * HEAD_DIM)
        q = q_ref[:, cols]
        sc = _dot_nt(q, kc)
        m = jnp.max(sc, axis=-1, keepdims=True)
        if latent:
            sl = _dot_nt(q, kl_ref[...])
            m = jnp.maximum(m, jnp.max(sl, axis=-1, keepdims=True))
            pl_ = jnp.exp(sl - m)
        pc = jnp.exp(sc - m)
        den = jnp.sum(pc, axis=-1, keepdims=True)
        o = _dot(pc.astype(BF16), vc)
        if latent:
            den = den + jnp.sum(pl_, axis=-1, keepdims=True)
            o = o + _dot(pl_.astype(BF16), vl_ref[...])
        o_ref[:, cols] = (o / den).astype(BF16)


def _attention(qkv, *, cfg, rows_out, prev=None):
    b, t, tc = cfg["b"], cfg["t"], cfg["tc"]
    latent = prev is None
    tq = ATTN_Q_TILE if latent else tc
    n_q_tiles = t // tq if latent else 1
    q_blk0 = 0 if latent else b * t // tc
    n_kv = cfg["n_kv_heads"]
    gw = Q_PER_KV * HEAD_DIM
    k_col = cfg["n_q_heads"]
    v_col = k_col + n_kv
    ctx_blk0 = b * t // tc

    q_spec = pl.BlockSpec((tq, gw), lambda bi, h, qi: (q_blk0 + bi * n_q_tiles + qi, h))
    lat_spec = lambda col: pl.BlockSpec((t, HEAD_DIM), lambda bi, h, qi: (bi, col + h))
    ctx_spec = lambda col: pl.BlockSpec((tc, HEAD_DIM), lambda bi, h, qi: (ctx_blk0 + bi, col + h))
    if latent:
        in_specs = [q_spec, lat_spec(k_col), ctx_spec(k_col), lat_spec(v_col), ctx_spec(v_col)]
        args, aliases = [qkv] * 5, {}
    else:
        in_specs = [q_spec, ctx_spec(k_col), ctx_spec(v_col), pl.BlockSpec(memory_space=pl.ANY)]
        args, aliases = [qkv, qkv, qkv, prev], {3: 0}
    return pl.pallas_call(
        functools.partial(_attn_kernel, latent=latent),
        grid=(b, n_kv, n_q_tiles),
        in_specs=in_specs,
        out_specs=q_spec,
        out_shape=jax.ShapeDtypeStruct((rows_out, cfg["n_q_heads"] * HEAD_DIM), BF16),
        input_output_aliases=aliases,
        compiler_params=_params("parallel", "parallel", "arbitrary"),
        name="attention",
    )(*args)


def _lru_kernel(ux_ref, ug_ref, cw_ref, cb_ref, w_ref, br_ref, bi_ref, lam_ref, o_ref,
                u_scr, acc_scr, af_scr, bf_scr, yf_scr, ab_scr, bb_scr, yb_scr, *, b, t, tc):
    ch, gap, pitch = LRU_CHUNK, LRU_GAP, LRU_PITCH
    n_lat = b * t
    conv_width = cw_ref.shape[0]
    conv_left = conv_width // 2

    lat = (gap, 0, t + gap, t)
    ctx = (gap + b * (t + gap), n_lat, tc + gap, tc)

    zeros_gap = jnp.zeros((gap, LANES), F32)
    for kind, n in ((lat, t), (ctx, tc)):
        for bi in range(b):
            u0 = kind[0] + bi * kind[2]
            r0 = kind[1] + bi * kind[3]
            u_scr[u0 - gap:u0, :] = zeros_gap
            u_scr[u0:u0 + n, :] = ux_ref[r0:r0 + n, :].astype(F32)
    u_end = ctx[0] + b * ctx[2] - gap
    u_scr[u_end:u_end + gap, :] = zeros_gap

    def slab_rows(kind, bi, off):
        start = kind[1] + bi * kind[3] + off
        if not isinstance(start, int):
            start = pl.multiple_of(start, ch)
        return pl.ds(start, ch)

    def coefficients(d, kind, off, a_scr, b_scr):
        lam = lam_ref[d:d + 1, :]
        softplus_neg_lam = jnp.maximum(-lam, 0.0) + jnp.log1p(jnp.exp(-jnp.abs(lam)))
        half_rate = -0.5 * LRU_C * softplus_neg_lam
        w_half = w_ref[:, d * 2 * LANES:(d + 1) * 2 * LANES] * 0.5
        br_half = 0.5 * br_ref[d:d + 1, :]
        bi_half = 0.5 * bi_ref[d:d + 1, :]
        for bi in range(b):
            s0 = kind[0] + bi * kind[2] + off - conv_left
            u = cb_ref[...]
            for j in range(conv_width):
                u = u + cw_ref[j:j + 1, :] * u_scr[pl.ds(s0 + j, ch), :]
            gates = _dot(u.astype(BF16), w_half)
            log_a = jnp.tanh(gates[:, :LANES] + br_half) * half_rate + half_rate
            i_gate = 0.5 * jnp.tanh(gates[:, LANES:] + bi_half) + 0.5
            a = jnp.exp(log_a)
            one_minus_a2 = -jnp.tanh(log_a) * (a * a + 1.0)
            root = jnp.where(one_minus_a2 > 0.0, one_minus_a2 * lax.rsqrt(one_minus_a2), 0.0)
            a_scr[bi * pitch:bi * pitch + ch, :] = a
            b_scr[bi * pitch:bi * pitch + ch, :] = root * i_gate * u

    def emit(rows, y):
        gate = jax.nn.gelu(ug_ref[rows, :].astype(F32), approximate=True)
        o_ref[rows, :] = (y * gate).astype(BF16)

    def pair(kind, f_off, b_off, carry, mode):
        coefficients(0, kind, f_off, af_scr, bf_scr)
        coefficients(1, kind, b_off, ab_scr, bb_scr)

        def step(s, hs):
            hf, hb = hs
            sf = pl.ds(s, b, stride=pitch)
            sb = pl.ds(ch - 1 - s, b, stride=pitch)
            hf = af_scr[sf, :] * hf + bf_scr[sf, :]
            hb = ab_scr[sb, :] * hb + bb_scr[sb, :]
            yf_scr[sf, :] = hf
            yb_scr[sb, :] = hb
            return hf, hb

        carry = lax.fori_loop(0, ch, step, carry, unroll=8)

        for bi in range(b):
            blk = slice(bi * pitch, bi * pitch + ch)
            rf = slab_rows(kind, bi, f_off)
            rb = slab_rows(kind, bi, b_off)
            if mode == "same":
                emit(rf, yf_scr[blk, :] + yb_scr[blk, :])
            elif mode == "first":
                acc_scr[rf, :] = yf_scr[blk, :]
                acc_scr[rb, :] = yb_scr[blk, :]
            else:
                emit(rf, acc_scr[rf, :] + yf_scr[blk, :])
                emit(rb, acc_scr[rb, :] + yb_scr[blk, :])
        return carry

    n_lat_chunks = t // ch
    half = n_lat_chunks // 2
    zero = jnp.zeros((b, LANES), F32)
    carry = pair(ctx, 0, 0, (zero, zero), "same")
    carry = lax.fori_loop(
        0, half,
        lambda q, hs: pair(lat, q * ch, (n_lat_chunks - 1 - q) * ch, hs, "first"), carry)
    lax.fori_loop(
        0, half,
        lambda q, hs: pair(lat, (half + q) * ch, (half - 1 - q) * ch, hs, "second"), carry)


def _lru(rest, conv_w, conv_b, w_bd, b_r, b_i, lam, *, cfg):
    rows = rest.shape[0]
    d_lru = conv_w.shape[1]
    ux_blk = cfg["ux_col"] // LANES
    ug_blk = cfg["ug_col"] // LANES
    nb = cfg["b"]
    assert cfg["tc"] == LRU_CHUNK and (cfg["t"] // LRU_CHUNK) % 2 == 0
    kern = functools.partial(_lru_kernel, b=nb, t=cfg["t"], tc=cfg["tc"])
    vec = lambda n: pl.BlockSpec((n, LANES), lambda c: (0, c))
    return pl.pallas_call(
        kern,
        grid=(d_lru // LANES,),
        in_specs=[
            pl.BlockSpec((rows, LANES), lambda c: (0, ux_blk + c), pipeline_mode=pl.Buffered(1)),
            pl.BlockSpec((rows, LANES), lambda c: (0, ug_blk + c), pipeline_mode=pl.Buffered(1)),
            vec(conv_w.shape[0]), vec(1),
            pl.BlockSpec((None, LANES, 4 * LANES), lambda c: (c, 0, 0)),
            vec(2), vec(2), vec(2),
        ],
        out_specs=pl.BlockSpec((rows, LANES), lambda c: (0, c)),
        out_shape=jax.ShapeDtypeStruct((rows, d_lru), BF16),
        scratch_shapes=[
            pltpu.VMEM((rows + (2 * nb + 1) * LRU_GAP, LANES), F32),
            pltpu.VMEM((rows, LANES), F32),
        ] + [pltpu.VMEM((nb * LRU_PITCH, LANES), F32)] * 6,
        compiler_params=_params("parallel"),
        name="rglru",
    )(rest, rest, conv_w, conv_b.reshape(1, d_lru), w_bd, b_r, b_i, lam)


def _lru_block_diag(w_r, w_i):
    _, nb, bw, _ = w_r.shape
    per = LANES // bw
    eye = jnp.eye(per, dtype=w_r.dtype)

    def bd(w):
        w = w.reshape(nb // per, per, bw, bw)
        return jnp.einsum("cipq,ij->cipjq", w, eye).reshape(nb // per, LANES, LANES)

    return jnp.concatenate([bd(w_r[0]), bd(w_i[0]), bd(w_r[1]), bd(w_i[1])], axis=-1).astype(BF16)


def _fourier_kernel(u_ref, cs_ref, w2_ref, *rest, t, n_groups, scale):
    o_ref, pq_scr = rest[-2], rest[-1]
    for g in range(n_groups):
        cols = slice(g * FOURIER_GROUP, (g + 1) * FOURIER_GROUP)
        pq = _dot(u_ref[:, cols], w2_ref[...])
        pq_scr[0:t, cols] = pq[:, :FOURIER_GROUP].astype(BF16)
        pq_scr[t:2 * t, cols] = pq[:, FOURIER_GROUP:].astype(BF16)
    o_ref[...] = (_dot(cs_ref[...], pq_scr[...]) * scale).astype(BF16)


def _dft_tables(t):
    def angles(n):
        k = jnp.arange(n, dtype=jnp.int32)
        return ((k[:, None] * k[None, :]) % n).astype(F32) * (2.0 * math.pi / n)
    at = angles(t)
    ac = angles(FOURIER_GROUP)
    cs = jnp.concatenate([jnp.cos(at), -jnp.sin(at)], axis=1).astype(BF16)
    w2 = jnp.concatenate([jnp.cos(ac), jnp.sin(ac)], axis=1).astype(BF16)
    return cs, w2


def _fourier(rest, cs, w2, *, cfg, seq_len, row_blk0, rows_out, prev=None):
    d_f = cfg["d_branch"]
    wcols = min(d_f, 512)
    uf_blk = cfg["uf_col"] // wcols
    kern = functools.partial(_fourier_kernel, t=seq_len, n_groups=wcols // FOURIER_GROUP,
                             scale=(seq_len * FOURIER_GROUP) ** -0.5)
    in_specs = [
        pl.BlockSpec((seq_len, wcols), lambda bi, hf: (row_blk0 + bi, uf_blk + hf)),
        pl.BlockSpec((seq_len, 2 * seq_len), lambda bi, hf: (0, 0), pipeline_mode=pl.Buffered(1)),
        pl.BlockSpec((FOURIER_GROUP, 2 * FOURIER_GROUP), lambda bi, hf: (0, 0)),
    ]
    args = [rest, cs, w2]
    aliases = {}
    if prev is not None:
        in_specs.append(pl.BlockSpec(memory_space=pl.ANY))
        args.append(prev)
        aliases = {3: 0}
    return pl.pallas_call(
        kern,
        grid=(cfg["b"], d_f // wcols),
        in_specs=in_specs,
        out_specs=pl.BlockSpec((seq_len, wcols), lambda bi, hf: (row_blk0 + bi, hf)),
        out_shape=jax.ShapeDtypeStruct((rows_out, d_f), BF16),
        scratch_shapes=[pltpu.VMEM((2 * seq_len, wcols), BF16)],
        input_output_aliases=aliases,
        compiler_params=_params("parallel", "parallel"),
        name="fourier",
    )(*args)


def _merge_kernel(h_ref, ya_ref, yr_ref, yf_ref, wg0, wg1, wg2, bg0, bg1, bg2,
                  wb0, wb1, wb2, o_ref):
    h = h_ref[...]
    acc = None
    for y_ref, wg, bg, wb in ((ya_ref, wg0, bg0, wb0), (yr_ref, wg1, bg1, wb1),
                              (yf_ref, wg2, bg2, wb2)):
        gate = jax.nn.sigmoid(_dot(h, wg[...]) + bg[...])
        term = gate * _dot(y_ref[...], wb[...])
        acc = term if acc is None else acc + term
    o_ref[...] = acc.astype(BF16)


def _merge(h, att, rec, fou, w_gate, b_gate, w_branch, l, *, rows):
    d = h.shape[1]
    d_b = att.shape[1]
    tm, tn = ROW_TILE, 512
    nj = d // tn
    y_spec = pl.BlockSpec((tm, d_b), lambda i, j: (i, 0))
    wg_spec = lambda k: pl.BlockSpec((None, d, tn), lambda i, j: (l, 0, k * nj + j))
    bg_spec = lambda k: pl.BlockSpec((None, 1, tn), lambda i, j: (l, 0, k * nj + j))
    wb_spec = lambda k: pl.BlockSpec((None, None, d_b, tn), lambda i, j: (l, k, 0, j))
    bg = b_gate.reshape(b_gate.shape[0], 1, -1)
    return pl.pallas_call(
        _merge_kernel,
        grid=(rows // tm, nj),
        in_specs=[pl.BlockSpec((tm, d), lambda i, j: (i, 0)), y_spec, y_spec, y_spec,
                  wg_spec(0), wg_spec(1), wg_spec(2), bg_spec(0), bg_spec(1), bg_spec(2),
                  wb_spec(0), wb_spec(1), wb_spec(2)],
        out_specs=pl.BlockSpec((tm, tn), lambda i, j: (i, j)),
        out_shape=jax.ShapeDtypeStruct((rows, d), BF16),
        compiler_params=_params("parallel", "arbitrary"),
        name="gated_merge",
    )(h, att, rec, fou, w_gate, w_gate, w_gate, bg, bg, bg, w_branch, w_branch, w_branch)


def _out_proj_kernel(x_ref, m_ref, w_ref, mod_ref, g_ref, o_ref, h_ref):
    y = x_ref[...] + mod_ref[2:3, :] * _dot(m_ref[...], w_ref[...])
    o_ref[...] = y
    h_ref[...] = _ada_norm(y, g_ref[...], mod_ref[3:4, :], mod_ref[4:5, :])


def _out_proj(xall, m, w_out, l, mod_l, g_norm2, *, cfg, rows):
    d = xall.shape[1]
    tm = ROW_TILE
    row_spec = pl.BlockSpec((tm, d), lambda i: (i, 0))
    return pl.pallas_call(
        _out_proj_kernel,
        grid=(rows // tm,),
        in_specs=[
            row_spec, row_spec,
            pl.BlockSpec((None, d, d), lambda i: (l, 0, 0), pipeline_mode=pl.Buffered(1)),
            pl.BlockSpec((None, 6, d), lambda i: (cfg["mod_row"](i), 0, 0)),
            pl.BlockSpec((1, d), lambda i: (0, 0)),
        ],
        out_specs=[row_spec, row_spec],
        out_shape=[jax.ShapeDtypeStruct((rows, d), F32), jax.ShapeDtypeStruct((rows, d), BF16)],
        compiler_params=_params("parallel"),
        name="out_proj",
    )(xall, m, w_out, mod_l, g_norm2.reshape(1, d))


def _ffn_up_kernel(h_ref, wg_ref, wu_ref, a_ref):
    h = h_ref[...]
    gt = _dot(h, wg_ref[...])
    up = _dot(h, wu_ref[...])
    a_ref[...] = (gt * jax.nn.sigmoid(gt) * up).astype(BF16)


def _ffn_up(h2, w_ffn_in, l, *, cfg, rows):
    d = h2.shape[1]
    d_ff = w_ffn_in.shape[2] // 2
    tm, tf = FFN_UP_ROW_TILE, cfg["tf"]
    nk = d_ff // tf
    return pl.pallas_call(
        _ffn_up_kernel,
        grid=(rows // tm, nk),
        in_specs=[
            pl.BlockSpec((tm, d), lambda i, k: (i, 0)),
            pl.BlockSpec((None, d, tf), lambda i, k: (l, 0, k)),
            pl.BlockSpec((None, d, tf), lambda i, k: (l, 0, nk + k)),
        ],
        out_specs=pl.BlockSpec((tm, tf), lambda i, k: (i, k)),
        out_shape=jax.ShapeDtypeStruct((rows, d_ff), BF16),
        compiler_params=_params("parallel", "arbitrary"),
        name="ffn_up",
    )(h2, w_ffn_in, w_ffn_in)


def _ffn_down_kernel(a_ref, x_ref, wo_ref, mod_ref, nmod_ref, g_ref, o_ref, *h_ref, tn, nj, mode):
    j = pl.program_id(1)
    y = x_ref[...] + mod_ref[5:6, :] * _dot(a_ref[...], wo_ref[...])
    for jj in range(nj):
        @pl.when(j == jj)
        def _(jj=jj):
            o_ref[:, jj * tn:(jj + 1) * tn] = y

    @pl.when(j == nj - 1)
    def _():
        if mode == "final":
            o_ref[...] = _rms(o_ref[...]) * g_ref[...]
        else:
            h_ref[0][...] = _ada_norm(o_ref[...], g_ref[...], nmod_ref[0:1, :], nmod_ref[1:2, :])


def _ffn_down(a, xmid, w_ffn_out, l, mod_l, next_mod, gain, *, cfg, rows, mode):
    d = xmid.shape[1]
    d_ff = a.shape[1]
    tm, tn = ROW_TILE, 512
    nj = d // tn
    out_specs = [pl.BlockSpec((tm, d), lambda i, j: (i, 0))]
    out_shape = [jax.ShapeDtypeStruct((rows, d), F32)]
    if mode == "next":
        out_specs.append(pl.BlockSpec((tm, d), lambda i, j: (i, 0)))
        out_shape.append(jax.ShapeDtypeStruct((rows, d), BF16))
    return pl.pallas_call(
        functools.partial(_ffn_down_kernel, tn=tn, nj=nj, mode=mode),
        grid=(rows // tm, nj),
        in_specs=[
            pl.BlockSpec((tm, d_ff), lambda i, j: (i, 0)),
            pl.BlockSpec((tm, tn), lambda i, j: (i, j)),
            pl.BlockSpec((None, d_ff, tn), lambda i, j: (l, 0, j)),
            pl.BlockSpec((None, 6, tn), lambda i, j: (cfg["mod_row"](i), 0, j)),
            pl.BlockSpec((None, 6, d), lambda i, j: (cfg["mod_row"](i), 0, 0)),
            pl.BlockSpec((1, d), lambda i, j: (0, 0)),
        ],
        out_specs=out_specs,
        out_shape=out_shape,
        compiler_params=_params("parallel", "arbitrary"),
        name="ffn_down",
    )(a, xmid, w_ffn_out, mod_l, next_mod, gain.reshape(1, d))


def _rope_tables(t, tile_rows):
    pairs = HEAD_DIM // 4
    rows = t // GRID_W
    row = jnp.repeat(jnp.arange(rows, dtype=F32), GRID_W)
    col = jnp.tile(jnp.arange(GRID_W, dtype=F32), rows)
    inv = ROPE_THETA ** (-jnp.arange(pairs, dtype=F32) / pairs)
    ang = jnp.concatenate([row[:, None] * inv, col[:, None] * inv], axis=-1)
    cos, sin = jnp.cos(ang), jnp.sin(ang)
    cosf = jnp.concatenate([cos, cos], axis=-1)
    sinf = jnp.concatenate([-sin, sin], axis=-1)
    cosf = jnp.concatenate([cosf, jnp.ones((tile_rows, HEAD_DIM), F32)], axis=0)
    sinf = jnp.concatenate([sinf, jnp.zeros((tile_rows, HEAD_DIM), F32)], axis=0)
    return cosf, sinf


def kernel(x, c, ctx, c_ctx, w_mod, b_mod, g_norm1, g_norm2, w_in, q_gain, k_gain, conv_w, conv_b, lru_w_r, lru_b_r, lru_w_i, lru_b_i, lru_lambda, w_branch, w_gate, b_gate, w_out, w_ffn_in, w_ffn_out, g_final):
    b, t, d = x.shape
    tc = ctx.shape[1]
    depth = w_in.shape[0]
    d_branch = w_branch.shape[2]
    n_q_heads = d_branch // HEAD_DIM
    n_kv_heads = n_q_heads // Q_PER_KV
    d_kv = n_kv_heads * HEAD_DIM
    d_ff = w_ffn_out.shape[1]
    n_lat, n_ctx = b * t, b * tc
    assert b == SUBLANES and t % ROW_TILE == 0 and n_ctx % FFN_UP_ROW_TILE == 0
    assert tc % LRU_CHUNK == 0 and t % LRU_CHUNK == 0
    tiles_per_batch = t // ROW_TILE
    n_lat_tiles = n_lat // ROW_TILE

    cfg = dict(
        b=b, t=t, tc=tc, d_branch=d_branch, n_q_heads=n_q_heads, n_kv_heads=n_kv_heads,
        tn_in=d_branch + 2 * d_kv,
        ux_col=0, ug_col=d_branch, uf_col=2 * d_branch,
        tf=512 if d_ff % 512 == 0 else 256,
        mod_row=lambda i: jnp.minimum(i // tiles_per_batch, b),
        rope_block=lambda i: jnp.where(i < n_lat_tiles, i % tiles_per_batch, tiles_per_batch),
    )

    xall = jnp.concatenate([x.reshape(n_lat, d), ctx.reshape(n_ctx, d)], axis=0)
    cc = jnp.concatenate([c, c_ctx[None, :], jnp.zeros((2 * SUBLANES - b - 1, d), F32)], axis=0)
    mod = _modulation(cc, w_mod, b_mod).reshape(depth, 2 * SUBLANES, 6, d)

    w_in_b = w_in.astype(BF16)
    w_gate_b = w_gate.astype(BF16)
    w_branch_b = w_branch.astype(BF16)
    w_out_b = w_out.astype(BF16)
    w_ffn_in_b = w_ffn_in.astype(BF16)
    w_ffn_out_b = w_ffn_out.astype(BF16)

    cosf, sinf = _rope_tables(t, ROW_TILE)
    cs_lat, w2 = _dft_tables(t)
    cs_ctx, _ = _dft_tables(tc)

    h1 = _prenorm(xall, mod[0], g_norm1[0], cfg=cfg)
    out = None
    for l in range(depth):
        last = l == depth - 1
        rows = n_lat if last else n_lat + n_ctx
        qkv, rest = _in_proj(h1, w_in_b, l, cosf, sinf, q_gain[l], k_gain[l], cfg=cfg)
        att = _attention(qkv, cfg=cfg, rows_out=rows)
        if not last:
            att = _attention(qkv, cfg=cfg, rows_out=rows, prev=att)
        rec = _lru(rest, conv_w[l], conv_b[l], _lru_block_diag(lru_w_r[l], lru_w_i[l]),
                   lru_b_r[l], lru_b_i[l], lru_lambda[l], cfg=cfg)
        fou = _fourier(rest, cs_lat, w2, cfg=cfg, seq_len=t, row_blk0=0, rows_out=rows)
        if not last:
            fou = _fourier(rest, cs_ctx, w2, cfg=cfg, seq_len=tc, row_blk0=n_lat // tc,
                           rows_out=rows, prev=fou)
        m = _merge(h1, att, rec, fou, w_gate_b, b_gate, w_branch_b, l, rows=rows)
        xmid, h2 = _out_proj(xall, m, w_out_b, l, mod[l], g_norm2[l], cfg=cfg, rows=rows)
        a = _ffn_up(h2, w_ffn_in_b, l, cfg=cfg, rows=rows)
        if last:
            (out,) = _ffn_down(a, xmid, w_ffn_out_b, l, mod[l], mod[l], g_final,
                               cfg=cfg, rows=rows, mode="final")
        else:
            xall, h1 = _ffn_down(a, xmid, w_ffn_out_b, l, mod[l], mod[l + 1], g_norm1[l + 1],
                                 cfg=cfg, rows=rows, mode="next")
    return out.reshape(b, t, d)
```

```python
import functools
import math

import jax
import jax.numpy as jnp
from jax import lax
from jax.experimental import pallas as pl
from jax.experimental.pallas import tpu as pltpu

F32 = jnp.float32
BF16 = jnp.bfloat16

HEAD_DIM = 128
Q_PER_KV = 4
GRID_W = 64
ROPE_THETA = 10000.0
NORM_EPS = 1e-6
LRU_C = 8.0
FOURIER_GROUP = 128

LANES = 128
SUBLANES = 8
VMEM_LIMIT_BYTES = 56 * 1024 * 1024

ROW_TILE = 512
FFN_UP_ROW_TILE = 1024
ATTN_Q_TILE = 1024
ATTN_SUB_ROWS = 512
LRU_CHUNK = 256
LRU_PITCH = LRU_CHUNK + SUBLANES
LRU_GAP = SUBLANES


def _params(*semantics):
    return pltpu.CompilerParams(dimension_semantics=semantics,
                                vmem_limit_bytes=VMEM_LIMIT_BYTES)


def _dot(a, b):
    return jnp.dot(a, b, preferred_element_type=F32)


def _dot_nt(a, b):
    return lax.dot_general(a, b, (((1,), (1,)), ((), ())), preferred_element_type=F32)


def _rms(x):
    return x * lax.rsqrt(jnp.mean(x * x, axis=-1, keepdims=True) + NORM_EPS)


def _ada_norm(x, gain, shift, scale):
    return (_rms(x) * gain * (1.0 + scale) + shift).astype(BF16)


def _mod_kernel(c_ref, w_ref, b_ref, o_ref):
    c = c_ref[...]
    s = (c * jax.nn.sigmoid(c)).astype(BF16)
    o_ref[...] = _dot(s, w_ref[...].astype(BF16)) + b_ref[...]


def _modulation(cc, w_mod, b_mod):
    depth, d, n = w_mod.shape
    rows = cc.shape[0]
    tn = 1024
    return pl.pallas_call(
        _mod_kernel,
        grid=(depth, n // tn),
        in_specs=[
            pl.BlockSpec((rows, d), lambda l, j: (0, 0)),
            pl.BlockSpec((None, d, tn), lambda l, j: (l, 0, j)),
            pl.BlockSpec((None, 1, tn), lambda l, j: (l, 0, j)),
        ],
        out_specs=pl.BlockSpec((None, rows, tn), lambda l, j: (l, 0, j)),
        out_shape=jax.ShapeDtypeStruct((depth, rows, n), F32),
        compiler_params=_params("parallel", "parallel"),
        name="modulation",
    )(cc, w_mod, b_mod.reshape(depth, 1, n))


def _split_row_specs(n_lat_tiles, tm, d):
    return [pl.BlockSpec((tm, d), lambda i: (jnp.minimum(i, n_lat_tiles - 1), 0)),
            pl.BlockSpec((tm, d), lambda i: (jnp.maximum(i - n_lat_tiles, 0), 0))]


def _for_row_source(x_ref, c_ref, n_lat_tiles, fn):
    i = pl.program_id(0)
    pl.when(i < n_lat_tiles)(lambda: fn(x_ref))
    pl.when(i >= n_lat_tiles)(lambda: fn(c_ref))


def _prenorm_kernel(x_ref, c_ref, mod_ref, g_ref, h_ref, *, n_lat_tiles):
    def emit(r):
        h_ref[...] = _ada_norm(r[...], g_ref[...], mod_ref[0:1, :], mod_ref[1:2, :])
    _for_row_source(x_ref, c_ref, n_lat_tiles, emit)


def _prenorm(x2d, ctx2d, mod_l, g_norm, *, cfg):
    d = x2d.shape[1]
    rows = x2d.shape[0] + ctx2d.shape[0]
    tm = ROW_TILE
    n_lat_tiles = x2d.shape[0] // tm
    return pl.pallas_call(
        functools.partial(_prenorm_kernel, n_lat_tiles=n_lat_tiles),
        grid=(rows // tm,),
        in_specs=_split_row_specs(n_lat_tiles, tm, d) + [
            pl.BlockSpec((None, 6, d), lambda i: (cfg["mod_row"](i), 0, 0)),
            pl.BlockSpec((1, d), lambda i: (0, 0)),
        ],
        out_specs=pl.BlockSpec((tm, d), lambda i: (i, 0)),
        out_shape=jax.ShapeDtypeStruct((rows, d), BF16),
        compiler_params=_params("parallel"),
        name="prenorm",
    )(x2d, ctx2d, mod_l, g_norm.reshape(1, d))


def _in_proj_kernel(h_ref, w_ref, cos_ref, sin_ref, qg_ref, kg_ref, qkv_ref, rest_ref, qkv_scr,
                    *, n_q_heads, n_kv_heads):
    j = pl.program_id(1)

    @pl.when(j == 0)
    def _():
        qkv_scr[...] = _dot(h_ref[...], w_ref[...])

    @pl.when(j == 1)
    def _():
        rest_ref[...] = _dot(h_ref[...], w_ref[...]).astype(BF16)
        cos = cos_ref[...]
        sin = sin_ref[...]
        q_gain = qg_ref[...] * (HEAD_DIM ** -0.5 * math.log2(math.e))
        k_gain = kg_ref[...]
        n_rot = n_q_heads + n_kv_heads
        for hh in range(n_rot):
            cols = slice(hh * HEAD_DIM, (hh + 1) * HEAD_DIM)
            y = _rms(qkv_scr[:, cols]) * (q_gain if hh < n_q_heads else k_gain)
            y = y * cos + pltpu.roll(y, HEAD_DIM // 2, axis=1) * sin
            qkv_ref[:, cols] = y.astype(BF16)
        qkv_ref[:, n_rot * HEAD_DIM:] = qkv_scr[:, n_rot * HEAD_DIM:].astype(BF16)

    @pl.when(j > 1)
    def _():
        rest_ref[...] = _dot(h_ref[...], w_ref[...]).astype(BF16)


def _in_proj(h1, w_in, l, cosf, sinf, q_gain, k_gain, *, cfg):
    rows, d = h1.shape
    d_in = w_in.shape[2]
    tm, tn = ROW_TILE, cfg["tn_in"]
    assert d_in == 3 * tn
    kern = functools.partial(_in_proj_kernel, n_q_heads=cfg["n_q_heads"],
                             n_kv_heads=cfg["n_kv_heads"])
    return pl.pallas_call(
        kern,
        grid=(rows // tm, d_in // tn),
        in_specs=[
            pl.BlockSpec((tm, d), lambda i, j: (i, 0)),
            pl.BlockSpec((None, d, tn), lambda i, j: (l, 0, j)),
            pl.BlockSpec((tm, HEAD_DIM), lambda i, j: (cfg["rope_block"](i), 0)),
            pl.BlockSpec((tm, HEAD_DIM), lambda i, j: (cfg["rope_block"](i), 0)),
            pl.BlockSpec((1, HEAD_DIM), lambda i, j: (0, 0)),
            pl.BlockSpec((1, HEAD_DIM), lambda i, j: (0, 0)),
        ],
        out_specs=[
            pl.BlockSpec((tm, tn), lambda i, j: (i, 0)),
            pl.BlockSpec((tm, tn), lambda i, j: (i, jnp.maximum(j - 1, 0))),
        ],
        out_shape=[
            jax.ShapeDtypeStruct((rows, tn), BF16),
            jax.ShapeDtypeStruct((rows, d_in - tn), BF16),
        ],
        scratch_shapes=[pltpu.VMEM((tm, tn), F32)],
        compiler_params=_params("parallel", "arbitrary"),
        name="in_proj",
    )(h1, w_in, cosf, sinf, q_gain.reshape(1, HEAD_DIM), k_gain.reshape(1, HEAD_DIM))


def _attn_kernel(q_ref, *refs, latent):
    if latent:
        kl_ref, kc_ref, vl_ref, vc_ref, o_ref, vl_aug, vc_aug = refs
    else:
        kc_ref, vc_ref, _, o_ref, vc_aug = refs

    @pl.when(pl.program_id(2) == 0)
    def _():
        for v_ref, aug in ((vl_ref, vl_aug), (vc_ref, vc_aug)) if latent else ((vc_ref, vc_aug),):
            lane = lax.broadcasted_iota(jnp.int32, v_ref.shape, 1)
            aug[:, :HEAD_DIM] = v_ref[...]
            aug[:, HEAD_DIM:] = jnp.where(lane == 0, 1.0, 0.0).astype(BF16)

    kc = kc_ref[...]
    sub = min(q_ref.shape[0], ATTN_SUB_ROWS)
    for r0 in range(0, q_ref.shape[0], sub):
        for g in range(Q_PER_KV):
            cols = slice(g * HEAD_DIM, (g + 1) * HEAD_DIM)
            q = q_ref[r0:r0 + sub, cols]
            sc = _dot_nt(q, kc)
            m = jnp.max(sc, axis=-1, keepdims=True)
            if latent:
                sl = _dot_nt(q, kl_ref[...])
                m = jnp.maximum(m, jnp.max(sl, axis=-1, keepdims=True))
            o = _dot(jnp.exp2((sc - m).astype(BF16)), vc_aug[...])
            if latent:
                o = o + _dot(jnp.exp2((sl - m).astype(BF16)), vl_aug[...])
            o_ref[r0:r0 + sub, cols] = (
                o[:, :HEAD_DIM] / o[:, HEAD_DIM:HEAD_DIM + 1]).astype(BF16)


def _attention(qkv, *, cfg, rows_out, prev=None):
    b, t, tc = cfg["b"], cfg["t"], cfg["tc"]
    latent = prev is None
    tq = ATTN_Q_TILE if latent else tc
    n_q_tiles = t // tq if latent else 1
    q_blk0 = 0 if latent else b * t // tc
    n_kv = cfg["n_kv_heads"]
    gw = Q_PER_KV * HEAD_DIM
    k_col = cfg["n_q_heads"]
    v_col = k_col + n_kv
    ctx_blk0 = b * t // tc

    q_spec = pl.BlockSpec((tq, gw), lambda bi, h, qi: (q_blk0 + bi * n_q_tiles + qi, h))
    lat_spec = lambda col: pl.BlockSpec((t, HEAD_DIM), lambda bi, h, qi: (bi, col + h))
    ctx_spec = lambda col: pl.BlockSpec((tc, HEAD_DIM), lambda bi, h, qi: (ctx_blk0 + bi, col + h))
    aug = lambda n: pltpu.VMEM((n, 2 * HEAD_DIM), BF16)
    if latent:
        in_specs = [q_spec, lat_spec(k_col), ctx_spec(k_col), lat_spec(v_col), ctx_spec(v_col)]
        args, aliases, scratch = [qkv] * 5, {}, [aug(t), aug(tc)]
    else:
        in_specs = [q_spec, ctx_spec(k_col), ctx_spec(v_col), pl.BlockSpec(memory_space=pl.ANY)]
        args, aliases, scratch = [qkv, qkv, qkv, prev], {3: 0}, [aug(tc)]
    return pl.pallas_call(
        functools.partial(_attn_kernel, latent=latent),
        grid=(b, n_kv, n_q_tiles),
        in_specs=in_specs,
        out_specs=q_spec,
        out_shape=jax.ShapeDtypeStruct((rows_out, cfg["n_q_heads"] * HEAD_DIM), BF16),
        scratch_shapes=scratch,
        input_output_aliases=aliases,
        compiler_params=_params("parallel", "parallel", "arbitrary"),
        name="attention",
    )(*args)


def _lru_kernel(ux_ref, ug_ref, cw_ref, cb_ref, w_ref, br_ref, bi_ref, lam_ref, o_ref,
                u_scr, acc_scr, af_scr, bf_scr, yf_scr, ab_scr, bb_scr, yb_scr, *, b, t, tc):
    ch, gap, pitch = LRU_CHUNK, LRU_GAP, LRU_PITCH
    n_lat = b * t
    conv_width = cw_ref.shape[0]
    conv_left = conv_width // 2

    lat = (gap, 0, t + gap, t)
    ctx = (gap + b * (t + gap), n_lat, tc + gap, tc)

    zeros_gap = jnp.zeros((gap, LANES), F32)
    for kind, n in ((lat, t), (ctx, tc)):
        for bi in range(b):
            u0 = kind[0] + bi * kind[2]
            r0 = kind[1] + bi * kind[3]
            u_scr[u0 - gap:u0, :] = zeros_gap
            u_scr[u0:u0 + n, :] = ux_ref[r0:r0 + n, :].astype(F32)
    u_end = ctx[0] + b * ctx[2] - gap
    u_scr[u_end:u_end + gap, :] = zeros_gap

    def slab_rows(kind, bi, off):
        start = kind[1] + bi * kind[3] + off
        if not isinstance(start, int):
            start = pl.multiple_of(start, ch)
        return pl.ds(start, ch)

    def coefficients(d, kind, off, a_scr, b_scr):
        lam = lam_ref[d:d + 1, :]
        softplus_neg_lam = jnp.maximum(-lam, 0.0) + jnp.log1p(jnp.exp(-jnp.abs(lam)))
        half_rate = -0.5 * LRU_C * softplus_neg_lam
        w_half = w_ref[:, d * 2 * LANES:(d + 1) * 2 * LANES] * 0.5
        br_half = 0.5 * br_ref[d:d + 1, :]
        bi_half = 0.5 * bi_ref[d:d + 1, :]
        for bi in range(b):
            s0 = kind[0] + bi * kind[2] + off - conv_left
            u = cb_ref[...]
            for j in range(conv_width):
                u = u + cw_ref[j:j + 1, :] * u_scr[pl.ds(s0 + j, ch), :]
            gates = _dot(u.astype(BF16), w_half)
            log_a = jnp.tanh(gates[:, :LANES] + br_half) * half_rate + half_rate
            i_gate = 0.5 * jnp.tanh(gates[:, LANES:] + bi_half) + 0.5
            a = jnp.exp(log_a)
            one_minus_a2 = -jnp.tanh(log_a) * (a * a + 1.0)
            root = jnp.where(one_minus_a2 > 0.0, one_minus_a2 * lax.rsqrt(one_minus_a2), 0.0)
            a_scr[bi * pitch:bi * pitch + ch, :] = a
            b_scr[bi * pitch:bi * pitch + ch, :] = root * i_gate * u

    def emit(rows, y):
        gate = jax.nn.gelu(ug_ref[rows, :].astype(F32), approximate=True)
        o_ref[rows, :] = (y * gate).astype(BF16)

    def pair(kind, f_off, b_off, carry, mode):
        coefficients(0, kind, f_off, af_scr, bf_scr)
        coefficients(1, kind, b_off, ab_scr, bb_scr)

        def step(s, hs):
            hf, hb = hs
            sf = pl.ds(s, b, stride=pitch)
            sb = pl.ds(ch - 1 - s, b, stride=pitch)
            hf = af_scr[sf, :] * hf + bf_scr[sf, :]
            hb = ab_scr[sb, :] * hb + bb_scr[sb, :]
            yf_scr[sf, :] = hf
            yb_scr[sb, :] = hb
            return hf, hb

        carry = lax.fori_loop(0, ch, step, carry, unroll=8)

        for bi in range(b):
            blk = slice(bi * pitch, bi * pitch + ch)
            rf = slab_rows(kind, bi, f_off)
            rb = slab_rows(kind, bi, b_off)
            if mode == "same":
                emit(rf, yf_scr[blk, :] + yb_scr[blk, :])
            elif mode == "first":
                acc_scr[rf, :] = yf_scr[blk, :]
                acc_scr[rb, :] = yb_scr[blk, :]
            else:
                emit(rf, acc_scr[rf, :] + yf_scr[blk, :])
                emit(rb, acc_scr[rb, :] + yb_scr[blk, :])
        return carry

    n_lat_chunks = t // ch
    half = n_lat_chunks // 2
    zero = jnp.zeros((b, LANES), F32)
    carry = pair(ctx, 0, 0, (zero, zero), "same")
    carry = lax.fori_loop(
        0, half,
        lambda q, hs: pair(lat, q * ch, (n_lat_chunks - 1 - q) * ch, hs, "first"), carry)
    lax.fori_loop(
        0, half,
        lambda q, hs: pair(lat, (half + q) * ch, (half - 1 - q) * ch, hs, "second"), carry)


def _lru(rest, conv_w, conv_b, w_bd, b_r, b_i, lam, *, cfg):
    rows = rest.shape[0]
    d_lru = conv_w.shape[1]
    ux_blk = cfg["ux_col"] // LANES
    ug_blk = cfg["ug_col"] // LANES
    nb = cfg["b"]
    assert cfg["tc"] == LRU_CHUNK and (cfg["t"] // LRU_CHUNK) % 2 == 0
    kern = functools.partial(_lru_kernel, b=nb, t=cfg["t"], tc=cfg["tc"])
    vec = lambda n: pl.BlockSpec((n, LANES), lambda c: (0, c))
    return pl.pallas_call(
        kern,
        grid=(d_lru // LANES,),
        in_specs=[
            pl.BlockSpec((rows, LANES), lambda c: (0, ux_blk + c), pipeline_mode=pl.Buffered(1)),
            pl.BlockSpec((rows, LANES), lambda c: (0, ug_blk + c), pipeline_mode=pl.Buffered(1)),
            vec(conv_w.shape[0]), vec(1),
            pl.BlockSpec((None, LANES, 4 * LANES), lambda c: (c, 0, 0)),
            vec(2), vec(2), vec(2),
        ],
        out_specs=pl.BlockSpec((rows, LANES), lambda c: (0, c)),
        out_shape=jax.ShapeDtypeStruct((rows, d_lru), BF16),
        scratch_shapes=[
            pltpu.VMEM((rows + (2 * nb + 1) * LRU_GAP, LANES), F32),
            pltpu.VMEM((rows, LANES), F32),
        ] + [pltpu.VMEM((nb * LRU_PITCH, LANES), F32)] * 6,
        compiler_params=_params("parallel"),
        name="rglru",
    )(rest, rest, conv_w, conv_b.reshape(1, d_lru), w_bd, b_r, b_i, lam)


def _lru_block_diag(w_r, w_i):
    _, nb, bw, _ = w_r.shape
    per = LANES // bw
    eye = jnp.eye(per, dtype=w_r.dtype)

    def bd(w):
        w = w.reshape(nb // per, per, bw, bw)
        return jnp.einsum("cipq,ij->cipjq", w, eye).reshape(nb // per, LANES, LANES)

    return jnp.concatenate([bd(w_r[0]), bd(w_i[0]), bd(w_r[1]), bd(w_i[1])], axis=-1).astype(BF16)


def _fourier_kernel(u_ref, cs_ref, w2_ref, *rest, t, n_groups, scale):
    o_ref, pq_scr = rest[-2], rest[-1]
    for g in range(n_groups):
        cols = slice(g * FOURIER_GROUP, (g + 1) * FOURIER_GROUP)
        pq = _dot(u_ref[:, cols], w2_ref[...])
        pq_scr[0:t, cols] = pq[:, :FOURIER_GROUP].astype(BF16)
        pq_scr[t:2 * t, cols] = pq[:, FOURIER_GROUP:].astype(BF16)
    o_ref[...] = (_dot(cs_ref[...], pq_scr[...]) * scale).astype(BF16)


def _dft_tables(t):
    def angles(n):
        k = jnp.arange(n, dtype=jnp.int32)
        return ((k[:, None] * k[None, :]) % n).astype(F32) * (2.0 * math.pi / n)
    at = angles(t)
    ac = angles(FOURIER_GROUP)
    cs = jnp.concatenate([jnp.cos(at), -jnp.sin(at)], axis=1).astype(BF16)
    w2 = jnp.concatenate([jnp.cos(ac), jnp.sin(ac)], axis=1).astype(BF16)
    return cs, w2


def _fourier(rest, cs, w2, *, cfg, seq_len, row_blk0, rows_out, prev=None):
    d_f = cfg["d_branch"]
    wcols = min(d_f, 512)
    uf_blk = cfg["uf_col"] // wcols
    kern = functools.partial(_fourier_kernel, t=seq_len, n_groups=wcols // FOURIER_GROUP,
                             scale=(seq_len * FOURIER_GROUP) ** -0.5)
    in_specs = [
        pl.BlockSpec((seq_len, wcols), lambda bi, hf: (row_blk0 + bi, uf_blk + hf)),
        pl.BlockSpec((seq_len, 2 * seq_len), lambda bi, hf: (0, 0), pipeline_mode=pl.Buffered(1)),
        pl.BlockSpec((FOURIER_GROUP, 2 * FOURIER_GROUP), lambda bi, hf: (0, 0)),
    ]
    args = [rest, cs, w2]
    aliases = {}
    if prev is not None:
        in_specs.append(pl.BlockSpec(memory_space=pl.ANY))
        args.append(prev)
        aliases = {3: 0}
    return pl.pallas_call(
        kern,
        grid=(cfg["b"], d_f // wcols),
        in_specs=in_specs,
        out_specs=pl.BlockSpec((seq_len, wcols), lambda bi, hf: (row_blk0 + bi, hf)),
        out_shape=jax.ShapeDtypeStruct((rows_out, d_f), BF16),
        scratch_shapes=[pltpu.VMEM((2 * seq_len, wcols), BF16)],
        input_output_aliases=aliases,
        compiler_params=_params("parallel", "parallel"),
        name="fourier",
    )(*args)


def _merge_kernel(h_ref, ya_ref, yr_ref, yf_ref, wg0, wg1, wg2, bg0, bg1, bg2,
                  wb0, wb1, wb2, o_ref):
    h = h_ref[...]
    acc = None
    for y_ref, wg, bg, wb in ((ya_ref, wg0, bg0, wb0), (yr_ref, wg1, bg1, wb1),
                              (yf_ref, wg2, bg2, wb2)):
        gate = jax.nn.sigmoid(_dot(h, wg[...]) + bg[...])
        term = gate * _dot(y_ref[...], wb[...])
        acc = term if acc is None else acc + term
    o_ref[...] = acc.astype(BF16)


def _merge(h, att, rec, fou, w_gate, b_gate, w_branch, l, *, rows):
    d = h.shape[1]
    d_b = att.shape[1]
    tm, tn = ROW_TILE, 512
    nj = d // tn
    y_spec = pl.BlockSpec((tm, d_b), lambda i, j: (i, 0))
    wg_spec = lambda k: pl.BlockSpec((None, d, tn), lambda i, j: (l, 0, k * nj + j))
    bg_spec = lambda k: pl.BlockSpec((None, 1, tn), lambda i, j: (l, 0, k * nj + j))
    wb_spec = lambda k: pl.BlockSpec((None, None, d_b, tn), lambda i, j: (l, k, 0, j))
    bg = b_gate.reshape(b_gate.shape[0], 1, -1)
    return pl.pallas_call(
        _merge_kernel,
        grid=(rows // tm, nj),
        in_specs=[pl.BlockSpec((tm, d), lambda i, j: (i, 0)), y_spec, y_spec, y_spec,
                  wg_spec(0), wg_spec(1), wg_spec(2), bg_spec(0), bg_spec(1), bg_spec(2),
                  wb_spec(0), wb_spec(1), wb_spec(2)],
        out_specs=pl.BlockSpec((tm, tn), lambda i, j: (i, j)),
        out_shape=jax.ShapeDtypeStruct((rows, d), BF16),
        compiler_params=_params("parallel", "arbitrary"),
        name="gated_merge",
    )(h, att, rec, fou, w_gate, w_gate, w_gate, bg, bg, bg, w_branch, w_branch, w_branch)


def _out_proj_kernel(*refs, n_lat_tiles):
    x_refs, (m_ref, w_ref, mod_ref, g_ref, o_ref, h_ref) = refs[:-6], refs[-6:]
    delta = mod_ref[2:3, :] * _dot(m_ref[...], w_ref[...])

    def emit(r):
        y = r[...] + delta
        o_ref[...] = y
        h_ref[...] = _ada_norm(y, g_ref[...], mod_ref[3:4, :], mod_ref[4:5, :])

    if len(x_refs) == 2:
        _for_row_source(*x_refs, n_lat_tiles, emit)
    else:
        emit(x_refs[0])


def _out_proj(x_parts, m, w_out, l, mod_l, g_norm2, *, cfg, rows):
    d = m.shape[1]
    tm = ROW_TILE
    row_spec = pl.BlockSpec((tm, d), lambda i: (i, 0))
    n_lat_tiles = x_parts[0].shape[0] // tm
    x_specs = _split_row_specs(n_lat_tiles, tm, d) if len(x_parts) == 2 else [row_spec]
    return pl.pallas_call(
        functools.partial(_out_proj_kernel, n_lat_tiles=n_lat_tiles),
        grid=(rows // tm,),
        in_specs=x_specs + [
            row_spec,
            pl.BlockSpec((None, d, d), lambda i: (l, 0, 0), pipeline_mode=pl.Buffered(1)),
            pl.BlockSpec((None, 6, d), lambda i: (cfg["mod_row"](i), 0, 0)),
            pl.BlockSpec((1, d), lambda i: (0, 0)),
        ],
        out_specs=[row_spec, row_spec],
        out_shape=[jax.ShapeDtypeStruct((rows, d), F32), jax.ShapeDtypeStruct((rows, d), BF16)],
        compiler_params=_params("parallel"),
        name="out_proj",
    )(*x_parts, m, w_out, mod_l, g_norm2.reshape(1, d))


def _ffn_up_kernel(h_ref, wg_ref, wu_ref, a_ref):
    h = h_ref[...]
    gt = _dot(h, wg_ref[...])
    up = _dot(h, wu_ref[...])
    a_ref[...] = (gt * jax.nn.sigmoid(gt) * up).astype(BF16)


def _ffn_up(h2, w_ffn_in, l, *, cfg, rows):
    d = h2.shape[1]
    d_ff = w_ffn_in.shape[2] // 2
    tm, tf = FFN_UP_ROW_TILE, cfg["tf"]
    nk = d_ff // tf
    return pl.pallas_call(
        _ffn_up_kernel,
        grid=(rows // tm, nk),
        in_specs=[
            pl.BlockSpec((tm, d), lambda i, k: (i, 0)),
            pl.BlockSpec((None, d, tf), lambda i, k: (l, 0, k)),
            pl.BlockSpec((None, d, tf), lambda i, k: (l, 0, nk + k)),
        ],
        out_specs=pl.BlockSpec((tm, tf), lambda i, k: (i, k)),
        out_shape=jax.ShapeDtypeStruct((rows, d_ff), BF16),
        compiler_params=_params("parallel", "arbitrary"),
        name="ffn_up",
    )(h2, w_ffn_in, w_ffn_in)


def _ffn_down_kernel(a_ref, x_ref, wo_ref, mod_ref, nmod_ref, g_ref, *refs, nj, d, final):
    out_refs, y_scr = refs[:-1], refs[-1]
    i, j = pl.program_id(0), pl.program_id(1)
    cur = i % 2

    @pl.when((i == 0) & (j == 0))
    def _():
        y_scr[1] = jnp.zeros(y_scr.shape[1:], F32)

    prv = y_scr.at[1 - cur]
    ssq = sum(jnp.sum(prv[jj] * prv[jj], axis=-1, keepdims=True) for jj in range(nj))
    normed = prv[j] * lax.rsqrt(ssq * (1.0 / d) + NORM_EPS) * g_ref[...]
    if final:
        out_refs[0][...] = normed
    else:
        out_refs[1][...] = (normed * (1.0 + nmod_ref[1:2, :]) + nmod_ref[0:1, :]).astype(BF16)

    y = x_ref[...] + mod_ref[5:6, :] * _dot(a_ref[...], wo_ref[...])
    if not final:
        out_refs[0][...] = y
    y_scr[cur, j] = y


def _ffn_down(a, xmid, w_ffn_out, l, mod_l, next_mod, gain, *, cfg, rows, final):
    d = xmid.shape[1]
    d_ff = a.shape[1]
    tm, tn = ROW_TILE, 512
    nj = d // tn
    n_tiles = rows // tm
    cur = lambda i: jnp.minimum(i, n_tiles - 1)
    prev = lambda i: jnp.maximum(i - 1, 0)
    cur_j = lambda i, j: jnp.where(i == n_tiles, nj - 1, j)
    prev_j = lambda i, j: jnp.where(i == 0, 0, j)
    cur_spec = pl.BlockSpec((tm, tn), lambda i, j: (cur(i), cur_j(i, j)))
    prev_spec = pl.BlockSpec((tm, tn), lambda i, j: (prev(i), prev_j(i, j)))
    if final:
        out_specs, out_shape = [prev_spec], [jax.ShapeDtypeStruct((rows, d), F32)]
    else:
        out_specs = [cur_spec, prev_spec]
        out_shape = [jax.ShapeDtypeStruct((rows, d), F32), jax.ShapeDtypeStruct((rows, d), BF16)]
    return pl.pallas_call(
        functools.partial(_ffn_down_kernel, nj=nj, d=d, final=final),
        grid=(n_tiles + 1, nj),
        in_specs=[
            pl.BlockSpec((tm, d_ff), lambda i, j: (cur(i), 0)),
            cur_spec,
            pl.BlockSpec((None, d_ff, tn), lambda i, j: (l, 0, cur_j(i, j))),
            pl.BlockSpec((None, 6, tn), lambda i, j: (cfg["mod_row"](cur(i)), 0, cur_j(i, j))),
            pl.BlockSpec((None, 6, tn), lambda i, j: (cfg["mod_row"](prev(i)), 0, j)),
            pl.BlockSpec((1, tn), lambda i, j: (0, j)),
        ],
        out_specs=out_specs,
        out_shape=out_shape,
        scratch_shapes=[pltpu.VMEM((2, nj, tm, tn), F32)],
        compiler_params=_params("arbitrary", "arbitrary"),
        name="ffn_down",
    )(a, xmid, w_ffn_out, mod_l, next_mod, gain.reshape(1, d))


def _rope_tables(t, tile_rows):
    pairs = HEAD_DIM // 4
    rows = t // GRID_W
    row = jnp.repeat(jnp.arange(rows, dtype=F32), GRID_W)
    col = jnp.tile(jnp.arange(GRID_W, dtype=F32), rows)
    inv = ROPE_THETA ** (-jnp.arange(pairs, dtype=F32) / pairs)
    ang = jnp.concatenate([row[:, None] * inv, col[:, None] * inv], axis=-1)
    cos, sin = jnp.cos(ang), jnp.sin(ang)
    cosf = jnp.concatenate([cos, cos], axis=-1)
    sinf = jnp.concatenate([-sin, sin], axis=-1)
    cosf = jnp.concatenate([cosf, jnp.ones((tile_rows, HEAD_DIM), F32)], axis=0)
    sinf = jnp.concatenate([sinf, jnp.zeros((tile_rows, HEAD_DIM), F32)], axis=0)
    return cosf, sinf


def kernel(x, c, ctx, c_ctx, w_mod, b_mod, g_norm1, g_norm2, w_in, q_gain, k_gain, conv_w, conv_b, lru_w_r, lru_b_r, lru_w_i, lru_b_i, lru_lambda, w_branch, w_gate, b_gate, w_out, w_ffn_in, w_ffn_out, g_final):
    b, t, d = x.shape
    tc = ctx.shape[1]
    depth = w_in.shape[0]
    d_branch = w_branch.shape[2]
    n_q_heads = d_branch // HEAD_DIM
    n_kv_heads = n_q_heads // Q_PER_KV
    d_kv = n_kv_heads * HEAD_DIM
    d_ff = w_ffn_out.shape[1]
    n_lat, n_ctx = b * t, b * tc
    assert b == SUBLANES and t % ROW_TILE == 0 and n_ctx % FFN_UP_ROW_TILE == 0
    assert tc % LRU_CHUNK == 0 and t % LRU_CHUNK == 0
    tiles_per_batch = t // ROW_TILE
    n_lat_tiles = n_lat // ROW_TILE

    cfg = dict(
        b=b, t=t, tc=tc, d_branch=d_branch, n_q_heads=n_q_heads, n_kv_heads=n_kv_heads,
        tn_in=d_branch + 2 * d_kv,
        ux_col=0, ug_col=d_branch, uf_col=2 * d_branch,
        tf=512 if d_ff % 512 == 0 else 256,
        mod_row=lambda i: jnp.minimum(i // tiles_per_batch, b),
        rope_block=lambda i: jnp.where(i < n_lat_tiles, i % tiles_per_batch, tiles_per_batch),
    )

    x_parts = (x.reshape(n_lat, d), ctx.reshape(n_ctx, d))
    cc = jnp.concatenate([c, c_ctx[None, :], jnp.zeros((2 * SUBLANES - b - 1, d), F32)], axis=0)
    mod = _modulation(cc, w_mod, b_mod).reshape(depth, 2 * SUBLANES, 6, d)

    w_in_b = w_in.astype(BF16)
    w_gate_b = w_gate.astype(BF16)
    w_branch_b = w_branch.astype(BF16)
    w_out_b = w_out.astype(BF16)
    w_ffn_in_b = w_ffn_in.astype(BF16)
    w_ffn_out_b = w_ffn_out.astype(BF16)

    cosf, sinf = _rope_tables(t, ROW_TILE)
    cs_lat, w2 = _dft_tables(t)
    cs_ctx, _ = _dft_tables(tc)

    h1 = _prenorm(*x_parts, mod[0], g_norm1[0], cfg=cfg)
    out = None
    for l in range(depth):
        last = l == depth - 1
        rows = n_lat if last else n_lat + n_ctx
        qkv, rest = _in_proj(h1, w_in_b, l, cosf, sinf, q_gain[l], k_gain[l], cfg=cfg)
        att = _attention(qkv, cfg=cfg, rows_out=rows)
        if not last:
            att = _attention(qkv, cfg=cfg, rows_out=rows, prev=att)
        rec = _lru(rest, conv_w[l], conv_b[l], _lru_block_diag(lru_w_r[l], lru_w_i[l]),
                   lru_b_r[l], lru_b_i[l], lru_lambda[l], cfg=cfg)
        fou = _fourier(rest, cs_lat, w2, cfg=cfg, seq_len=t, row_blk0=0, rows_out=rows)
        if not last:
            fou = _fourier(rest, cs_ctx, w2, cfg=cfg, seq_len=tc, row_blk0=n_lat // tc,
                           rows_out=rows, prev=fou)
        m = _merge(h1, att, rec, fou, w_gate_b, b_gate, w_branch_b, l, rows=rows)
        xmid, h2 = _out_proj(x_parts, m, w_out_b, l, mod[l], g_norm2[l], cfg=cfg, rows=rows)
        a = _ffn_up(h2, w_ffn_in_b, l, cfg=cfg, rows=rows)
        if last:
            (out,) = _ffn_down(a, xmid, w_ffn_out_b, l, mod[l], mod[l], g_final,
                               cfg=cfg, rows=rows, final=True)
        else:
            xall, h1 = _ffn_down(a, xmid, w_ffn_out_b, l, mod[l], mod[l + 1], g_norm1[l + 1],
                                 cfg=cfg, rows=rows, final=False)
            x_parts = (xall,)
    return out.reshape(b, t, d)
```

```python
import functools
import math

import jax
import jax.numpy as jnp
from jax import lax
from jax.experimental import pallas as pl
from jax.experimental.pallas import tpu as pltpu

F32 = jnp.float32
BF16 = jnp.bfloat16

HEAD_DIM = 128
Q_PER_KV = 4
GRID_W = 64
ROPE_THETA = 10000.0
NORM_EPS = 1e-6
LRU_C = 8.0
FOURIER_GROUP = 128

LANES = 128
SUBLANES = 8
VMEM_LIMIT_BYTES = 56 * 1024 * 1024

ROW_TILE = 512
FFN_UP_ROW_TILE = 1024
ATTN_Q_TILE = 1024
ATTN_SUB_ROWS = 512
LRU_CHUNK = 256
LRU_GAP = SUBLANES


def _params(*semantics):
    return pltpu.CompilerParams(dimension_semantics=semantics,
                                vmem_limit_bytes=VMEM_LIMIT_BYTES)


def _dot(a, b):
    return jnp.dot(a, b, preferred_element_type=F32)


def _dot_nt(a, b):
    return lax.dot_general(a, b, (((1,), (1,)), ((), ())), preferred_element_type=F32)


def _rms(x):
    return x * lax.rsqrt(jnp.mean(x * x, axis=-1, keepdims=True) + NORM_EPS)


def _ada_norm(x, gain, shift, scale):
    return (_rms(x) * gain * (1.0 + scale) + shift).astype(BF16)


def _mod_kernel(c_ref, w_ref, b_ref, o_ref):
    c = c_ref[...]
    s = (c * jax.nn.sigmoid(c)).astype(BF16)
    o_ref[...] = _dot(s, w_ref[...].astype(BF16)) + b_ref[...]


def _modulation(cc, w_mod, b_mod):
    depth, d, n = w_mod.shape
    rows = cc.shape[0]
    tn = 1024
    return pl.pallas_call(
        _mod_kernel,
        grid=(depth, n // tn),
        in_specs=[
            pl.BlockSpec((rows, d), lambda l, j: (0, 0)),
            pl.BlockSpec((None, d, tn), lambda l, j: (l, 0, j)),
            pl.BlockSpec((None, 1, tn), lambda l, j: (l, 0, j)),
        ],
        out_specs=pl.BlockSpec((None, rows, tn), lambda l, j: (l, 0, j)),
        out_shape=jax.ShapeDtypeStruct((depth, rows, n), F32),
        compiler_params=_params("parallel", "parallel"),
        name="modulation",
    )(cc, w_mod, b_mod.reshape(depth, 1, n))


def _split_row_specs(n_lat_tiles, tm, d):
    return [pl.BlockSpec((tm, d), lambda i: (jnp.minimum(i, n_lat_tiles - 1), 0)),
            pl.BlockSpec((tm, d), lambda i: (jnp.maximum(i - n_lat_tiles, 0), 0))]


def _for_row_source(x_ref, c_ref, n_lat_tiles, fn):
    i = pl.program_id(0)
    pl.when(i < n_lat_tiles)(lambda: fn(x_ref))
    pl.when(i >= n_lat_tiles)(lambda: fn(c_ref))


def _prenorm_kernel(x_ref, c_ref, mod_ref, g_ref, h_ref, *, n_lat_tiles):
    def emit(r):
        h_ref[...] = _ada_norm(r[...], g_ref[...], mod_ref[0:1, :], mod_ref[1:2, :])
    _for_row_source(x_ref, c_ref, n_lat_tiles, emit)


def _prenorm(x2d, ctx2d, mod_l, g_norm, *, cfg):
    d = x2d.shape[1]
    rows = x2d.shape[0] + ctx2d.shape[0]
    tm = ROW_TILE
    n_lat_tiles = x2d.shape[0] // tm
    return pl.pallas_call(
        functools.partial(_prenorm_kernel, n_lat_tiles=n_lat_tiles),
        grid=(rows // tm,),
        in_specs=_split_row_specs(n_lat_tiles, tm, d) + [
            pl.BlockSpec((None, 6, d), lambda i: (cfg["mod_row"](i), 0, 0)),
            pl.BlockSpec((1, d), lambda i: (0, 0)),
        ],
        out_specs=pl.BlockSpec((tm, d), lambda i: (i, 0)),
        out_shape=jax.ShapeDtypeStruct((rows, d), BF16),
        compiler_params=_params("parallel"),
        name="prenorm",
    )(x2d, ctx2d, mod_l, g_norm.reshape(1, d))


def _in_proj_kernel(h_ref, w_ref, cos_ref, sin_ref, qg_ref, kg_ref, qkv_ref, rest_ref, qkv_scr,
                    *, n_q_heads, n_kv_heads):
    j = pl.program_id(1)

    @pl.when(j == 0)
    def _():
        qkv_scr[...] = _dot(h_ref[...], w_ref[...])

    @pl.when(j == 1)
    def _():
        rest_ref[...] = _dot(h_ref[...], w_ref[...]).astype(BF16)
        cos = cos_ref[...]
        sin = sin_ref[...]
        q_gain = qg_ref[...] * (HEAD_DIM ** -0.5 * math.log2(math.e))
        k_gain = kg_ref[...]
        n_rot = n_q_heads + n_kv_heads
        for hh in range(n_rot):
            cols = slice(hh * HEAD_DIM, (hh + 1) * HEAD_DIM)
            y = _rms(qkv_scr[:, cols]) * (q_gain if hh < n_q_heads else k_gain)
            y = y * cos + pltpu.roll(y, HEAD_DIM // 2, axis=1) * sin
            qkv_ref[:, cols] = y.astype(BF16)
        qkv_ref[:, n_rot * HEAD_DIM:] = qkv_scr[:, n_rot * HEAD_DIM:].astype(BF16)

    @pl.when(j > 1)
    def _():
        rest_ref[...] = _dot(h_ref[...], w_ref[...]).astype(BF16)


def _in_proj(h1, w_in, l, cosf, sinf, q_gain, k_gain, *, cfg):
    rows, d = h1.shape
    d_in = w_in.shape[2]
    tm, tn = ROW_TILE, cfg["tn_in"]
    assert d_in == 3 * tn
    kern = functools.partial(_in_proj_kernel, n_q_heads=cfg["n_q_heads"],
                             n_kv_heads=cfg["n_kv_heads"])
    return pl.pallas_call(
        kern,
        grid=(rows // tm, d_in // tn),
        in_specs=[
            pl.BlockSpec((tm, d), lambda i, j: (i, 0)),
            pl.BlockSpec((None, d, tn), lambda i, j: (l, 0, j)),
            pl.BlockSpec((tm, HEAD_DIM), lambda i, j: (cfg["rope_block"](i), 0)),
            pl.BlockSpec((tm, HEAD_DIM), lambda i, j: (cfg["rope_block"](i), 0)),
            pl.BlockSpec((1, HEAD_DIM), lambda i, j: (0, 0)),
            pl.BlockSpec((1, HEAD_DIM), lambda i, j: (0, 0)),
        ],
        out_specs=[
            pl.BlockSpec((tm, tn), lambda i, j: (i, 0)),
            pl.BlockSpec((tm, tn), lambda i, j: (i, jnp.maximum(j - 1, 0))),
        ],
        out_shape=[
            jax.ShapeDtypeStruct((rows, tn), BF16),
            jax.ShapeDtypeStruct((rows, d_in - tn), BF16),
        ],
        scratch_shapes=[pltpu.VMEM((tm, tn), F32)],
        compiler_params=_params("parallel", "arbitrary"),
        name="in_proj",
    )(h1, w_in, cosf, sinf, q_gain.reshape(1, HEAD_DIM), k_gain.reshape(1, HEAD_DIM))


def _attn_kernel(q_ref, *refs, latent):
    if latent:
        kl_ref, kc_ref, vl_ref, vc_ref, o_ref, vl_aug, vc_aug = refs
    else:
        kc_ref, vc_ref, o_ref, vc_aug = refs

    @pl.when(pl.program_id(2) == 0)
    def _():
        for v_ref, aug in ((vl_ref, vl_aug), (vc_ref, vc_aug)) if latent else ((vc_ref, vc_aug),):
            lane = lax.broadcasted_iota(jnp.int32, v_ref.shape, 1)
            aug[:, :HEAD_DIM] = v_ref[...]
            aug[:, HEAD_DIM:] = jnp.where(lane == 0, 1.0, 0.0).astype(BF16)

    kc = kc_ref[...]
    sub = min(q_ref.shape[0], ATTN_SUB_ROWS)
    for r0 in range(0, q_ref.shape[0], sub):
        for g in range(Q_PER_KV):
            cols = slice(g * HEAD_DIM, (g + 1) * HEAD_DIM)
            q = q_ref[r0:r0 + sub, cols]
            sc = _dot_nt(q, kc)
            m = jnp.max(sc, axis=-1, keepdims=True)
            if latent:
                sl = _dot_nt(q, kl_ref[...])
                m = jnp.maximum(m, jnp.max(sl, axis=-1, keepdims=True))
            o = _dot(jnp.exp2((sc - m).astype(BF16)), vc_aug[...])
            if latent:
                o = o + _dot(jnp.exp2((sl - m).astype(BF16)), vl_aug[...])
            o_ref[r0:r0 + sub, cols] = (
                o[:, :HEAD_DIM] / o[:, HEAD_DIM:HEAD_DIM + 1]).astype(BF16)


def _attention(qkv, *, cfg, latent):
    b, t, tc = cfg["b"], cfg["t"], cfg["tc"]
    tq = ATTN_Q_TILE if latent else tc
    n_q_tiles = t // tq if latent else 1
    q_blk0 = 0 if latent else b * t // tc
    n_kv = cfg["n_kv_heads"]
    gw = Q_PER_KV * HEAD_DIM
    k_col = cfg["n_q_heads"]
    v_col = k_col + n_kv
    ctx_blk0 = b * t // tc

    q_spec = pl.BlockSpec((tq, gw), lambda bi, h, qi: (q_blk0 + bi * n_q_tiles + qi, h))
    lat_spec = lambda col: pl.BlockSpec((t, HEAD_DIM), lambda bi, h, qi: (bi, col + h))
    ctx_spec = lambda col: pl.BlockSpec((tc, HEAD_DIM), lambda bi, h, qi: (ctx_blk0 + bi, col + h))
    aug = lambda n: pltpu.VMEM((n, 2 * HEAD_DIM), BF16)
    if latent:
        in_specs = [q_spec, lat_spec(k_col), ctx_spec(k_col), lat_spec(v_col), ctx_spec(v_col)]
        scratch = [aug(t), aug(tc)]
    else:
        in_specs = [q_spec, ctx_spec(k_col), ctx_spec(v_col)]
        scratch = [aug(tc)]
    return pl.pallas_call(
        functools.partial(_attn_kernel, latent=latent),
        grid=(b, n_kv, n_q_tiles),
        in_specs=in_specs,
        out_specs=pl.BlockSpec((tq, gw), lambda bi, h, qi: (bi * n_q_tiles + qi, h)),
        out_shape=jax.ShapeDtypeStruct((b * (t if latent else tc), cfg["n_q_heads"] * HEAD_DIM),
                                       BF16),
        scratch_shapes=scratch,
        compiler_params=_params("parallel", "parallel", "arbitrary"),
        name="attention",
    )(*[qkv] * len(in_specs))


def _lru_kernel(ux_ref, ug_ref, cw_ref, cb_ref, w_ref, br_ref, bi_ref, lam_ref, o_ref,
                u_scr, acc_scr, af_scr, bf_scr, yf_scr, ab_scr, bb_scr, yb_scr, *, b, t, tc):
    ch, gap = LRU_CHUNK, LRU_GAP
    n_lat = b * t
    conv_width = cw_ref.shape[0]
    conv_left = conv_width // 2

    lat = (gap, 0, t + gap, t)
    ctx = (gap + b * (t + gap), n_lat, tc + gap, tc)

    zeros_gap = jnp.zeros((gap, LANES), F32)
    for kind, n in ((lat, t), (ctx, tc)):
        for bi in range(b):
            u0 = kind[0] + bi * kind[2]
            r0 = kind[1] + bi * kind[3]
            u_scr[u0 - gap:u0, :] = zeros_gap
            u_scr[u0:u0 + n, :] = ux_ref[r0:r0 + n, :].astype(F32)
    u_end = ctx[0] + b * ctx[2] - gap
    u_scr[u_end:u_end + gap, :] = zeros_gap

    def slab_rows(kind, bi, off):
        start = kind[1] + bi * kind[3] + off
        if not isinstance(start, int):
            start = pl.multiple_of(start, ch)
        return pl.ds(start, ch)

    def coefficients(d, kind, off, a_scr, b_scr):
        lam = lam_ref[d:d + 1, :]
        softplus_neg_lam = jnp.maximum(-lam, 0.0) + jnp.log1p(jnp.exp(-jnp.abs(lam)))
        half_rate = -0.5 * LRU_C * softplus_neg_lam
        w_half = w_ref[:, d * 2 * LANES:(d + 1) * 2 * LANES] * 0.5
        br_half = 0.5 * br_ref[d:d + 1, :]
        bi_half = 0.5 * bi_ref[d:d + 1, :]
        for bi in range(b):
            s0 = kind[0] + bi * kind[2] + off - conv_left
            u = cb_ref[...]
            for j in range(conv_width):
                u = u + cw_ref[j:j + 1, :] * u_scr[pl.ds(s0 + j, ch), :]
            gates = _dot(u.astype(BF16), w_half)
            log_a = jnp.tanh(gates[:, :LANES] + br_half) * half_rate + half_rate
            i_gate = 0.5 * jnp.tanh(gates[:, LANES:] + bi_half) + 0.5
            a = jnp.exp(log_a)
            one_minus_a2 = -jnp.tanh(log_a) * (a * a + 1.0)
            root = jnp.where(one_minus_a2 > 0.0, one_minus_a2 * lax.rsqrt(one_minus_a2), 0.0)
            a_scr[pl.ds(bi, ch, stride=b), :] = a
            b_scr[pl.ds(bi, ch, stride=b), :] = root * i_gate * u

    def emit(rows, y):
        gate = jax.nn.gelu(ug_ref[rows, :].astype(F32), approximate=True)
        o_ref[rows, :] = (y * gate).astype(BF16)

    def pair(kind, f_off, b_off, carry, mode):
        coefficients(0, kind, f_off, af_scr, bf_scr)
        coefficients(1, kind, b_off, ab_scr, bb_scr)

        def two_steps(h, a_scr, b_scr, y_scr, s0, s1):
            r0 = pl.ds(pl.multiple_of(s0 * b, b), b)
            r1 = pl.ds(pl.multiple_of(s1 * b, b), b)
            a0, b0, a1, b1 = a_scr[r0, :], b_scr[r0, :], a_scr[r1, :], b_scr[r1, :]
            y_scr[r0, :] = a0 * h + b0
            h = (a1 * a0) * h + (a1 * b0 + b1)
            y_scr[r1, :] = h
            return h

        def step(s, hs):
            hf, hb = hs
            hf = two_steps(hf, af_scr, bf_scr, yf_scr, 2 * s, 2 * s + 1)
            hb = two_steps(hb, ab_scr, bb_scr, yb_scr, ch - 1 - 2 * s, ch - 2 - 2 * s)
            return hf, hb

        carry = lax.fori_loop(0, ch // 2, step, carry, unroll=16)

        for bi in range(b):
            blk = pl.ds(bi, ch, stride=b)
            rf = slab_rows(kind, bi, f_off)
            rb = slab_rows(kind, bi, b_off)
            if mode == "same":
                emit(rf, yf_scr[blk, :] + yb_scr[blk, :])
            elif mode == "first":
                acc_scr[rf, :] = yf_scr[blk, :]
                acc_scr[rb, :] = yb_scr[blk, :]
            else:
                emit(rf, acc_scr[rf, :] + yf_scr[blk, :])
                emit(rb, acc_scr[rb, :] + yb_scr[blk, :])
        return carry

    n_lat_chunks = t // ch
    half = n_lat_chunks // 2
    zero = jnp.zeros((b, LANES), F32)
    carry = pair(ctx, 0, 0, (zero, zero), "same")
    carry = lax.fori_loop(
        0, half,
        lambda q, hs: pair(lat, q * ch, (n_lat_chunks - 1 - q) * ch, hs, "first"), carry)
    lax.fori_loop(
        0, half,
        lambda q, hs: pair(lat, (half + q) * ch, (half - 1 - q) * ch, hs, "second"), carry)


def _lru(rest, conv_w, conv_b, w_bd, b_r, b_i, lam, *, cfg):
    rows = rest.shape[0]
    d_lru = conv_w.shape[1]
    ux_blk = cfg["ux_col"] // LANES
    ug_blk = cfg["ug_col"] // LANES
    nb = cfg["b"]
    assert cfg["tc"] == LRU_CHUNK and (cfg["t"] // LRU_CHUNK) % 2 == 0
    kern = functools.partial(_lru_kernel, b=nb, t=cfg["t"], tc=cfg["tc"])
    vec = lambda n: pl.BlockSpec((n, LANES), lambda c: (0, c))
    return pl.pallas_call(
        kern,
        grid=(d_lru // LANES,),
        in_specs=[
            pl.BlockSpec((rows, LANES), lambda c: (0, ux_blk + c), pipeline_mode=pl.Buffered(1)),
            pl.BlockSpec((rows, LANES), lambda c: (0, ug_blk + c), pipeline_mode=pl.Buffered(1)),
            vec(conv_w.shape[0]), vec(1),
            pl.BlockSpec((None, LANES, 4 * LANES), lambda c: (c, 0, 0)),
            vec(2), vec(2), vec(2),
        ],
        out_specs=pl.BlockSpec((rows, LANES), lambda c: (0, c)),
        out_shape=jax.ShapeDtypeStruct((rows, d_lru), BF16),
        scratch_shapes=[
            pltpu.VMEM((rows + (2 * nb + 1) * LRU_GAP, LANES), F32),
            pltpu.VMEM((rows, LANES), F32),
        ] + [pltpu.VMEM((nb * LRU_CHUNK, LANES), F32)] * 6,
        compiler_params=_params("parallel"),
        name="rglru",
    )(rest, rest, conv_w, conv_b.reshape(1, d_lru), w_bd, b_r, b_i, lam)


def _lru_block_diag(w_r, w_i):
    _, nb, bw, _ = w_r.shape
    per = LANES // bw
    eye = jnp.eye(per, dtype=w_r.dtype)

    def bd(w):
        w = w.reshape(nb // per, per, bw, bw)
        return jnp.einsum("cipq,ij->cipjq", w, eye).reshape(nb // per, LANES, LANES)

    return jnp.concatenate([bd(w_r[0]), bd(w_i[0]), bd(w_r[1]), bd(w_i[1])], axis=-1).astype(BF16)


def _fourier_kernel(u_ref, cs_ref, w2_ref, o_ref, pq_scr, *, t, n_groups, scale):
    for g in range(n_groups):
        cols = slice(g * FOURIER_GROUP, (g + 1) * FOURIER_GROUP)
        pq = _dot(u_ref[:, cols], w2_ref[...])
        pq_scr[0:t, cols] = pq[:, :FOURIER_GROUP].astype(BF16)
        pq_scr[t:2 * t, cols] = pq[:, FOURIER_GROUP:].astype(BF16)
    o_ref[...] = (_dot(cs_ref[...], pq_scr[...]) * scale).astype(BF16)


def _dft_tables(t):
    def angles(n):
        k = jnp.arange(n, dtype=jnp.int32)
        return ((k[:, None] * k[None, :]) % n).astype(F32) * (2.0 * math.pi / n)
    at = angles(t)
    ac = angles(FOURIER_GROUP)
    cs = jnp.concatenate([jnp.cos(at), -jnp.sin(at)], axis=1).astype(BF16)
    w2 = jnp.concatenate([jnp.cos(ac), jnp.sin(ac)], axis=1).astype(BF16)
    return cs, w2


def _fourier(rest, cs, w2, *, cfg, seq_len, row_blk0):
    d_f = cfg["d_branch"]
    wcols = min(d_f, 512)
    uf_blk = cfg["uf_col"] // wcols
    kern = functools.partial(_fourier_kernel, t=seq_len, n_groups=wcols // FOURIER_GROUP,
                             scale=(seq_len * FOURIER_GROUP) ** -0.5)
    return pl.pallas_call(
        kern,
        grid=(cfg["b"], d_f // wcols),
        in_specs=[
            pl.BlockSpec((seq_len, wcols), lambda bi, hf: (row_blk0 + bi, uf_blk + hf)),
            pl.BlockSpec((seq_len, 2 * seq_len), lambda bi, hf: (0, 0),
                         pipeline_mode=pl.Buffered(1)),
            pl.BlockSpec((FOURIER_GROUP, 2 * FOURIER_GROUP), lambda bi, hf: (0, 0)),
        ],
        out_specs=pl.BlockSpec((seq_len, wcols), lambda bi, hf: (bi, hf)),
        out_shape=jax.ShapeDtypeStruct((cfg["b"] * seq_len, d_f), BF16),
        scratch_shapes=[pltpu.VMEM((2 * seq_len, wcols), BF16)],
        compiler_params=_params("parallel", "parallel"),
        name="fourier",
    )(rest, cs, w2)


def _merge_kernel(*refs, n_parts, n_lat_tiles):
    h_ref = refs[0]
    att_refs = refs[1:1 + n_parts]
    rec_ref = refs[1 + n_parts]
    fou_refs = refs[2 + n_parts:2 + 2 * n_parts]
    wgs, bgs, wbs = (refs[2 + 2 * n_parts + 3 * k:5 + 2 * n_parts + 3 * k] for k in range(3))
    o_ref = refs[-1]
    is_latent = pl.program_id(0) < n_lat_tiles

    def rows_of(parts):
        if len(parts) == 1:
            return parts[0][...]
        return jnp.where(is_latent, parts[0][...], parts[1][...])

    h = h_ref[...]
    acc = None
    for y, wg, bg, wb in zip((rows_of(att_refs), rec_ref[...], rows_of(fou_refs)), wgs, bgs, wbs):
        gate = jax.nn.sigmoid(_dot(h, wg[...]) + bg[...])
        term = gate * _dot(y, wb[...])
        acc = term if acc is None else acc + term
    o_ref[...] = acc.astype(BF16)


def _merge(h, att_parts, rec, fou_parts, w_gate, b_gate, w_branch, l, *, rows):
    d = h.shape[1]
    d_b = rec.shape[1]
    tm, tn = ROW_TILE, 512
    nj = d // tn
    n_parts = len(att_parts)
    n_lat_tiles = att_parts[0].shape[0] // tm
    y_spec = pl.BlockSpec((tm, d_b), lambda i, j: (i, 0))
    part_specs = [pl.BlockSpec((tm, d_b), lambda i, j: (jnp.minimum(i, n_lat_tiles - 1), 0)),
                  pl.BlockSpec((tm, d_b), lambda i, j: (jnp.maximum(i - n_lat_tiles, 0), 0))
                  ][:n_parts]
    wg_spec = lambda k: pl.BlockSpec((None, d, tn), lambda i, j: (l, 0, k * nj + j))
    bg_spec = lambda k: pl.BlockSpec((None, 1, tn), lambda i, j: (l, 0, k * nj + j))
    wb_spec = lambda k: pl.BlockSpec((None, None, d_b, tn), lambda i, j: (l, k, 0, j))
    bg = b_gate.reshape(b_gate.shape[0], 1, -1)
    return pl.pallas_call(
        functools.partial(_merge_kernel, n_parts=n_parts, n_lat_tiles=n_lat_tiles),
        grid=(rows // tm, nj),
        in_specs=[pl.BlockSpec((tm, d), lambda i, j: (i, 0))] + part_specs + [y_spec] + part_specs
                 + [wg_spec(0), wg_spec(1), wg_spec(2), bg_spec(0), bg_spec(1), bg_spec(2),
                    wb_spec(0), wb_spec(1), wb_spec(2)],
        out_specs=pl.BlockSpec((tm, tn), lambda i, j: (i, j)),
        out_shape=jax.ShapeDtypeStruct((rows, d), BF16),
        compiler_params=_params("parallel", "arbitrary"),
        name="gated_merge",
    )(h, *att_parts, rec, *fou_parts, w_gate, w_gate, w_gate, bg, bg, bg,
      w_branch, w_branch, w_branch)


def _out_proj_kernel(*refs, n_lat_tiles):
    x_refs, (m_ref, w_ref, mod_ref, g_ref, o_ref, h_ref) = refs[:-6], refs[-6:]
    delta = mod_ref[2:3, :] * _dot(m_ref[...], w_ref[...])

    def emit(r):
        y = r[...] + delta
        o_ref[...] = y
        h_ref[...] = _ada_norm(y, g_ref[...], mod_ref[3:4, :], mod_ref[4:5, :])

    if len(x_refs) == 2:
        _for_row_source(*x_refs, n_lat_tiles, emit)
    else:
        emit(x_refs[0])


def _out_proj(x_parts, m, w_out, l, mod_l, g_norm2, *, cfg, rows):
    d = m.shape[1]
    tm = ROW_TILE
    row_spec = pl.BlockSpec((tm, d), lambda i: (i, 0))
    n_lat_tiles = x_parts[0].shape[0] // tm
    x_specs = _split_row_specs(n_lat_tiles, tm, d) if len(x_parts) == 2 else [row_spec]
    return pl.pallas_call(
        functools.partial(_out_proj_kernel, n_lat_tiles=n_lat_tiles),
        grid=(rows // tm,),
        in_specs=x_specs + [
            row_spec,
            pl.BlockSpec((None, d, d), lambda i: (l, 0, 0), pipeline_mode=pl.Buffered(1)),
            pl.BlockSpec((None, 6, d), lambda i: (cfg["mod_row"](i), 0, 0)),
            pl.BlockSpec((1, d), lambda i: (0, 0)),
        ],
        out_specs=[row_spec, row_spec],
        out_shape=[jax.ShapeDtypeStruct((rows, d), F32), jax.ShapeDtypeStruct((rows, d), BF16)],
        compiler_params=_params("parallel"),
        name="out_proj",
    )(*x_parts, m, w_out, mod_l, g_norm2.reshape(1, d))


def _ffn_up_kernel(h_ref, wg_ref, wu_ref, a_ref):
    h = h_ref[...]
    gt = _dot(h, wg_ref[...])
    up = _dot(h, wu_ref[...])
    a_ref[...] = (gt * jax.nn.sigmoid(gt) * up).astype(BF16)


def _ffn_up(h2, w_ffn_in, l, *, cfg, rows):
    d = h2.shape[1]
    d_ff = w_ffn_in.shape[2] // 2
    tm, tf = FFN_UP_ROW_TILE, cfg["tf"]
    nk = d_ff // tf
    return pl.pallas_call(
        _ffn_up_kernel,
        grid=(rows // tm, nk),
        in_specs=[
            pl.BlockSpec((tm, d), lambda i, k: (i, 0)),
            pl.BlockSpec((None, d, tf), lambda i, k: (l, 0, k)),
            pl.BlockSpec((None, d, tf), lambda i, k: (l, 0, nk + k)),
        ],
        out_specs=pl.BlockSpec((tm, tf), lambda i, k: (i, k)),
        out_shape=jax.ShapeDtypeStruct((rows, d_ff), BF16),
        compiler_params=_params("parallel", "arbitrary"),
        name="ffn_up",
    )(h2, w_ffn_in, w_ffn_in)


def _ffn_down_kernel(a_ref, x_ref, wo_ref, mod_ref, nmod_ref, g_ref, *refs, nj, d, final):
    out_refs, y_scr = refs[:-1], refs[-1]
    i, j = pl.program_id(0), pl.program_id(1)
    cur = i % 2

    @pl.when((i == 0) & (j == 0))
    def _():
        y_scr[1] = jnp.zeros(y_scr.shape[1:], F32)

    prv = y_scr.at[1 - cur]
    ssq = sum(jnp.sum(prv[jj] * prv[jj], axis=-1, keepdims=True) for jj in range(nj))
    normed = prv[j] * lax.rsqrt(ssq * (1.0 / d) + NORM_EPS) * g_ref[...]
    if final:
        out_refs[0][...] = normed
    else:
        out_refs[1][...] = (normed * (1.0 + nmod_ref[1:2, :]) + nmod_ref[0:1, :]).astype(BF16)

    y = x_ref[...] + mod_ref[5:6, :] * _dot(a_ref[...], wo_ref[...])
    if not final:
        out_refs[0][...] = y
    y_scr[cur, j] = y


def _ffn_down(a, xmid, w_ffn_out, l, mod_l, next_mod, gain, *, cfg, rows, final):
    d = xmid.shape[1]
    d_ff = a.shape[1]
    tm, tn = ROW_TILE, 512
    nj = d // tn
    n_tiles = rows // tm
    cur = lambda i: jnp.minimum(i, n_tiles - 1)
    prev = lambda i: jnp.maximum(i - 1, 0)
    cur_j = lambda i, j: jnp.where(i == n_tiles, nj - 1, j)
    prev_j = lambda i, j: jnp.where(i == 0, 0, j)
    cur_spec = pl.BlockSpec((tm, tn), lambda i, j: (cur(i), cur_j(i, j)))
    prev_spec = pl.BlockSpec((tm, tn), lambda i, j: (prev(i), prev_j(i, j)))
    if final:
        out_specs, out_shape = [prev_spec], [jax.ShapeDtypeStruct((rows, d), F32)]
    else:
        out_specs = [cur_spec, prev_spec]
        out_shape = [jax.ShapeDtypeStruct((rows, d), F32), jax.ShapeDtypeStruct((rows, d), BF16)]
    return pl.pallas_call(
        functools.partial(_ffn_down_kernel, nj=nj, d=d, final=final),
        grid=(n_tiles + 1, nj),
        in_specs=[
            pl.BlockSpec((tm, d_ff), lambda i, j: (cur(i), 0)),
            cur_spec,
            pl.BlockSpec((None, d_ff, tn), lambda i, j: (l, 0, cur_j(i, j))),
            pl.BlockSpec((None, 6, tn), lambda i, j: (cfg["mod_row"](cur(i)), 0, cur_j(i, j))),
            pl.BlockSpec((None, 6, tn), lambda i, j: (cfg["mod_row"](prev(i)), 0, j)),
            pl.BlockSpec((1, tn), lambda i, j: (0, j)),
        ],
        out_specs=out_specs,
        out_shape=out_shape,
        scratch_shapes=[pltpu.VMEM((2, nj, tm, tn), F32)],
        compiler_params=_params("arbitrary", "arbitrary"),
        name="ffn_down",
    )(a, xmid, w_ffn_out, mod_l, next_mod, gain.reshape(1, d))


def _rope_tables(t, tile_rows):
    pairs = HEAD_DIM // 4
    rows = t // GRID_W
    row = jnp.repeat(jnp.arange(rows, dtype=F32), GRID_W)
    col = jnp.tile(jnp.arange(GRID_W, dtype=F32), rows)
    inv = ROPE_THETA ** (-jnp.arange(pairs, dtype=F32) / pairs)
    ang = jnp.concatenate([row[:, None] * inv, col[:, None] * inv], axis=-1)
    cos, sin = jnp.cos(ang), jnp.sin(ang)
    cosf = jnp.concatenate([cos, cos], axis=-1)
    sinf = jnp.concatenate([-sin, sin], axis=-1)
    cosf = jnp.concatenate([cosf, jnp.ones((tile_rows, HEAD_DIM), F32)], axis=0)
    sinf = jnp.concatenate([sinf, jnp.zeros((tile_rows, HEAD_DIM), F32)], axis=0)
    return cosf, sinf


def kernel(x, c, ctx, c_ctx, w_mod, b_mod, g_norm1, g_norm2, w_in, q_gain, k_gain, conv_w, conv_b, lru_w_r, lru_b_r, lru_w_i, lru_b_i, lru_lambda, w_branch, w_gate, b_gate, w_out, w_ffn_in, w_ffn_out, g_final):
    b, t, d = x.shape
    tc = ctx.shape[1]
    depth = w_in.shape[0]
    d_branch = w_branch.shape[2]
    n_q_heads = d_branch // HEAD_DIM
    n_kv_heads = n_q_heads // Q_PER_KV
    d_kv = n_kv_heads * HEAD_DIM
    d_ff = w_ffn_out.shape[1]
    n_lat, n_ctx = b * t, b * tc
    assert b == SUBLANES and t % ROW_TILE == 0 and n_ctx % FFN_UP_ROW_TILE == 0
    assert tc % LRU_CHUNK == 0 and t % LRU_CHUNK == 0
    tiles_per_batch = t // ROW_TILE
    n_lat_tiles = n_lat // ROW_TILE

    cfg = dict(
        b=b, t=t, tc=tc, d_branch=d_branch, n_q_heads=n_q_heads, n_kv_heads=n_kv_heads,
        tn_in=d_branch + 2 * d_kv,
        ux_col=0, ug_col=d_branch, uf_col=2 * d_branch,
        tf=512 if d_ff % 512 == 0 else 256,
        mod_row=lambda i: jnp.minimum(i // tiles_per_batch, b),
        rope_block=lambda i: jnp.where(i < n_lat_tiles, i % tiles_per_batch, tiles_per_batch),
    )

    x_parts = (x.reshape(n_lat, d), ctx.reshape(n_ctx, d))
    cc = jnp.concatenate([c, c_ctx[None, :], jnp.zeros((2 * SUBLANES - b - 1, d), F32)], axis=0)
    mod = _modulation(cc, w_mod, b_mod).reshape(depth, 2 * SUBLANES, 6, d)

    w_in_b = w_in.astype(BF16)
    w_gate_b = w_gate.astype(BF16)
    w_branch_b = w_branch.astype(BF16)
    w_out_b = w_out.astype(BF16)
    w_ffn_in_b = w_ffn_in.astype(BF16)
    w_ffn_out_b = w_ffn_out.astype(BF16)

    cosf, sinf = _rope_tables(t, ROW_TILE)
    cs_lat, w2 = _dft_tables(t)
    cs_ctx, _ = _dft_tables(tc)

    h1 = _prenorm(*x_parts, mod[0], g_norm1[0], cfg=cfg)
    out = None
    for l in range(depth):
        last = l == depth - 1
        rows = n_lat if last else n_lat + n_ctx
        qkv, rest = _in_proj(h1, w_in_b, l, cosf, sinf, q_gain[l], k_gain[l], cfg=cfg)
        att = (_attention(qkv, cfg=cfg, latent=True),)
        rec = _lru(rest, conv_w[l], conv_b[l], _lru_block_diag(lru_w_r[l], lru_w_i[l]),
                   lru_b_r[l], lru_b_i[l], lru_lambda[l], cfg=cfg)
        fou = (_fourier(rest, cs_lat, w2, cfg=cfg, seq_len=t, row_blk0=0),)
        if not last:
            att += (_attention(qkv, cfg=cfg, latent=False),)
            fou += (_fourier(rest, cs_ctx, w2, cfg=cfg, seq_len=tc, row_blk0=n_lat // tc),)
        m = _merge(h1, att, rec, fou, w_gate_b, b_gate, w_branch_b, l, rows=rows)
        xmid, h2 = _out_proj(x_parts, m, w_out_b, l, mod[l], g_norm2[l], cfg=cfg, rows=rows)
        a = _ffn_up(h2, w_ffn_in_b, l, cfg=cfg, rows=rows)
        if last:
            (out,) = _ffn_down(a, xmid, w_ffn_out_b, l, mod[l], mod[l], g_final,
                               cfg=cfg, rows=rows, final=True)
        else:
            xall, h1 = _ffn_down(a, xmid, w_ffn_out_b, l, mod[l], mod[l + 1], g_norm1[l + 1],
                                 cfg=cfg, rows=rows, final=False)
            x_parts = (xall,)
    return out.reshape(b, t, d)
```

```python
import functools
import math

import jax
import jax.numpy as jnp
from jax import lax
from jax.experimental import pallas as pl
from jax.experimental.pallas import tpu as pltpu

F32 = jnp.float32
BF16 = jnp.bfloat16

HEAD_DIM = 128
Q_PER_KV = 4
GRID_W = 64
ROPE_THETA = 10000.0
NORM_EPS = 1e-6
LRU_C = 8.0
FOURIER_GROUP = 128

LANES = 128
SUBLANES = 8
VMEM_LIMIT_BYTES = 56 * 1024 * 1024

ROW_TILE = 512
FFN_UP_ROW_TILE = 1024
FFN_DOWN_COL_BLOCKS = 4
ATTN_Q_TILE = 1024
ATTN_SUB_ROWS = 512
LRU_CHUNK = 256
LRU_GAP = SUBLANES


def _params(*semantics):
    return pltpu.CompilerParams(dimension_semantics=semantics,
                                vmem_limit_bytes=VMEM_LIMIT_BYTES)


def _dot(a, b):
    return jnp.dot(a, b, preferred_element_type=F32)


def _dot_nt(a, b):
    return lax.dot_general(a, b, (((1,), (1,)), ((), ())), preferred_element_type=F32)


def _rms(x):
    return x * lax.rsqrt(jnp.mean(x * x, axis=-1, keepdims=True) + NORM_EPS)


def _ada_norm(x, gain, shift, scale):
    return (_rms(x) * gain * (1.0 + scale) + shift).astype(BF16)


def _mod_kernel(c_ref, w_ref, b_ref, o_ref):
    c = c_ref[...]
    s = (c * jax.nn.sigmoid(c)).astype(BF16)
    o_ref[...] = _dot(s, w_ref[...].astype(BF16)) + b_ref[...]


def _modulation(cc, w_mod, b_mod):
    depth, d, n = w_mod.shape
    rows = cc.shape[0]
    tn = 1024
    return pl.pallas_call(
        _mod_kernel,
        grid=(depth, n // tn),
        in_specs=[
            pl.BlockSpec((rows, d), lambda l, j: (0, 0)),
            pl.BlockSpec((None, d, tn), lambda l, j: (l, 0, j)),
            pl.BlockSpec((None, 1, tn), lambda l, j: (l, 0, j)),
        ],
        out_specs=pl.BlockSpec((None, rows, tn), lambda l, j: (l, 0, j)),
        out_shape=jax.ShapeDtypeStruct((depth, rows, n), F32),
        compiler_params=_params("parallel", "parallel"),
        name="modulation",
    )(cc, w_mod, b_mod.reshape(depth, 1, n))


def _split_row_specs(n_lat_tiles, tm, d):
    return [pl.BlockSpec((tm, d), lambda i: (jnp.minimum(i, n_lat_tiles - 1), 0)),
            pl.BlockSpec((tm, d), lambda i: (jnp.maximum(i - n_lat_tiles, 0), 0))]


def _for_row_source(x_ref, c_ref, n_lat_tiles, fn):
    i = pl.program_id(0)
    pl.when(i < n_lat_tiles)(lambda: fn(x_ref))
    pl.when(i >= n_lat_tiles)(lambda: fn(c_ref))


def _prenorm_kernel(x_ref, c_ref, mod_ref, g_ref, h_ref, *, n_lat_tiles):
    def emit(r):
        h_ref[...] = _ada_norm(r[...], g_ref[...], mod_ref[0:1, :], mod_ref[1:2, :])
    _for_row_source(x_ref, c_ref, n_lat_tiles, emit)


def _prenorm(x2d, ctx2d, mod_l, g_norm, *, cfg):
    d = x2d.shape[1]
    rows = x2d.shape[0] + ctx2d.shape[0]
    tm = ROW_TILE
    n_lat_tiles = x2d.shape[0] // tm
    return pl.pallas_call(
        functools.partial(_prenorm_kernel, n_lat_tiles=n_lat_tiles),
        grid=(rows // tm,),
        in_specs=_split_row_specs(n_lat_tiles, tm, d) + [
            pl.BlockSpec((None, 6, d), lambda i: (cfg["mod_row"](i), 0, 0)),
            pl.BlockSpec((1, d), lambda i: (0, 0)),
        ],
        out_specs=pl.BlockSpec((tm, d), lambda i: (i, 0)),
        out_shape=jax.ShapeDtypeStruct((rows, d), BF16),
        compiler_params=_params("parallel"),
        name="prenorm",
    )(x2d, ctx2d, mod_l, g_norm.reshape(1, d))


def _in_proj_kernel(h_ref, w_ref, cos_ref, sin_ref, qg_ref, kg_ref, qkv_ref, rest_ref, qkv_scr,
                    *, n_q_heads, n_kv_heads):
    j = pl.program_id(1)

    @pl.when(j == 0)
    def _():
        qkv_scr[...] = _dot(h_ref[...], w_ref[...])

    @pl.when(j == 1)
    def _():
        rest_ref[...] = _dot(h_ref[...], w_ref[...]).astype(BF16)
        cos = cos_ref[...]
        sin = sin_ref[...]
        q_gain = qg_ref[...] * (HEAD_DIM ** -0.5 * math.log2(math.e))
        k_gain = kg_ref[...]
        n_rot = n_q_heads + n_kv_heads
        for hh in range(n_rot):
            cols = slice(hh * HEAD_DIM, (hh + 1) * HEAD_DIM)
            y = _rms(qkv_scr[:, cols]) * (q_gain if hh < n_q_heads else k_gain)
            y = y * cos + pltpu.roll(y, HEAD_DIM // 2, axis=1) * sin
            qkv_ref[:, cols] = y.astype(BF16)
        qkv_ref[:, n_rot * HEAD_DIM:] = qkv_scr[:, n_rot * HEAD_DIM:].astype(BF16)

    @pl.when(j > 1)
    def _():
        rest_ref[...] = _dot(h_ref[...], w_ref[...]).astype(BF16)


def _in_proj(h1, w_in, l, cosf, sinf, q_gain, k_gain, *, cfg):
    rows, d = h1.shape
    d_in = w_in.shape[2]
    tm, tn = ROW_TILE, cfg["tn_in"]
    assert d_in == 3 * tn
    kern = functools.partial(_in_proj_kernel, n_q_heads=cfg["n_q_heads"],
                             n_kv_heads=cfg["n_kv_heads"])
    return pl.pallas_call(
        kern,
        grid=(rows // tm, d_in // tn),
        in_specs=[
            pl.BlockSpec((tm, d), lambda i, j: (i, 0)),
            pl.BlockSpec((None, d, tn), lambda i, j: (l, 0, j)),
            pl.BlockSpec((tm, HEAD_DIM), lambda i, j: (cfg["rope_block"](i), 0)),
            pl.BlockSpec((tm, HEAD_DIM), lambda i, j: (cfg["rope_block"](i), 0)),
            pl.BlockSpec((1, HEAD_DIM), lambda i, j: (0, 0)),
            pl.BlockSpec((1, HEAD_DIM), lambda i, j: (0, 0)),
        ],
        out_specs=[
            pl.BlockSpec((tm, tn), lambda i, j: (i, 0)),
            pl.BlockSpec((tm, tn), lambda i, j: (i, jnp.maximum(j - 1, 0))),
        ],
        out_shape=[
            jax.ShapeDtypeStruct((rows, tn), BF16),
            jax.ShapeDtypeStruct((rows, d_in - tn), BF16),
        ],
        scratch_shapes=[pltpu.VMEM((tm, tn), F32)],
        compiler_params=_params("parallel", "arbitrary"),
        name="in_proj",
    )(h1, w_in, cosf, sinf, q_gain.reshape(1, HEAD_DIM), k_gain.reshape(1, HEAD_DIM))


def _attn_kernel(q_ref, *refs, latent):
    if latent:
        kl_ref, kc_ref, vl_ref, vc_ref, o_ref, vl_aug, vc_aug = refs
    else:
        kc_ref, vc_ref, o_ref, vc_aug = refs

    @pl.when(pl.program_id(2) == 0)
    def _():
        for v_ref, aug in ((vl_ref, vl_aug), (vc_ref, vc_aug)) if latent else ((vc_ref, vc_aug),):
            lane = lax.broadcasted_iota(jnp.int32, v_ref.shape, 1)
            aug[:, :HEAD_DIM] = v_ref[...]
            aug[:, HEAD_DIM:] = jnp.where(lane == 0, 1.0, 0.0).astype(BF16)

    kc = kc_ref[...]
    sub = min(q_ref.shape[0], ATTN_SUB_ROWS)
    for r0 in range(0, q_ref.shape[0], sub):
        for g in range(Q_PER_KV):
            cols = slice(g * HEAD_DIM, (g + 1) * HEAD_DIM)
            q = q_ref[r0:r0 + sub, cols]
            sc = _dot_nt(q, kc)
            m = jnp.max(sc, axis=-1, keepdims=True)
            if latent:
                sl = _dot_nt(q, kl_ref[...])
                m = jnp.maximum(m, jnp.max(sl, axis=-1, keepdims=True))
            o = _dot(jnp.exp2((sc - m).astype(BF16)), vc_aug[...])
            if latent:
                o = o + _dot(jnp.exp2((sl - m).astype(BF16)), vl_aug[...])
            o_ref[r0:r0 + sub, cols] = (
                o[:, :HEAD_DIM] / o[:, HEAD_DIM:HEAD_DIM + 1]).astype(BF16)


def _attention(qkv, *, cfg, latent):
    b, t, tc = cfg["b"], cfg["t"], cfg["tc"]
    tq = ATTN_Q_TILE if latent else tc
    n_q_tiles = t // tq if latent else 1
    q_blk0 = 0 if latent else b * t // tc
    n_kv = cfg["n_kv_heads"]
    gw = Q_PER_KV * HEAD_DIM
    k_col = cfg["n_q_heads"]
    v_col = k_col + n_kv
    ctx_blk0 = b * t // tc

    q_spec = pl.BlockSpec((tq, gw), lambda bi, h, qi: (q_blk0 + bi * n_q_tiles + qi, h))
    lat_spec = lambda col: pl.BlockSpec((t, HEAD_DIM), lambda bi, h, qi: (bi, col + h))
    ctx_spec = lambda col: pl.BlockSpec((tc, HEAD_DIM), lambda bi, h, qi: (ctx_blk0 + bi, col + h))
    aug = lambda n: pltpu.VMEM((n, 2 * HEAD_DIM), BF16)
    if latent:
        in_specs = [q_spec, lat_spec(k_col), ctx_spec(k_col), lat_spec(v_col), ctx_spec(v_col)]
        scratch = [aug(t), aug(tc)]
    else:
        in_specs = [q_spec, ctx_spec(k_col), ctx_spec(v_col)]
        scratch = [aug(tc)]
    return pl.pallas_call(
        functools.partial(_attn_kernel, latent=latent),
        grid=(b, n_kv, n_q_tiles),
        in_specs=in_specs,
        out_specs=pl.BlockSpec((tq, gw), lambda bi, h, qi: (bi * n_q_tiles + qi, h)),
        out_shape=jax.ShapeDtypeStruct((b * (t if latent else tc), cfg["n_q_heads"] * HEAD_DIM),
                                       BF16),
        scratch_shapes=scratch,
        compiler_params=_params("parallel", "parallel", "arbitrary"),
        name="attention",
    )(*[qkv] * len(in_specs))


def _lru_kernel(ux_ref, ug_ref, cw_ref, cb_ref, w_ref, br_ref, bi_ref, lam_ref, o_ref,
                u_scr, acc_scr, af_scr, bf_scr, yf_scr, ab_scr, bb_scr, yb_scr, *, b, t, tc):
    ch, gap = LRU_CHUNK, LRU_GAP
    n_lat = b * t
    conv_width = cw_ref.shape[0]
    conv_left = conv_width // 2

    lat = (gap, 0, t + gap, t)
    ctx = (gap + b * (t + gap), n_lat, tc + gap, tc)

    zeros_gap = jnp.zeros((gap, LANES), F32)
    for kind, n in ((lat, t), (ctx, tc)):
        for bi in range(b):
            u0 = kind[0] + bi * kind[2]
            r0 = kind[1] + bi * kind[3]
            u_scr[u0 - gap:u0, :] = zeros_gap
            u_scr[u0:u0 + n, :] = ux_ref[r0:r0 + n, :].astype(F32)
    u_end = ctx[0] + b * ctx[2] - gap
    u_scr[u_end:u_end + gap, :] = zeros_gap

    def slab_rows(kind, bi, off):
        start = kind[1] + bi * kind[3] + off
        if not isinstance(start, int):
            start = pl.multiple_of(start, ch)
        return pl.ds(start, ch)

    def coefficients(d, kind, off, a_scr, b_scr):
        lam = lam_ref[d:d + 1, :]
        softplus_neg_lam = jnp.maximum(-lam, 0.0) + jnp.log1p(jnp.exp(-jnp.abs(lam)))
        half_rate = -0.5 * LRU_C * softplus_neg_lam
        w_half = w_ref[:, d * 2 * LANES:(d + 1) * 2 * LANES] * 0.5
        br_half = 0.5 * br_ref[d:d + 1, :]
        bi_half = 0.5 * bi_ref[d:d + 1, :]
        for bi in range(b):
            s0 = kind[0] + bi * kind[2] + off - conv_left
            u = cb_ref[...]
            for j in range(conv_width):
                u = u + cw_ref[j:j + 1, :] * u_scr[pl.ds(s0 + j, ch), :]
            gates = _dot(u.astype(BF16), w_half)
            log_a = jnp.tanh(gates[:, :LANES] + br_half) * half_rate + half_rate
            i_gate = 0.5 * jnp.tanh(gates[:, LANES:] + bi_half) + 0.5
            a = jnp.exp(log_a)
            one_minus_a2 = -jnp.tanh(log_a) * (a * a + 1.0)
            root = jnp.where(one_minus_a2 > 0.0, one_minus_a2 * lax.rsqrt(one_minus_a2), 0.0)
            a_scr[pl.ds(bi, ch, stride=b), :] = a
            b_scr[pl.ds(bi, ch, stride=b), :] = root * i_gate * u

    def emit(rows, y):
        gate = jax.nn.gelu(ug_ref[rows, :].astype(F32), approximate=True)
        o_ref[rows, :] = (y * gate).astype(BF16)

    def pair(kind, f_off, b_off, carry, mode):
        coefficients(0, kind, f_off, af_scr, bf_scr)
        coefficients(1, kind, b_off, ab_scr, bb_scr)

        def two_steps(h, a_scr, b_scr, y_scr, s0, s1):
            r0 = pl.ds(pl.multiple_of(s0 * b, b), b)
            r1 = pl.ds(pl.multiple_of(s1 * b, b), b)
            a0, b0, a1, b1 = a_scr[r0, :], b_scr[r0, :], a_scr[r1, :], b_scr[r1, :]
            y_scr[r0, :] = a0 * h + b0
            h = (a1 * a0) * h + (a1 * b0 + b1)
            y_scr[r1, :] = h
            return h

        def step(s, hs):
            hf, hb = hs
            hf = two_steps(hf, af_scr, bf_scr, yf_scr, 2 * s, 2 * s + 1)
            hb = two_steps(hb, ab_scr, bb_scr, yb_scr, ch - 1 - 2 * s, ch - 2 - 2 * s)
            return hf, hb

        carry = lax.fori_loop(0, ch // 2, step, carry, unroll=16)

        for bi in range(b):
            blk = pl.ds(bi, ch, stride=b)
            rf = slab_rows(kind, bi, f_off)
            rb = slab_rows(kind, bi, b_off)
            if mode == "same":
                emit(rf, yf_scr[blk, :] + yb_scr[blk, :])
            elif mode == "first":
                acc_scr[rf, :] = yf_scr[blk, :]
                acc_scr[rb, :] = yb_scr[blk, :]
            else:
                emit(rf, acc_scr[rf, :] + yf_scr[blk, :])
                emit(rb, acc_scr[rb, :] + yb_scr[blk, :])
        return carry

    n_lat_chunks = t // ch
    half = n_lat_chunks // 2
    zero = jnp.zeros((b, LANES), F32)
    carry = pair(ctx, 0, 0, (zero, zero), "same")
    carry = lax.fori_loop(
        0, half,
        lambda q, hs: pair(lat, q * ch, (n_lat_chunks - 1 - q) * ch, hs, "first"), carry)
    lax.fori_loop(
        0, half,
        lambda q, hs: pair(lat, (half + q) * ch, (half - 1 - q) * ch, hs, "second"), carry)


def _lru(rest, conv_w, conv_b, w_bd, b_r, b_i, lam, *, cfg):
    rows = rest.shape[0]
    d_lru = conv_w.shape[1]
    ux_blk = cfg["ux_col"] // LANES
    ug_blk = cfg["ug_col"] // LANES
    nb = cfg["b"]
    assert cfg["tc"] == LRU_CHUNK and (cfg["t"] // LRU_CHUNK) % 2 == 0
    kern = functools.partial(_lru_kernel, b=nb, t=cfg["t"], tc=cfg["tc"])
    vec = lambda n: pl.BlockSpec((n, LANES), lambda c: (0, c))
    return pl.pallas_call(
        kern,
        grid=(d_lru // LANES,),
        in_specs=[
            pl.BlockSpec((rows, LANES), lambda c: (0, ux_blk + c), pipeline_mode=pl.Buffered(1)),
            pl.BlockSpec((rows, LANES), lambda c: (0, ug_blk + c), pipeline_mode=pl.Buffered(1)),
            vec(conv_w.shape[0]), vec(1),
            pl.BlockSpec((None, LANES, 4 * LANES), lambda c: (c, 0, 0)),
            vec(2), vec(2), vec(2),
        ],
        out_specs=pl.BlockSpec((rows, LANES), lambda c: (0, c)),
        out_shape=jax.ShapeDtypeStruct((rows, d_lru), BF16),
        scratch_shapes=[
            pltpu.VMEM((rows + (2 * nb + 1) * LRU_GAP, LANES), F32),
            pltpu.VMEM((rows, LANES), F32),
        ] + [pltpu.VMEM((nb * LRU_CHUNK, LANES), F32)] * 6,
        compiler_params=_params("parallel"),
        name="rglru",
    )(rest, rest, conv_w, conv_b.reshape(1, d_lru), w_bd, b_r, b_i, lam)


def _lru_block_diag(w_r, w_i):
    _, nb, bw, _ = w_r.shape
    per = LANES // bw
    eye = jnp.eye(per, dtype=w_r.dtype)

    def bd(w):
        w = w.reshape(nb // per, per, bw, bw)
        return jnp.einsum("cipq,ij->cipjq", w, eye).reshape(nb // per, LANES, LANES)

    return jnp.concatenate([bd(w_r[0]), bd(w_i[0]), bd(w_r[1]), bd(w_i[1])], axis=-1).astype(BF16)


def _fourier_kernel(u_ref, cs_ref, w2_ref, o_ref, pq_scr, *, t, n_groups, scale):
    for g in range(n_groups):
        cols = slice(g * FOURIER_GROUP, (g + 1) * FOURIER_GROUP)
        pq = _dot(u_ref[:, cols], w2_ref[...])
        pq_scr[0:t, cols] = pq[:, :FOURIER_GROUP].astype(BF16)
        pq_scr[t:2 * t, cols] = pq[:, FOURIER_GROUP:].astype(BF16)
    o_ref[...] = (_dot(cs_ref[...], pq_scr[...]) * scale).astype(BF16)


def _dft_tables(t):
    def angles(n):
        k = jnp.arange(n, dtype=jnp.int32)
        return ((k[:, None] * k[None, :]) % n).astype(F32) * (2.0 * math.pi / n)
    at = angles(t)
    ac = angles(FOURIER_GROUP)
    cs = jnp.concatenate([jnp.cos(at), -jnp.sin(at)], axis=1).astype(BF16)
    w2 = jnp.concatenate([jnp.cos(ac), jnp.sin(ac)], axis=1).astype(BF16)
    return cs, w2


def _fourier(rest, cs, w2, *, cfg, seq_len, row_blk0):
    d_f = cfg["d_branch"]
    wcols = min(d_f, 512)
    uf_blk = cfg["uf_col"] // wcols
    kern = functools.partial(_fourier_kernel, t=seq_len, n_groups=wcols // FOURIER_GROUP,
                             scale=(seq_len * FOURIER_GROUP) ** -0.5)
    return pl.pallas_call(
        kern,
        grid=(cfg["b"], d_f // wcols),
        in_specs=[
            pl.BlockSpec((seq_len, wcols), lambda bi, hf: (row_blk0 + bi, uf_blk + hf)),
            pl.BlockSpec((seq_len, 2 * seq_len), lambda bi, hf: (0, 0),
                         pipeline_mode=pl.Buffered(1)),
            pl.BlockSpec((FOURIER_GROUP, 2 * FOURIER_GROUP), lambda bi, hf: (0, 0)),
        ],
        out_specs=pl.BlockSpec((seq_len, wcols), lambda bi, hf: (bi, hf)),
        out_shape=jax.ShapeDtypeStruct((cfg["b"] * seq_len, d_f), BF16),
        scratch_shapes=[pltpu.VMEM((2 * seq_len, wcols), BF16)],
        compiler_params=_params("parallel", "parallel"),
        name="fourier",
    )(rest, cs, w2)


def _merge_kernel(*refs, n_parts, n_lat_tiles):
    h_ref = refs[0]
    att_refs = refs[1:1 + n_parts]
    rec_ref = refs[1 + n_parts]
    fou_refs = refs[2 + n_parts:2 + 2 * n_parts]
    wgs, bgs, wbs = (refs[2 + 2 * n_parts + 3 * k:5 + 2 * n_parts + 3 * k] for k in range(3))
    o_ref = refs[-1]
    is_latent = pl.program_id(0) < n_lat_tiles

    def rows_of(parts):
        if len(parts) == 1:
            return parts[0][...]
        return jnp.where(is_latent, parts[0][...], parts[1][...])

    h = h_ref[...]
    acc = None
    for y, wg, bg, wb in zip((rows_of(att_refs), rec_ref[...], rows_of(fou_refs)), wgs, bgs, wbs):
        gate = jax.nn.sigmoid(_dot(h, wg[...]) + bg[...])
        term = gate * _dot(y, wb[...])
        acc = term if acc is None else acc + term
    o_ref[...] = acc.astype(BF16)


def _merge(h, att_parts, rec, fou_parts, w_gate, b_gate, w_branch, l, *, rows):
    d = h.shape[1]
    d_b = rec.shape[1]
    tm, tn = ROW_TILE, 512
    nj = d // tn
    n_parts = len(att_parts)
    n_lat_tiles = att_parts[0].shape[0] // tm
    y_spec = pl.BlockSpec((tm, d_b), lambda i, j: (i, 0))
    part_specs = [pl.BlockSpec((tm, d_b), lambda i, j: (jnp.minimum(i, n_lat_tiles - 1), 0)),
                  pl.BlockSpec((tm, d_b), lambda i, j: (jnp.maximum(i - n_lat_tiles, 0), 0))
                  ][:n_parts]
    wg_spec = lambda k: pl.BlockSpec((None, d, tn), lambda i, j: (l, 0, k * nj + j))
    bg_spec = lambda k: pl.BlockSpec((None, 1, tn), lambda i, j: (l, 0, k * nj + j))
    wb_spec = lambda k: pl.BlockSpec((None, None, d_b, tn), lambda i, j: (l, k, 0, j))
    bg = b_gate.reshape(b_gate.shape[0], 1, -1)
    return pl.pallas_call(
        functools.partial(_merge_kernel, n_parts=n_parts, n_lat_tiles=n_lat_tiles),
        grid=(rows // tm, nj),
        in_specs=[pl.BlockSpec((tm, d), lambda i, j: (i, 0))] + part_specs + [y_spec] + part_specs
                 + [wg_spec(0), wg_spec(1), wg_spec(2), bg_spec(0), bg_spec(1), bg_spec(2),
                    wb_spec(0), wb_spec(1), wb_spec(2)],
        out_specs=pl.BlockSpec((tm, tn), lambda i, j: (i, j)),
        out_shape=jax.ShapeDtypeStruct((rows, d), BF16),
        compiler_params=_params("parallel", "arbitrary"),
        name="gated_merge",
    )(h, *att_parts, rec, *fou_parts, w_gate, w_gate, w_gate, bg, bg, bg,
      w_branch, w_branch, w_branch)


def _out_proj_kernel(*refs, n_lat_tiles):
    x_refs, (m_ref, w_ref, mod_ref, g_ref, o_ref, h_ref) = refs[:-6], refs[-6:]
    delta = mod_ref[2:3, :] * _dot(m_ref[...], w_ref[...])

    def emit(r):
        y = r[...] + delta
        o_ref[...] = y
        h_ref[...] = _ada_norm(y, g_ref[...], mod_ref[3:4, :], mod_ref[4:5, :])

    if len(x_refs) == 2:
        _for_row_source(*x_refs, n_lat_tiles, emit)
    else:
        emit(x_refs[0])


def _out_proj(x_parts, m, w_out, l, mod_l, g_norm2, *, cfg, rows):
    d = m.shape[1]
    tm = ROW_TILE
    row_spec = pl.BlockSpec((tm, d), lambda i: (i, 0))
    n_lat_tiles = x_parts[0].shape[0] // tm
    x_specs = _split_row_specs(n_lat_tiles, tm, d) if len(x_parts) == 2 else [row_spec]
    return pl.pallas_call(
        functools.partial(_out_proj_kernel, n_lat_tiles=n_lat_tiles),
        grid=(rows // tm,),
        in_specs=x_specs + [
            row_spec,
            pl.BlockSpec((None, d, d), lambda i: (l, 0, 0), pipeline_mode=pl.Buffered(1)),
            pl.BlockSpec((None, 6, d), lambda i: (cfg["mod_row"](i), 0, 0)),
            pl.BlockSpec((1, d), lambda i: (0, 0)),
        ],
        out_specs=[row_spec, row_spec],
        out_shape=[jax.ShapeDtypeStruct((rows, d), F32), jax.ShapeDtypeStruct((rows, d), BF16)],
        compiler_params=_params("parallel"),
        name="out_proj",
    )(*x_parts, m, w_out, mod_l, g_norm2.reshape(1, d))


def _ffn_up_kernel(h_ref, wg_ref, wu_ref, *refs):
    n_cast = (len(refs) - 1) // 2
    a_ref = refs[n_cast]
    h = h_ref[...]
    gt = _dot(h, wg_ref[...])
    up = _dot(h, wu_ref[...])
    a_ref[...] = (gt * jax.nn.sigmoid(gt) * up).astype(BF16)
    for src, dst in zip(refs[:n_cast], refs[n_cast + 1:]):
        if len(dst.shape) == 2:
            dst[...] = src[...].astype(BF16)
        else:
            for g in range(dst.shape[0]):
                dst[g] = src[:, g * dst.shape[2]:(g + 1) * dst.shape[2]].astype(BF16)


def _col_blocked(w, n_groups):
    l, r, c = w.shape
    return w.reshape(l, r, n_groups, c // n_groups).transpose(0, 2, 1, 3)


def _ffn_up(h2, w_ffn_in, l, *, cfg, rows, cast_srcs=(), cast_layer=None):
    d = h2.shape[1]
    d_ff = w_ffn_in.shape[2] // 2
    tm, tf = FFN_UP_ROW_TILE, cfg["tf"]
    nk = d_ff // tf
    n_steps = (rows // tm) * nk
    cast_in, cast_out, cast_shape = [], [], []
    for w, groups in cast_srcs:
        _, r, c = w.shape
        br = next(x for x in (16, 32, 64, 128, 256) if r % x == 0 and r // x <= n_steps)
        blk = lambda i, k, n=r // br: jnp.minimum(i * nk + k, n - 1)
        cast_in.append(pl.BlockSpec((None, br, c), lambda i, k, blk=blk: (cast_layer, blk(i, k), 0)))
        if groups == 1:
            cast_out.append(pl.BlockSpec((None, br, c), lambda i, k, blk=blk: (0, blk(i, k), 0)))
            cast_shape.append(jax.ShapeDtypeStruct((1, r, c), BF16))
        else:
            cast_out.append(pl.BlockSpec((None, groups, br, c // groups),
                                         lambda i, k, blk=blk: (0, 0, blk(i, k), 0)))
            cast_shape.append(jax.ShapeDtypeStruct((1, groups, r, c // groups), BF16))
    return pl.pallas_call(
        _ffn_up_kernel,
        grid=(rows // tm, nk),
        in_specs=[
            pl.BlockSpec((tm, d), lambda i, k: (i, 0)),
            pl.BlockSpec((None, d, tf), lambda i, k: (l, 0, k)),
            pl.BlockSpec((None, d, tf), lambda i, k: (l, 0, nk + k)),
        ] + cast_in,
        out_specs=[pl.BlockSpec((tm, tf), lambda i, k: (i, k))] + cast_out,
        out_shape=[jax.ShapeDtypeStruct((rows, d_ff), BF16)] + cast_shape,
        compiler_params=_params("arbitrary", "arbitrary"),
        name="ffn_up",
    )(h2, w_ffn_in, w_ffn_in, *[w for w, _ in cast_srcs])


def _ffn_down_kernel(a_ref, x_ref, wo_ref, mod_ref, nmod_ref, g_ref, *refs, nj, d, final):
    out_refs, y_scr = refs[:-1], refs[-1]
    i, j = pl.program_id(0), pl.program_id(1)
    cur = i % 2

    @pl.when((i == 0) & (j == 0))
    def _():
        y_scr[1] = jnp.zeros(y_scr.shape[1:], F32)

    prv = y_scr.at[1 - cur]
    ssq = sum(jnp.sum(prv[jj] * prv[jj], axis=-1, keepdims=True) for jj in range(nj))
    normed = prv[j] * lax.rsqrt(ssq * (1.0 / d) + NORM_EPS) * g_ref[...]
    if final:
        out_refs[0][...] = normed
    else:
        out_refs[1][...] = (normed * (1.0 + nmod_ref[1:2, :]) + nmod_ref[0:1, :]).astype(BF16)

    jw = jnp.where(i == pl.num_programs(0) - 1, nj - 1, j)
    y = x_ref[...] + mod_ref[5:6, :] * _dot(a_ref[...], wo_ref[jw])
    if not final:
        out_refs[0][...] = y
    y_scr[cur, j] = y


def _ffn_down(a, xmid, w_ffn_out, l, mod_l, next_mod, gain, *, cfg, rows, final):
    d = xmid.shape[1]
    d_ff = a.shape[1]
    tm = ROW_TILE
    _, nj, _, tn = w_ffn_out.shape
    n_tiles = rows // tm
    cur = lambda i: jnp.minimum(i, n_tiles - 1)
    prev = lambda i: jnp.maximum(i - 1, 0)
    cur_j = lambda i, j: jnp.where(i == n_tiles, nj - 1, j)
    prev_j = lambda i, j: jnp.where(i == 0, 0, j)
    cur_spec = pl.BlockSpec((tm, tn), lambda i, j: (cur(i), cur_j(i, j)))
    prev_spec = pl.BlockSpec((tm, tn), lambda i, j: (prev(i), prev_j(i, j)))
    if final:
        out_specs, out_shape = [prev_spec], [jax.ShapeDtypeStruct((rows, d), F32)]
    else:
        out_specs = [cur_spec, prev_spec]
        out_shape = [jax.ShapeDtypeStruct((rows, d), F32), jax.ShapeDtypeStruct((rows, d), BF16)]
    return pl.pallas_call(
        functools.partial(_ffn_down_kernel, nj=nj, d=d, final=final),
        grid=(n_tiles + 1, nj),
        in_specs=[
            pl.BlockSpec((tm, d_ff), lambda i, j: (cur(i), 0)),
            cur_spec,
            pl.BlockSpec((None, nj, d_ff, tn), lambda i, j: (l, 0, 0, 0),
                         pipeline_mode=pl.Buffered(1)),
            pl.BlockSpec((None, 6, tn), lambda i, j: (cfg["mod_row"](cur(i)), 0, cur_j(i, j))),
            pl.BlockSpec((None, 6, tn), lambda i, j: (cfg["mod_row"](prev(i)), 0, j)),
            pl.BlockSpec((1, tn), lambda i, j: (0, j)),
        ],
        out_specs=out_specs,
        out_shape=out_shape,
        scratch_shapes=[pltpu.VMEM((2, nj, tm, tn), F32)],
        compiler_params=_params("arbitrary", "arbitrary"),
        name="ffn_down",
    )(a, xmid, w_ffn_out, mod_l, next_mod, gain.reshape(1, d))


def _rope_tables(t, tile_rows):
    pairs = HEAD_DIM // 4
    rows = t // GRID_W
    row = jnp.repeat(jnp.arange(rows, dtype=F32), GRID_W)
    col = jnp.tile(jnp.arange(GRID_W, dtype=F32), rows)
    inv = ROPE_THETA ** (-jnp.arange(pairs, dtype=F32) / pairs)
    ang = jnp.concatenate([row[:, None] * inv, col[:, None] * inv], axis=-1)
    cos, sin = jnp.cos(ang), jnp.sin(ang)
    cosf = jnp.concatenate([cos, cos], axis=-1)
    sinf = jnp.concatenate([-sin, sin], axis=-1)
    cosf = jnp.concatenate([cosf, jnp.ones((tile_rows, HEAD_DIM), F32)], axis=0)
    sinf = jnp.concatenate([sinf, jnp.zeros((tile_rows, HEAD_DIM), F32)], axis=0)
    return cosf, sinf


def kernel(x, c, ctx, c_ctx, w_mod, b_mod, g_norm1, g_norm2, w_in, q_gain, k_gain, conv_w, conv_b, lru_w_r, lru_b_r, lru_w_i, lru_b_i, lru_lambda, w_branch, w_gate, b_gate, w_out, w_ffn_in, w_ffn_out, g_final):
    b, t, d = x.shape
    tc = ctx.shape[1]
    depth = w_in.shape[0]
    d_branch = w_branch.shape[2]
    n_q_heads = d_branch // HEAD_DIM
    n_kv_heads = n_q_heads // Q_PER_KV
    d_kv = n_kv_heads * HEAD_DIM
    d_ff = w_ffn_out.shape[1]
    n_lat, n_ctx = b * t, b * tc
    assert b == SUBLANES and t % ROW_TILE == 0 and n_ctx % FFN_UP_ROW_TILE == 0
    assert tc % LRU_CHUNK == 0 and t % LRU_CHUNK == 0
    tiles_per_batch = t // ROW_TILE
    n_lat_tiles = n_lat // ROW_TILE

    cfg = dict(
        b=b, t=t, tc=tc, d_branch=d_branch, n_q_heads=n_q_heads, n_kv_heads=n_kv_heads,
        tn_in=d_branch + 2 * d_kv,
        ux_col=0, ug_col=d_branch, uf_col=2 * d_branch,
        tf=512 if d_ff % 512 == 0 else 256,
        mod_row=lambda i: jnp.minimum(i // tiles_per_batch, b),
        rope_block=lambda i: jnp.where(i < n_lat_tiles, i % tiles_per_batch, tiles_per_batch),
    )

    x_parts = (x.reshape(n_lat, d), ctx.reshape(n_ctx, d))
    cc = jnp.concatenate([c, c_ctx[None, :], jnp.zeros((2 * SUBLANES - b - 1, d), F32)], axis=0)
    mod = _modulation(cc, w_mod, b_mod).reshape(depth, 2 * SUBLANES, 6, d)

    w_f32 = ((w_in, 1), (w_gate, 1), (w_branch.reshape(depth, -1, d), 1), (w_out, 1),
             (w_ffn_in, 1), (w_ffn_out, FFN_DOWN_COL_BLOCKS))
    wb = [(w[0:1] if g == 1 else _col_blocked(w[0:1], g)).astype(BF16) for w, g in w_f32]

    cosf, sinf = _rope_tables(t, ROW_TILE)
    cs_lat, w2 = _dft_tables(t)
    cs_ctx, _ = _dft_tables(tc)

    h1 = _prenorm(*x_parts, mod[0], g_norm1[0], cfg=cfg)
    out = None
    for l in range(depth):
        last = l == depth - 1
        rows = n_lat if last else n_lat + n_ctx
        w_in_b, w_gate_b, w_branch_b, w_out_b, w_ffn_in_b, w_ffn_out_b = wb
        w_branch_b = w_branch_b.reshape((1,) + w_branch.shape[1:])
        qkv, rest = _in_proj(h1, w_in_b, 0, cosf, sinf, q_gain[l], k_gain[l], cfg=cfg)
        att = (_attention(qkv, cfg=cfg, latent=True),)
        rec = _lru(rest, conv_w[l], conv_b[l], _lru_block_diag(lru_w_r[l], lru_w_i[l]),
                   lru_b_r[l], lru_b_i[l], lru_lambda[l], cfg=cfg)
        fou = (_fourier(rest, cs_lat, w2, cfg=cfg, seq_len=t, row_blk0=0),)
        if not last:
            att += (_attention(qkv, cfg=cfg, latent=False),)
            fou += (_fourier(rest, cs_ctx, w2, cfg=cfg, seq_len=tc, row_blk0=n_lat // tc),)
        m = _merge(h1, att, rec, fou, w_gate_b, b_gate[l:l + 1], w_branch_b, 0, rows=rows)
        xmid, h2 = _out_proj(x_parts, m, w_out_b, 0, mod[l], g_norm2[l], cfg=cfg, rows=rows)
        if last:
            (a,) = _ffn_up(h2, w_ffn_in_b, 0, cfg=cfg, rows=rows)
            (out,) = _ffn_down(a, xmid, w_ffn_out_b, 0, mod[l], mod[l], g_final,
                               cfg=cfg, rows=rows, final=True)
        else:
            a, *wb = _ffn_up(h2, w_ffn_in_b, 0, cfg=cfg, rows=rows, cast_srcs=w_f32,
                             cast_layer=l + 1)
            xall, h1 = _ffn_down(a, xmid, w_ffn_out_b, 0, mod[l], mod[l + 1], g_norm1[l + 1],
                                 cfg=cfg, rows=rows, final=False)
            x_parts = (xall,)
    return out.reshape(b, t, d)
```

```python
import functools
import math

import jax
import jax.numpy as jnp
from jax import lax
from jax.experimental import pallas as pl
from jax.experimental.pallas import tpu as pltpu

F32 = jnp.float32
BF16 = jnp.bfloat16

HEAD_DIM = 128
Q_PER_KV = 4
GRID_W = 64
ROPE_THETA = 10000.0
NORM_EPS = 1e-6
LRU_C = 8.0
FOURIER_GROUP = 128

LANES = 128
SUBLANES = 8
VMEM_LIMIT_BYTES = 56 * 1024 * 1024

ROW_TILE = 512
IN_PROJ_ROW_TILE = 1024
FFN_UP_ROW_TILE = 2048
FFN_DOWN_COL_BLOCKS = 4
ATTN_Q_TILE = 1024
ATTN_SUB_ROWS = 512
LRU_CHUNK = 256
LRU_GAP = SUBLANES


def _params(*semantics):
    return pltpu.CompilerParams(dimension_semantics=semantics,
                                vmem_limit_bytes=VMEM_LIMIT_BYTES)


def _dot(a, b):
    return jnp.dot(a, b, preferred_element_type=F32)


def _dot_nt(a, b):
    return lax.dot_general(a, b, (((1,), (1,)), ((), ())), preferred_element_type=F32)


def _rms(x):
    return x * lax.rsqrt(jnp.mean(x * x, axis=-1, keepdims=True) + NORM_EPS)


def _ada_norm(x, gain, shift, scale):
    return (_rms(x) * gain * (1.0 + scale) + shift).astype(BF16)


def _mod_kernel(c_ref, w_ref, b_ref, o_ref):
    c = c_ref[...]
    s = (c * jax.nn.sigmoid(c)).astype(BF16)
    o_ref[...] = _dot(s, w_ref[...].astype(BF16)) + b_ref[...]


def _modulation(cc, w_mod, b_mod):
    depth, d, n = w_mod.shape
    rows = cc.shape[0]
    tn = 1024
    return pl.pallas_call(
        _mod_kernel,
        grid=(depth, n // tn),
        in_specs=[
            pl.BlockSpec((rows, d), lambda l, j: (0, 0)),
            pl.BlockSpec((None, d, tn), lambda l, j: (l, 0, j)),
            pl.BlockSpec((None, 1, tn), lambda l, j: (l, 0, j)),
        ],
        out_specs=pl.BlockSpec((None, rows, tn), lambda l, j: (l, 0, j)),
        out_shape=jax.ShapeDtypeStruct((depth, rows, n), F32),
        compiler_params=_params("parallel", "parallel"),
        name="modulation",
    )(cc, w_mod, b_mod.reshape(depth, 1, n))


def _split_row_specs(n_lat_tiles, tm, d):
    return [pl.BlockSpec((tm, d), lambda i: (jnp.minimum(i, n_lat_tiles - 1), 0)),
            pl.BlockSpec((tm, d), lambda i: (jnp.maximum(i - n_lat_tiles, 0), 0))]


def _for_row_source(x_ref, c_ref, n_lat_tiles, fn):
    i = pl.program_id(0)
    pl.when(i < n_lat_tiles)(lambda: fn(x_ref))
    pl.when(i >= n_lat_tiles)(lambda: fn(c_ref))


def _prenorm_kernel(x_ref, c_ref, mod_ref, g_ref, h_ref, *, n_lat_tiles):
    def emit(r):
        h_ref[...] = _ada_norm(r[...], g_ref[...], mod_ref[0:1, :], mod_ref[1:2, :])
    _for_row_source(x_ref, c_ref, n_lat_tiles, emit)


def _prenorm(x2d, ctx2d, mod_l, g_norm, *, cfg):
    d = x2d.shape[1]
    rows = x2d.shape[0] + ctx2d.shape[0]
    tm = ROW_TILE
    n_lat_tiles = x2d.shape[0] // tm
    return pl.pallas_call(
        functools.partial(_prenorm_kernel, n_lat_tiles=n_lat_tiles),
        grid=(rows // tm,),
        in_specs=_split_row_specs(n_lat_tiles, tm, d) + [
            pl.BlockSpec((None, 6, d), lambda i: (cfg["mod_row"](i), 0, 0)),
            pl.BlockSpec((1, d), lambda i: (0, 0)),
        ],
        out_specs=pl.BlockSpec((tm, d), lambda i: (i, 0)),
        out_shape=jax.ShapeDtypeStruct((rows, d), BF16),
        compiler_params=_params("parallel"),
        name="prenorm",
    )(x2d, ctx2d, mod_l, g_norm.reshape(1, d))


def _in_proj_kernel(h_ref, w_ref, cos_ref, sin_ref, qg_ref, kg_ref, qkv_ref, rest_ref, qkv_scr,
                    *, n_q_heads, n_kv_heads):
    j = pl.program_id(1)

    @pl.when(j == 0)
    def _():
        qkv_scr[...] = _dot(h_ref[...], w_ref[...])

    @pl.when(j == 1)
    def _():
        rest_ref[...] = _dot(h_ref[...], w_ref[...]).astype(BF16)
        cos = cos_ref[...]
        sin = sin_ref[...]
        q_gain = qg_ref[...] * (HEAD_DIM ** -0.5 * math.log2(math.e))
        k_gain = kg_ref[...]
        n_rot = n_q_heads + n_kv_heads
        for hh in range(n_rot):
            cols = slice(hh * HEAD_DIM, (hh + 1) * HEAD_DIM)
            y = _rms(qkv_scr[:, cols]) * (q_gain if hh < n_q_heads else k_gain)
            y = y * cos + pltpu.roll(y, HEAD_DIM // 2, axis=1) * sin
            qkv_ref[:, cols] = y.astype(BF16)
        qkv_ref[:, n_rot * HEAD_DIM:] = qkv_scr[:, n_rot * HEAD_DIM:].astype(BF16)

    @pl.when(j > 1)
    def _():
        rest_ref[...] = _dot(h_ref[...], w_ref[...]).astype(BF16)


def _in_proj(h1, w_in, l, cosf, sinf, q_gain, k_gain, *, cfg):
    rows, d = h1.shape
    d_in = w_in.shape[2]
    tm, tn = IN_PROJ_ROW_TILE, cfg["tn_in"]
    assert d_in == 3 * tn
    kern = functools.partial(_in_proj_kernel, n_q_heads=cfg["n_q_heads"],
                             n_kv_heads=cfg["n_kv_heads"])
    return pl.pallas_call(
        kern,
        grid=(rows // tm, d_in // tn),
        in_specs=[
            pl.BlockSpec((tm, d), lambda i, j: (i, 0)),
            pl.BlockSpec((None, d, tn), lambda i, j: (l, 0, j)),
            pl.BlockSpec((tm, HEAD_DIM), lambda i, j: (cfg["rope_block"](i), 0)),
            pl.BlockSpec((tm, HEAD_DIM), lambda i, j: (cfg["rope_block"](i), 0)),
            pl.BlockSpec((1, HEAD_DIM), lambda i, j: (0, 0)),
            pl.BlockSpec((1, HEAD_DIM), lambda i, j: (0, 0)),
        ],
        out_specs=[
            pl.BlockSpec((tm, tn), lambda i, j: (i, 0)),
            pl.BlockSpec((tm, tn), lambda i, j: (i, jnp.maximum(j - 1, 0))),
        ],
        out_shape=[
            jax.ShapeDtypeStruct((rows, tn), BF16),
            jax.ShapeDtypeStruct((rows, d_in - tn), BF16),
        ],
        scratch_shapes=[pltpu.VMEM((tm, tn), F32)],
        compiler_params=_params("parallel", "arbitrary"),
        name="in_proj",
    )(h1, w_in, cosf, sinf, q_gain.reshape(1, HEAD_DIM), k_gain.reshape(1, HEAD_DIM))


def _attn_kernel(q_ref, *refs, latent):
    if latent:
        kl_ref, kc_ref, vl_ref, vc_ref, o_ref, vl_aug, vc_aug = refs
    else:
        kc_ref, vc_ref, o_ref, vc_aug = refs

    @pl.when(pl.program_id(2) == 0)
    def _():
        for v_ref, aug in ((vl_ref, vl_aug), (vc_ref, vc_aug)) if latent else ((vc_ref, vc_aug),):
            lane = lax.broadcasted_iota(jnp.int32, v_ref.shape, 1)
            aug[:, :HEAD_DIM] = v_ref[...]
            aug[:, HEAD_DIM:] = jnp.where(lane == 0, 1.0, 0.0).astype(BF16)

    kc = kc_ref[...]
    sub = min(q_ref.shape[0], ATTN_SUB_ROWS)
    for r0 in range(0, q_ref.shape[0], sub):
        for g in range(Q_PER_KV):
            cols = slice(g * HEAD_DIM, (g + 1) * HEAD_DIM)
            q = q_ref[r0:r0 + sub, cols]
            sc = _dot_nt(q, kc)
            m = jnp.max(sc, axis=-1, keepdims=True)
            if latent:
                sl = _dot_nt(q, kl_ref[...])
                m = jnp.maximum(m, jnp.max(sl, axis=-1, keepdims=True))
            o = _dot(jnp.exp2((sc - m).astype(BF16)), vc_aug[...])
            if latent:
                o = o + _dot(jnp.exp2((sl - m).astype(BF16)), vl_aug[...])
            o_ref[r0:r0 + sub, cols] = (
                o[:, :HEAD_DIM] / o[:, HEAD_DIM:HEAD_DIM + 1]).astype(BF16)


def _attention(qkv, *, cfg, latent):
    b, t, tc = cfg["b"], cfg["t"], cfg["tc"]
    tq = ATTN_Q_TILE if latent else tc
    n_q_tiles = t // tq if latent else 1
    q_blk0 = 0 if latent else b * t // tc
    n_kv = cfg["n_kv_heads"]
    gw = Q_PER_KV * HEAD_DIM
    k_col = cfg["n_q_heads"]
    v_col = k_col + n_kv
    ctx_blk0 = b * t // tc

    q_spec = pl.BlockSpec((tq, gw), lambda bi, h, qi: (q_blk0 + bi * n_q_tiles + qi, h))
    lat_spec = lambda col: pl.BlockSpec((t, HEAD_DIM), lambda bi, h, qi: (bi, col + h))
    ctx_spec = lambda col: pl.BlockSpec((tc, HEAD_DIM), lambda bi, h, qi: (ctx_blk0 + bi, col + h))
    aug = lambda n: pltpu.VMEM((n, 2 * HEAD_DIM), BF16)
    if latent:
        in_specs = [q_spec, lat_spec(k_col), ctx_spec(k_col), lat_spec(v_col), ctx_spec(v_col)]
        scratch = [aug(t), aug(tc)]
    else:
        in_specs = [q_spec, ctx_spec(k_col), ctx_spec(v_col)]
        scratch = [aug(tc)]
    return pl.pallas_call(
        functools.partial(_attn_kernel, latent=latent),
        grid=(b, n_kv, n_q_tiles),
        in_specs=in_specs,
        out_specs=pl.BlockSpec((tq, gw), lambda bi, h, qi: (bi * n_q_tiles + qi, h)),
        out_shape=jax.ShapeDtypeStruct((b * (t if latent else tc), cfg["n_q_heads"] * HEAD_DIM),
                                       BF16),
        scratch_shapes=scratch,
        compiler_params=_params("parallel", "parallel", "arbitrary"),
        name="attention",
    )(*[qkv] * len(in_specs))


def _lru_kernel(ux_ref, ug_ref, cw_ref, cb_ref, w_ref, br_ref, bi_ref, lam_ref, o_ref,
                u_scr, acc_scr, af_scr, bf_scr, yf_scr, ab_scr, bb_scr, yb_scr, *, b, t, tc):
    ch, gap = LRU_CHUNK, LRU_GAP
    n_lat = b * t
    conv_width = cw_ref.shape[0]
    conv_left = conv_width // 2

    lat = (gap, 0, t + gap, t)
    ctx = (gap + b * (t + gap), n_lat, tc + gap, tc)

    zeros_gap = jnp.zeros((gap, LANES), F32)
    for kind, n in ((lat, t), (ctx, tc)):
        for bi in range(b):
            u0 = kind[0] + bi * kind[2]
            r0 = kind[1] + bi * kind[3]
            u_scr[u0 - gap:u0, :] = zeros_gap
            u_scr[u0:u0 + n, :] = ux_ref[r0:r0 + n, :].astype(F32)
    u_end = ctx[0] + b * ctx[2] - gap
    u_scr[u_end:u_end + gap, :] = zeros_gap

    def slab_rows(kind, bi, off):
        start = kind[1] + bi * kind[3] + off
        if not isinstance(start, int):
            start = pl.multiple_of(start, ch)
        return pl.ds(start, ch)

    def coefficients(d, kind, off, a_scr, b_scr):
        lam = lam_ref[d:d + 1, :]
        softplus_neg_lam = jnp.maximum(-lam, 0.0) + jnp.log1p(jnp.exp(-jnp.abs(lam)))
        half_rate = -0.5 * LRU_C * softplus_neg_lam
        w_half = w_ref[:, d * 2 * LANES:(d + 1) * 2 * LANES] * 0.5
        br_half = 0.5 * br_ref[d:d + 1, :]
        bi_half = 0.5 * bi_ref[d:d + 1, :]
        for bi in range(b):
            s0 = kind[0] + bi * kind[2] + off - conv_left
            u = cb_ref[...]
            for j in range(conv_width):
                u = u + cw_ref[j:j + 1, :] * u_scr[pl.ds(s0 + j, ch), :]
            gates = _dot(u.astype(BF16), w_half)
            log_a = jnp.tanh(gates[:, :LANES] + br_half) * half_rate + half_rate
            i_gate = 0.5 * jnp.tanh(gates[:, LANES:] + bi_half) + 0.5
            a = jnp.exp(log_a)
            one_minus_a2 = -jnp.tanh(log_a) * (a * a + 1.0)
            root = jnp.where(one_minus_a2 > 0.0, one_minus_a2 * lax.rsqrt(one_minus_a2), 0.0)
            a_scr[pl.ds(bi, ch, stride=b), :] = a
            b_scr[pl.ds(bi, ch, stride=b), :] = root * i_gate * u

    def emit(rows, y):
        gate = jax.nn.gelu(ug_ref[rows, :].astype(F32), approximate=True)
        o_ref[rows, :] = (y * gate).astype(BF16)

    def pair(kind, f_off, b_off, carry, mode):
        coefficients(0, kind, f_off, af_scr, bf_scr)
        coefficients(1, kind, b_off, ab_scr, bb_scr)

        def two_steps(h, a_scr, b_scr, y_scr, s0, s1):
            r0 = pl.ds(pl.multiple_of(s0 * b, b), b)
            r1 = pl.ds(pl.multiple_of(s1 * b, b), b)
            a0, b0, a1, b1 = a_scr[r0, :], b_scr[r0, :], a_scr[r1, :], b_scr[r1, :]
            y_scr[r0, :] = a0 * h + b0
            h = (a1 * a0) * h + (a1 * b0 + b1)
            y_scr[r1, :] = h
            return h

        def step(s, hs):
            hf, hb = hs
            hf = two_steps(hf, af_scr, bf_scr, yf_scr, 2 * s, 2 * s + 1)
            hb = two_steps(hb, ab_scr, bb_scr, yb_scr, ch - 1 - 2 * s, ch - 2 - 2 * s)
            return hf, hb

        carry = lax.fori_loop(0, ch // 2, step, carry, unroll=16)

        for bi in range(b):
            blk = pl.ds(bi, ch, stride=b)
            rf = slab_rows(kind, bi, f_off)
            rb = slab_rows(kind, bi, b_off)
            if mode == "same":
                emit(rf, yf_scr[blk, :] + yb_scr[blk, :])
            elif mode == "first":
                acc_scr[rf, :] = yf_scr[blk, :]
                acc_scr[rb, :] = yb_scr[blk, :]
            else:
                emit(rf, acc_scr[rf, :] + yf_scr[blk, :])
                emit(rb, acc_scr[rb, :] + yb_scr[blk, :])
        return carry

    n_lat_chunks = t // ch
    half = n_lat_chunks // 2
    zero = jnp.zeros((b, LANES), F32)
    carry = pair(ctx, 0, 0, (zero, zero), "same")
    carry = lax.fori_loop(
        0, half,
        lambda q, hs: pair(lat, q * ch, (n_lat_chunks - 1 - q) * ch, hs, "first"), carry)
    lax.fori_loop(
        0, half,
        lambda q, hs: pair(lat, (half + q) * ch, (half - 1 - q) * ch, hs, "second"), carry)


def _lru(rest, conv_w, conv_b, w_bd, b_r, b_i, lam, *, cfg):
    rows = rest.shape[0]
    d_lru = conv_w.shape[1]
    ux_blk = cfg["ux_col"] // LANES
    ug_blk = cfg["ug_col"] // LANES
    nb = cfg["b"]
    assert cfg["tc"] == LRU_CHUNK and (cfg["t"] // LRU_CHUNK) % 2 == 0
    kern = functools.partial(_lru_kernel, b=nb, t=cfg["t"], tc=cfg["tc"])
    vec = lambda n: pl.BlockSpec((n, LANES), lambda c: (0, c))
    return pl.pallas_call(
        kern,
        grid=(d_lru // LANES,),
        in_specs=[
            pl.BlockSpec((rows, LANES), lambda c: (0, ux_blk + c)),
            pl.BlockSpec((rows, LANES), lambda c: (0, ug_blk + c), pipeline_mode=pl.Buffered(1)),
            vec(conv_w.shape[0]), vec(1),
            pl.BlockSpec((None, LANES, 4 * LANES), lambda c: (c, 0, 0)),
            vec(2), vec(2), vec(2),
        ],
        out_specs=pl.BlockSpec((rows, LANES), lambda c: (0, c)),
        out_shape=jax.ShapeDtypeStruct((rows, d_lru), BF16),
        scratch_shapes=[
            pltpu.VMEM((rows + (2 * nb + 1) * LRU_GAP, LANES), F32),
            pltpu.VMEM((rows, LANES), F32),
        ] + [pltpu.VMEM((nb * LRU_CHUNK, LANES), F32)] * 6,
        compiler_params=_params("parallel"),
        name="rglru",
    )(rest, rest, conv_w, conv_b.reshape(1, d_lru), w_bd, b_r, b_i, lam)


def _lru_block_diag(w_r, w_i):
    _, nb, bw, _ = w_r.shape
    per = LANES // bw
    eye = jnp.eye(per, dtype=w_r.dtype)

    def bd(w):
        w = w.reshape(nb // per, per, bw, bw)
        return jnp.einsum("cipq,ij->cipjq", w, eye).reshape(nb // per, LANES, LANES)

    return jnp.concatenate([bd(w_r[0]), bd(w_i[0]), bd(w_r[1]), bd(w_i[1])], axis=-1).astype(BF16)


def _fourier_kernel(u_ref, cst_ref, rev_ref, w2_ref, o_ref, p_scr, q_scr, *, t, n_groups, scale):
    half = t // 2
    for g in range(n_groups):
        cols = slice(g * FOURIER_GROUP, (g + 1) * FOURIER_GROUP)
        pq = _dot(u_ref[:, cols], w2_ref[...])
        p_scr[:, cols] = pq[:, :FOURIER_GROUP].astype(BF16)
        q_scr[:, cols] = pq[:, FOURIER_GROUP:].astype(BF16)
    a = _dot(cst_ref[:, :t], p_scr[...])
    bs = _dot(cst_ref[:, t:], q_scr[...])
    o_ref[0:half, :] = ((a - bs) * scale).astype(BF16)
    mirrored = _dot(rev_ref[...], ((a + bs) * scale).astype(BF16))
    p = p_scr[...].astype(F32)
    odd_row = (lax.broadcasted_iota(jnp.int32, p.shape, 0) & 1) == 1
    mid = jnp.sum(jnp.where(odd_row, -p, p), axis=0, keepdims=True) * scale
    first_row = lax.broadcasted_iota(jnp.int32, mirrored.shape, 0) == 0
    o_ref[half:t, :] = jnp.where(first_row, mid, mirrored).astype(BF16)


def _dft_tables(t):
    half = t // 2
    k = jnp.arange(half, dtype=jnp.int32)[:, None]
    s = jnp.arange(t, dtype=jnp.int32)[None, :]
    at = ((k * s) % t).astype(F32) * (2.0 * math.pi / t)
    cst = jnp.concatenate([jnp.cos(at), jnp.sin(at)], axis=1).astype(BF16)
    r = jnp.arange(half, dtype=jnp.int32)
    rev = (r[:, None] + r[None, :] == half).astype(BF16)
    c = jnp.arange(FOURIER_GROUP, dtype=jnp.int32)
    ac = ((c[:, None] * c[None, :]) % FOURIER_GROUP).astype(F32) * (2.0 * math.pi / FOURIER_GROUP)
    w2 = jnp.concatenate([jnp.cos(ac), jnp.sin(ac)], axis=1).astype(BF16)
    return cst, rev, w2


def _fourier(rest, cst, rev, w2, *, cfg, seq_len, row_blk0):
    d_f = cfg["d_branch"]
    wcols = min(d_f, 512)
    uf_blk = cfg["uf_col"] // wcols
    kern = functools.partial(_fourier_kernel, t=seq_len, n_groups=wcols // FOURIER_GROUP,
                             scale=(seq_len * FOURIER_GROUP) ** -0.5)
    const = lambda shape: pl.BlockSpec(shape, lambda bi, hf: (0, 0), pipeline_mode=pl.Buffered(1))
    return pl.pallas_call(
        kern,
        grid=(cfg["b"], d_f // wcols),
        in_specs=[
            pl.BlockSpec((seq_len, wcols), lambda bi, hf: (row_blk0 + bi, uf_blk + hf)),
            const(cst.shape), const(rev.shape), const(w2.shape),
        ],
        out_specs=pl.BlockSpec((seq_len, wcols), lambda bi, hf: (bi, hf)),
        out_shape=jax.ShapeDtypeStruct((cfg["b"] * seq_len, d_f), BF16),
        scratch_shapes=[pltpu.VMEM((seq_len, wcols), BF16)] * 2,
        compiler_params=_params("parallel", "parallel"),
        name="fourier",
    )(rest, cst, rev, w2)


def _merge_kernel(*refs, n_parts, n_lat_tiles):
    h_ref = refs[0]
    att_refs = refs[1:1 + n_parts]
    rec_ref = refs[1 + n_parts]
    fou_refs = refs[2 + n_parts:2 + 2 * n_parts]
    wgs, bgs, wbs = (refs[2 + 2 * n_parts + 3 * k:5 + 2 * n_parts + 3 * k] for k in range(3))
    o_ref = refs[-1]
    is_latent = pl.program_id(0) < n_lat_tiles

    def rows_of(parts):
        if len(parts) == 1:
            return parts[0][...]
        return jnp.where(is_latent, parts[0][...], parts[1][...])

    h = h_ref[...]
    acc = None
    for y, wg, bg, wb in zip((rows_of(att_refs), rec_ref[...], rows_of(fou_refs)), wgs, bgs, wbs):
        gate = jax.nn.sigmoid(_dot(h, wg[...]) + bg[...])
        term = gate * _dot(y, wb[...])
        acc = term if acc is None else acc + term
    o_ref[...] = acc.astype(BF16)


def _merge(h, att_parts, rec, fou_parts, w_gate, b_gate, w_branch, l, *, rows):
    d = h.shape[1]
    d_b = rec.shape[1]
    tm, tn = ROW_TILE, 512
    nj = d // tn
    n_parts = len(att_parts)
    n_lat_tiles = att_parts[0].shape[0] // tm
    y_spec = pl.BlockSpec((tm, d_b), lambda i, j: (i, 0))
    part_specs = [pl.BlockSpec((tm, d_b), lambda i, j: (jnp.minimum(i, n_lat_tiles - 1), 0)),
                  pl.BlockSpec((tm, d_b), lambda i, j: (jnp.maximum(i - n_lat_tiles, 0), 0))
                  ][:n_parts]
    wg_spec = lambda k: pl.BlockSpec((None, d, tn), lambda i, j: (l, 0, k * nj + j))
    bg_spec = lambda k: pl.BlockSpec((None, 1, tn), lambda i, j: (l, 0, k * nj + j))
    wb_spec = lambda k: pl.BlockSpec((None, None, d_b, tn), lambda i, j: (l, k, 0, j))
    bg = b_gate.reshape(b_gate.shape[0], 1, -1)
    return pl.pallas_call(
        functools.partial(_merge_kernel, n_parts=n_parts, n_lat_tiles=n_lat_tiles),
        grid=(rows // tm, nj),
        in_specs=[pl.BlockSpec((tm, d), lambda i, j: (i, 0))] + part_specs + [y_spec] + part_specs
                 + [wg_spec(0), wg_spec(1), wg_spec(2), bg_spec(0), bg_spec(1), bg_spec(2),
                    wb_spec(0), wb_spec(1), wb_spec(2)],
        out_specs=pl.BlockSpec((tm, tn), lambda i, j: (i, j)),
        out_shape=jax.ShapeDtypeStruct((rows, d), BF16),
        compiler_params=_params("parallel", "arbitrary"),
        name="gated_merge",
    )(h, *att_parts, rec, *fou_parts, w_gate, w_gate, w_gate, bg, bg, bg,
      w_branch, w_branch, w_branch)


def _out_proj_kernel(*refs, n_lat_tiles):
    x_refs, (m_ref, w_ref, mod_ref, g_ref, o_ref, h_ref) = refs[:-6], refs[-6:]
    delta = mod_ref[2:3, :] * _dot(m_ref[...], w_ref[...])

    def emit(r):
        y = r[...] + delta
        o_ref[...] = y
        h_ref[...] = _ada_norm(y, g_ref[...], mod_ref[3:4, :], mod_ref[4:5, :])

    if len(x_refs) == 2:
        _for_row_source(*x_refs, n_lat_tiles, emit)
    else:
        emit(x_refs[0])


def _out_proj(x_parts, m, w_out, l, mod_l, g_norm2, *, cfg, rows):
    d = m.shape[1]
    tm = ROW_TILE
    row_spec = pl.BlockSpec((tm, d), lambda i: (i, 0))
    n_lat_tiles = x_parts[0].shape[0] // tm
    x_specs = _split_row_specs(n_lat_tiles, tm, d) if len(x_parts) == 2 else [row_spec]
    return pl.pallas_call(
        functools.partial(_out_proj_kernel, n_lat_tiles=n_lat_tiles),
        grid=(rows // tm,),
        in_specs=x_specs + [
            row_spec,
            pl.BlockSpec((None, d, d), lambda i: (l, 0, 0), pipeline_mode=pl.Buffered(1)),
            pl.BlockSpec((None, 6, d), lambda i: (cfg["mod_row"](i), 0, 0)),
            pl.BlockSpec((1, d), lambda i: (0, 0)),
        ],
        out_specs=[row_spec, row_spec],
        out_shape=[jax.ShapeDtypeStruct((rows, d), F32), jax.ShapeDtypeStruct((rows, d), BF16)],
        compiler_params=_params("parallel"),
        name="out_proj",
    )(*x_parts, m, w_out, mod_l, g_norm2.reshape(1, d))


def _ffn_up_kernel(h_ref, wg_ref, wu_ref, *refs):
    n_cast = (len(refs) - 1) // 2
    a_ref = refs[n_cast]
    h = h_ref[...]
    gt = _dot(h, wg_ref[...])
    up = _dot(h, wu_ref[...])
    a_ref[...] = (gt * jax.nn.sigmoid(gt) * up).astype(BF16)
    for src, dst in zip(refs[:n_cast], refs[n_cast + 1:]):
        if len(dst.shape) == 2:
            dst[...] = src[...].astype(BF16)
        else:
            for g in range(dst.shape[0]):
                dst[g] = src[:, g * dst.shape[2]:(g + 1) * dst.shape[2]].astype(BF16)


def _col_blocked(w, n_groups):
    l, r, c = w.shape
    return w.reshape(l, r, n_groups, c // n_groups).transpose(0, 2, 1, 3)


def _ffn_up(h2, w_ffn_in, l, *, cfg, rows, cast_srcs=(), cast_layer=None):
    d = h2.shape[1]
    d_ff = w_ffn_in.shape[2] // 2
    tm, tf = FFN_UP_ROW_TILE, cfg["tf"]
    nk = d_ff // tf
    n_steps = (rows // tm) * nk
    cast_in, cast_out, cast_shape = [], [], []
    for w, groups in cast_srcs:
        _, r, c = w.shape
        br = next(x for x in (16, 32, 64, 128, 256) if r % x == 0 and r // x <= n_steps)
        blk = lambda i, k, n=r // br: jnp.minimum(i * nk + k, n - 1)
        cast_in.append(pl.BlockSpec((None, br, c), lambda i, k, blk=blk: (cast_layer, blk(i, k), 0)))
        if groups == 1:
            cast_out.append(pl.BlockSpec((None, br, c), lambda i, k, blk=blk: (0, blk(i, k), 0)))
            cast_shape.append(jax.ShapeDtypeStruct((1, r, c), BF16))
        else:
            cast_out.append(pl.BlockSpec((None, groups, br, c // groups),
                                         lambda i, k, blk=blk: (0, 0, blk(i, k), 0)))
            cast_shape.append(jax.ShapeDtypeStruct((1, groups, r, c // groups), BF16))
    return pl.pallas_call(
        _ffn_up_kernel,
        grid=(rows // tm, nk),
        in_specs=[
            pl.BlockSpec((tm, d), lambda i, k: (i, 0)),
            pl.BlockSpec((None, d, tf), lambda i, k: (l, 0, k)),
            pl.BlockSpec((None, d, tf), lambda i, k: (l, 0, nk + k)),
        ] + cast_in,
        out_specs=[pl.BlockSpec((tm, tf), lambda i, k: (i, k))] + cast_out,
        out_shape=[jax.ShapeDtypeStruct((rows, d_ff), BF16)] + cast_shape,
        compiler_params=_params("arbitrary", "arbitrary"),
        name="ffn_up",
    )(h2, w_ffn_in, w_ffn_in, *[w for w, _ in cast_srcs])


def _ffn_down_kernel(a_ref, x_ref, wo_ref, mod_ref, nmod_ref, g_ref, *refs, nj, d, final):
    out_refs, y_scr = refs[:-1], refs[-1]
    i, j = pl.program_id(0), pl.program_id(1)
    cur = i % 2

    @pl.when((i == 0) & (j == 0))
    def _():
        y_scr[1] = jnp.zeros(y_scr.shape[1:], F32)

    prv = y_scr.at[1 - cur]
    ssq = sum(jnp.sum(prv[jj] * prv[jj], axis=-1, keepdims=True) for jj in range(nj))
    normed = prv[j] * lax.rsqrt(ssq * (1.0 / d) + NORM_EPS) * g_ref[...]
    if final:
        out_refs[0][...] = normed
    else:
        out_refs[1][...] = (normed * (1.0 + nmod_ref[1:2, :]) + nmod_ref[0:1, :]).astype(BF16)

    jw = jnp.where(i == pl.num_programs(0) - 1, nj - 1, j)
    y = x_ref[...] + mod_ref[5:6, :] * _dot(a_ref[...], wo_ref[jw])
    if not final:
        out_refs[0][...] = y
    y_scr[cur, j] = y


def _ffn_down(a, xmid, w_ffn_out, l, mod_l, next_mod, gain, *, cfg, rows, final):
    d = xmid.shape[1]
    d_ff = a.shape[1]
    tm = ROW_TILE
    _, nj, _, tn = w_ffn_out.shape
    n_tiles = rows // tm
    cur = lambda i: jnp.minimum(i, n_tiles - 1)
    prev = lambda i: jnp.maximum(i - 1, 0)
    cur_j = lambda i, j: jnp.where(i == n_tiles, nj - 1, j)
    prev_j = lambda i, j: jnp.where(i == 0, 0, j)
    cur_spec = pl.BlockSpec((tm, tn), lambda i, j: (cur(i), cur_j(i, j)))
    prev_spec = pl.BlockSpec((tm, tn), lambda i, j: (prev(i), prev_j(i, j)))
    if final:
        out_specs, out_shape = [prev_spec], [jax.ShapeDtypeStruct((rows, d), F32)]
    else:
        out_specs = [cur_spec, prev_spec]
        out_shape = [jax.ShapeDtypeStruct((rows, d), F32), jax.ShapeDtypeStruct((rows, d), BF16)]
    return pl.pallas_call(
        functools.partial(_ffn_down_kernel, nj=nj, d=d, final=final),
        grid=(n_tiles + 1, nj),
        in_specs=[
            pl.BlockSpec((tm, d_ff), lambda i, j: (cur(i), 0)),
            cur_spec,
            pl.BlockSpec((None, nj, d_ff, tn), lambda i, j: (l, 0, 0, 0),
                         pipeline_mode=pl.Buffered(1)),
            pl.BlockSpec((None, 6, tn), lambda i, j: (cfg["mod_row"](cur(i)), 0, cur_j(i, j))),
            pl.BlockSpec((None, 6, tn), lambda i, j: (cfg["mod_row"](prev(i)), 0, j)),
            pl.BlockSpec((1, tn), lambda i, j: (0, j)),
        ],
        out_specs=out_specs,
        out_shape=out_shape,
        scratch_shapes=[pltpu.VMEM((2, nj, tm, tn), F32)],
        compiler_params=_params("arbitrary", "arbitrary"),
        name="ffn_down",
    )(a, xmid, w_ffn_out, mod_l, next_mod, gain.reshape(1, d))


def _rope_tables(t, tile_rows):
    pairs = HEAD_DIM // 4
    rows = t // GRID_W
    row = jnp.repeat(jnp.arange(rows, dtype=F32), GRID_W)
    col = jnp.tile(jnp.arange(GRID_W, dtype=F32), rows)
    inv = ROPE_THETA ** (-jnp.arange(pairs, dtype=F32) / pairs)
    ang = jnp.concatenate([row[:, None] * inv, col[:, None] * inv], axis=-1)
    cos, sin = jnp.cos(ang), jnp.sin(ang)
    cosf = jnp.concatenate([cos, cos], axis=-1)
    sinf = jnp.concatenate([-sin, sin], axis=-1)
    cosf = jnp.concatenate([cosf, jnp.ones((tile_rows, HEAD_DIM), F32)], axis=0)
    sinf = jnp.concatenate([sinf, jnp.zeros((tile_rows, HEAD_DIM), F32)], axis=0)
    return cosf, sinf


def kernel(x, c, ctx, c_ctx, w_mod, b_mod, g_norm1, g_norm2, w_in, q_gain, k_gain, conv_w, conv_b, lru_w_r, lru_b_r, lru_w_i, lru_b_i, lru_lambda, w_branch, w_gate, b_gate, w_out, w_ffn_in, w_ffn_out, g_final):
    b, t, d = x.shape
    tc = ctx.shape[1]
    depth = w_in.shape[0]
    d_branch = w_branch.shape[2]
    n_q_heads = d_branch // HEAD_DIM
    n_kv_heads = n_q_heads // Q_PER_KV
    d_kv = n_kv_heads * HEAD_DIM
    d_ff = w_ffn_out.shape[1]
    n_lat, n_ctx = b * t, b * tc
    assert b == SUBLANES and t % IN_PROJ_ROW_TILE == 0 and n_ctx % FFN_UP_ROW_TILE == 0
    assert tc % LRU_CHUNK == 0 and t % LRU_CHUNK == 0
    tiles_per_batch = t // ROW_TILE

    cfg = dict(
        b=b, t=t, tc=tc, d_branch=d_branch, n_q_heads=n_q_heads, n_kv_heads=n_kv_heads,
        tn_in=d_branch + 2 * d_kv,
        ux_col=0, ug_col=d_branch, uf_col=2 * d_branch,
        tf=512 if d_ff % 512 == 0 else 256,
        mod_row=lambda i: jnp.minimum(i // tiles_per_batch, b),
        rope_block=lambda i: jnp.where(i < n_lat // IN_PROJ_ROW_TILE, i % (t // IN_PROJ_ROW_TILE),
                                       t // IN_PROJ_ROW_TILE),
    )

    x_parts = (x.reshape(n_lat, d), ctx.reshape(n_ctx, d))
    cc = jnp.concatenate([c, c_ctx[None, :], jnp.zeros((2 * SUBLANES - b - 1, d), F32)], axis=0)
    mod = _modulation(cc, w_mod, b_mod).reshape(depth, 2 * SUBLANES, 6, d)

    w_f32 = ((w_in, 1), (w_gate, 1), (w_branch.reshape(depth, -1, d), 1), (w_out, 1),
             (w_ffn_in, 1), (w_ffn_out, FFN_DOWN_COL_BLOCKS))
    wb = [(w[0:1] if g == 1 else _col_blocked(w[0:1], g)).astype(BF16) for w, g in w_f32]

    cosf, sinf = _rope_tables(t, IN_PROJ_ROW_TILE)
    dft_lat = _dft_tables(t)
    dft_ctx = _dft_tables(tc)

    h1 = _prenorm(*x_parts, mod[0], g_norm1[0], cfg=cfg)
    out = None
    for l in range(depth):
        last = l == depth - 1
        rows = n_lat if last else n_lat + n_ctx
        w_in_b, w_gate_b, w_branch_b, w_out_b, w_ffn_in_b, w_ffn_out_b = wb
        w_branch_b = w_branch_b.reshape((1,) + w_branch.shape[1:])
        qkv, rest = _in_proj(h1, w_in_b, 0, cosf, sinf, q_gain[l], k_gain[l], cfg=cfg)
        att = (_attention(qkv, cfg=cfg, latent=True),)
        rec = _lru(rest, conv_w[l], conv_b[l], _lru_block_diag(lru_w_r[l], lru_w_i[l]),
                   lru_b_r[l], lru_b_i[l], lru_lambda[l], cfg=cfg)
        fou = (_fourier(rest, *dft_lat, cfg=cfg, seq_len=t, row_blk0=0),)
        if not last:
            att += (_attention(qkv, cfg=cfg, latent=False),)
            fou += (_fourier(rest, *dft_ctx, cfg=cfg, seq_len=tc, row_blk0=n_lat // tc),)
        m = _merge(h1, att, rec, fou, w_gate_b, b_gate[l:l + 1], w_branch_b, 0, rows=rows)
        xmid, h2 = _out_proj(x_parts, m, w_out_b, 0, mod[l], g_norm2[l], cfg=cfg, rows=rows)
        if last:
            (a,) = _ffn_up(h2, w_ffn_in_b, 0, cfg=cfg, rows=rows)
            (out,) = _ffn_down(a, xmid, w_ffn_out_b, 0, mod[l], mod[l], g_final,
                               cfg=cfg, rows=rows, final=True)
        else:
            a, *wb = _ffn_up(h2, w_ffn_in_b, 0, cfg=cfg, rows=rows, cast_srcs=w_f32,
                             cast_layer=l + 1)
            xall, h1 = _ffn_down(a, xmid, w_ffn_out_b, 0, mod[l], mod[l + 1], g_norm1[l + 1],
                                 cfg=cfg, rows=rows, final=False)
            x_parts = (xall,)
    return out.reshape(b, t, d)
```

```python
import functools
import math

import jax
import jax.numpy as jnp
from jax import lax
from jax.experimental import pallas as pl
from jax.experimental.pallas import tpu as pltpu

F32 = jnp.float32
BF16 = jnp.bfloat16

HEAD_DIM = 128
Q_PER_KV = 4
GRID_W = 64
ROPE_THETA = 10000.0
NORM_EPS = 1e-6
LRU_C = 8.0
FOURIER_GROUP = 128

LANES = 128
SUBLANES = 8
VMEM_LIMIT_BYTES = 56 * 1024 * 1024

ROW_TILE = 512
IN_PROJ_ROW_TILE = 1024
FFN_UP_ROW_TILE = 1024
FFN_DOWN_COL_BLOCKS = 4
ATTN_Q_TILE = 2048
ATTN_SUB_ROWS = 512
LRU_CHUNK = 256
LRU_GAP = SUBLANES


def _params(*semantics):
    return pltpu.CompilerParams(dimension_semantics=semantics,
                                vmem_limit_bytes=VMEM_LIMIT_BYTES)


def _dot(a, b):
    return jnp.dot(a, b, preferred_element_type=F32)


def _dot_nt(a, b):
    return lax.dot_general(a, b, (((1,), (1,)), ((), ())), preferred_element_type=F32)


def _rms(x):
    return x * lax.rsqrt(jnp.mean(x * x, axis=-1, keepdims=True) + NORM_EPS)


def _ada_norm(x, gain, shift, scale):
    return (_rms(x) * gain * (1.0 + scale) + shift).astype(BF16)


def _mod_kernel(c_ref, w_ref, b_ref, o_ref):
    c = c_ref[...]
    s = (c * jax.nn.sigmoid(c)).astype(BF16)
    o_ref[...] = _dot(s, w_ref[...].astype(BF16)) + b_ref[...]


def _modulation(cc, w_mod, b_mod):
    depth, d, n = w_mod.shape
    rows = cc.shape[0]
    tn = 1024
    return pl.pallas_call(
        _mod_kernel,
        grid=(depth, n // tn),
        in_specs=[
            pl.BlockSpec((rows, d), lambda l, j: (0, 0)),
            pl.BlockSpec((None, d, tn), lambda l, j: (l, 0, j)),
            pl.BlockSpec((None, 1, tn), lambda l, j: (l, 0, j)),
        ],
        out_specs=pl.BlockSpec((None, rows, tn), lambda l, j: (l, 0, j)),
        out_shape=jax.ShapeDtypeStruct((depth, rows, n), F32),
        compiler_params=_params("parallel", "parallel"),
        name="modulation",
    )(cc, w_mod, b_mod.reshape(depth, 1, n))


def _split_row_specs(n_lat_tiles, tm, d):
    return [pl.BlockSpec((tm, d), lambda i: (jnp.minimum(i, n_lat_tiles - 1), 0)),
            pl.BlockSpec((tm, d), lambda i: (jnp.maximum(i - n_lat_tiles, 0), 0))]


def _for_row_source(x_ref, c_ref, n_lat_tiles, fn):
    i = pl.program_id(0)
    pl.when(i < n_lat_tiles)(lambda: fn(x_ref))
    pl.when(i >= n_lat_tiles)(lambda: fn(c_ref))


def _prenorm_kernel(x_ref, c_ref, mod_ref, g_ref, h_ref, *, n_lat_tiles):
    def emit(r):
        h_ref[...] = _ada_norm(r[...], g_ref[...], mod_ref[0:1, :], mod_ref[1:2, :])
    _for_row_source(x_ref, c_ref, n_lat_tiles, emit)


def _prenorm(x2d, ctx2d, mod_l, g_norm, *, cfg):
    d = x2d.shape[1]
    rows = x2d.shape[0] + ctx2d.shape[0]
    tm = ROW_TILE
    n_lat_tiles = x2d.shape[0] // tm
    return pl.pallas_call(
        functools.partial(_prenorm_kernel, n_lat_tiles=n_lat_tiles),
        grid=(rows // tm,),
        in_specs=_split_row_specs(n_lat_tiles, tm, d) + [
            pl.BlockSpec((None, 6, d), lambda i: (cfg["mod_row"](i), 0, 0)),
            pl.BlockSpec((1, d), lambda i: (0, 0)),
        ],
        out_specs=pl.BlockSpec((tm, d), lambda i: (i, 0)),
        out_shape=jax.ShapeDtypeStruct((rows, d), BF16),
        compiler_params=_params("parallel"),
        name="prenorm",
    )(x2d, ctx2d, mod_l, g_norm.reshape(1, d))


def _in_proj_kernel(h_ref, w_ref, cos_ref, sin_ref, qg_ref, kg_ref, qkv_ref, rest_ref, qkv_scr,
                    *, n_q_heads, n_kv_heads):
    j = pl.program_id(1)

    @pl.when(j == 0)
    def _():
        qkv_scr[...] = _dot(h_ref[...], w_ref[...])

    @pl.when(j == 1)
    def _():
        rest_ref[...] = _dot(h_ref[...], w_ref[...]).astype(BF16)
        cos = cos_ref[...]
        sin = sin_ref[...]
        q_gain = qg_ref[...] * (HEAD_DIM ** -0.5 * math.log2(math.e))
        k_gain = kg_ref[...]
        n_rot = n_q_heads + n_kv_heads
        for hh in range(n_rot):
            cols = slice(hh * HEAD_DIM, (hh + 1) * HEAD_DIM)
            y = _rms(qkv_scr[:, cols]) * (q_gain if hh < n_q_heads else k_gain)
            y = y * cos + pltpu.roll(y, HEAD_DIM // 2, axis=1) * sin
            qkv_ref[:, cols] = y.astype(BF16)
        qkv_ref[:, n_rot * HEAD_DIM:] = qkv_scr[:, n_rot * HEAD_DIM:].astype(BF16)

    @pl.when(j > 1)
    def _():
        rest_ref[...] = _dot(h_ref[...], w_ref[...]).astype(BF16)


def _in_proj(h1, w_in, l, cosf, sinf, q_gain, k_gain, *, cfg):
    rows, d = h1.shape
    d_in = w_in.shape[2]
    tm, tn = IN_PROJ_ROW_TILE, cfg["tn_in"]
    assert d_in == 3 * tn
    kern = functools.partial(_in_proj_kernel, n_q_heads=cfg["n_q_heads"],
                             n_kv_heads=cfg["n_kv_heads"])
    return pl.pallas_call(
        kern,
        grid=(rows // tm, d_in // tn),
        in_specs=[
            pl.BlockSpec((tm, d), lambda i, j: (i, 0)),
            pl.BlockSpec((None, d, tn), lambda i, j: (l, 0, j)),
            pl.BlockSpec((tm, HEAD_DIM), lambda i, j: (cfg["rope_block"](i), 0)),
            pl.BlockSpec((tm, HEAD_DIM), lambda i, j: (cfg["rope_block"](i), 0)),
            pl.BlockSpec((1, HEAD_DIM), lambda i, j: (0, 0)),
            pl.BlockSpec((1, HEAD_DIM), lambda i, j: (0, 0)),
        ],
        out_specs=[
            pl.BlockSpec((tm, tn), lambda i, j: (i, 0)),
            pl.BlockSpec((tm, tn), lambda i, j: (i, jnp.maximum(j - 1, 0))),
        ],
        out_shape=[
            jax.ShapeDtypeStruct((rows, tn), BF16),
            jax.ShapeDtypeStruct((rows, d_in - tn), BF16),
        ],
        scratch_shapes=[pltpu.VMEM((tm, tn), F32)],
        compiler_params=_params("parallel", "arbitrary"),
        name="in_proj",
    )(h1, w_in, cosf, sinf, q_gain.reshape(1, HEAD_DIM), k_gain.reshape(1, HEAD_DIM))


def _attn_kernel(q_ref, *refs, latent):
    if latent:
        kl_ref, kc_ref, vl_ref, vc_ref, o_ref, vl_aug, vc_aug = refs
    else:
        kc_ref, vc_ref, o_ref, vc_aug = refs

    @pl.when(pl.program_id(2) == 0)
    def _():
        for v_ref, aug in ((vl_ref, vl_aug), (vc_ref, vc_aug)) if latent else ((vc_ref, vc_aug),):
            lane = lax.broadcasted_iota(jnp.int32, v_ref.shape, 1)
            aug[:, :HEAD_DIM] = v_ref[...]
            aug[:, HEAD_DIM:] = jnp.where(lane == 0, 1.0, 0.0).astype(BF16)

    kc = kc_ref[...]
    sub = min(q_ref.shape[0], ATTN_SUB_ROWS)
    for r0 in range(0, q_ref.shape[0], sub):
        for g in range(Q_PER_KV):
            cols = slice(g * HEAD_DIM, (g + 1) * HEAD_DIM)
            q = q_ref[r0:r0 + sub, cols]
            sc = _dot_nt(q, kc)
            m = jnp.max(sc, axis=-1, keepdims=True)
            if latent:
                sl = _dot_nt(q, kl_ref[...])
                m = jnp.maximum(m, jnp.max(sl, axis=-1, keepdims=True))
            o = _dot(jnp.exp2((sc - m).astype(BF16)), vc_aug[...])
            if latent:
                o = o + _dot(jnp.exp2((sl - m).astype(BF16)), vl_aug[...])
            o_ref[r0:r0 + sub, cols] = (
                o[:, :HEAD_DIM] / o[:, HEAD_DIM:HEAD_DIM + 1]).astype(BF16)


def _attention(qkv, *, cfg, latent):
    b, t, tc = cfg["b"], cfg["t"], cfg["tc"]
    tq = min(ATTN_Q_TILE, t) if latent else tc
    assert t % tq == 0
    n_q_tiles = t // tq if latent else 1
    q_blk0 = 0 if latent else b * t // tc
    n_kv = cfg["n_kv_heads"]
    gw = Q_PER_KV * HEAD_DIM
    k_col = cfg["n_q_heads"]
    v_col = k_col + n_kv
    ctx_blk0 = b * t // tc

    q_spec = pl.BlockSpec((tq, gw), lambda bi, h, qi: (q_blk0 + bi * n_q_tiles + qi, h))
    lat_spec = lambda col: pl.BlockSpec((t, HEAD_DIM), lambda bi, h, qi: (bi, col + h))
    ctx_spec = lambda col: pl.BlockSpec((tc, HEAD_DIM), lambda bi, h, qi: (ctx_blk0 + bi, col + h))
    aug = lambda n: pltpu.VMEM((n, 2 * HEAD_DIM), BF16)
    if latent:
        in_specs = [q_spec, lat_spec(k_col), ctx_spec(k_col), lat_spec(v_col), ctx_spec(v_col)]
        scratch = [aug(t), aug(tc)]
    else:
        in_specs = [q_spec, ctx_spec(k_col), ctx_spec(v_col)]
        scratch = [aug(tc)]
    return pl.pallas_call(
        functools.partial(_attn_kernel, latent=latent),
        grid=(b, n_kv, n_q_tiles),
        in_specs=in_specs,
        out_specs=pl.BlockSpec((tq, gw), lambda bi, h, qi: (bi * n_q_tiles + qi, h)),
        out_shape=jax.ShapeDtypeStruct((b * (t if latent else tc), cfg["n_q_heads"] * HEAD_DIM),
                                       BF16),
        scratch_shapes=scratch,
        compiler_params=_params("parallel", "parallel", "arbitrary"),
        name="attention",
    )(*[qkv] * len(in_specs))


def _lru_kernel(ux_ref, ug_ref, cw_ref, cb_ref, w_ref, br_ref, bi_ref, lam_ref, o_ref,
                u_scr, acc_scr, af_scr, bf_scr, yf_scr, ab_scr, bb_scr, yb_scr, *, b, t, tc):
    ch, gap = LRU_CHUNK, LRU_GAP
    n_lat = b * t
    conv_width = cw_ref.shape[0]
    conv_left = conv_width // 2

    lat = (gap, 0, t + gap, t)
    ctx = (gap + b * (t + gap), n_lat, tc + gap, tc)

    zeros_gap = jnp.zeros((gap, LANES), F32)
    for kind, n in ((lat, t), (ctx, tc)):
        for bi in range(b):
            u0 = kind[0] + bi * kind[2]
            r0 = kind[1] + bi * kind[3]
            u_scr[u0 - gap:u0, :] = zeros_gap
            u_scr[u0:u0 + n, :] = ux_ref[r0:r0 + n, :].astype(F32)
    u_end = ctx[0] + b * ctx[2] - gap
    u_scr[u_end:u_end + gap, :] = zeros_gap

    def slab_rows(kind, bi, off):
        start = kind[1] + bi * kind[3] + off
        if not isinstance(start, int):
            start = pl.multiple_of(start, ch)
        return pl.ds(start, ch)

    def coefficients(d, kind, off, a_scr, b_scr):
        lam = lam_ref[d:d + 1, :]
        softplus_neg_lam = jnp.maximum(-lam, 0.0) + jnp.log1p(jnp.exp(-jnp.abs(lam)))
        half_rate = -0.5 * LRU_C * softplus_neg_lam
        w_half = w_ref[:, d * 2 * LANES:(d + 1) * 2 * LANES] * 0.5
        br_half = 0.5 * br_ref[d:d + 1, :]
        bi_half = 0.5 * bi_ref[d:d + 1, :]
        for bi in range(b):
            s0 = kind[0] + bi * kind[2] + off - conv_left
            u = cb_ref[...]
            for j in range(conv_width):
                u = u + cw_ref[j:j + 1, :] * u_scr[pl.ds(s0 + j, ch), :]
            gates = _dot(u.astype(BF16), w_half)
            log_a = jnp.tanh(gates[:, :LANES] + br_half) * half_rate + half_rate
            i_gate = 0.5 * jnp.tanh(gates[:, LANES:] + bi_half) + 0.5
            a = jnp.exp(log_a)
            one_minus_a2 = -jnp.tanh(log_a) * (a * a + 1.0)
            root = jnp.where(one_minus_a2 > 0.0, one_minus_a2 * lax.rsqrt(one_minus_a2), 0.0)
            a_scr[pl.ds(bi, ch, stride=b), :] = a
            b_scr[pl.ds(bi, ch, stride=b), :] = root * i_gate * u

    def emit(rows, y):
        gate = jax.nn.gelu(ug_ref[rows, :].astype(F32), approximate=True)
        o_ref[rows, :] = (y * gate).astype(BF16)

    def pair(kind, f_off, b_off, carry, mode):
        coefficients(0, kind, f_off, af_scr, bf_scr)
        coefficients(1, kind, b_off, ab_scr, bb_scr)

        def two_steps(h, a_scr, b_scr, y_scr, s0, s1):
            r0 = pl.ds(pl.multiple_of(s0 * b, b), b)
            r1 = pl.ds(pl.multiple_of(s1 * b, b), b)
            a0, b0, a1, b1 = a_scr[r0, :], b_scr[r0, :], a_scr[r1, :], b_scr[r1, :]
            y_scr[r0, :] = a0 * h + b0
            h = (a1 * a0) * h + (a1 * b0 + b1)
            y_scr[r1, :] = h
            return h

        def step(s, hs):
            hf, hb = hs
            hf = two_steps(hf, af_scr, bf_scr, yf_scr, 2 * s, 2 * s + 1)
            hb = two_steps(hb, ab_scr, bb_scr, yb_scr, ch - 1 - 2 * s, ch - 2 - 2 * s)
            return hf, hb

        carry = lax.fori_loop(0, ch // 2, step, carry, unroll=16)

        for bi in range(b):
            blk = pl.ds(bi, ch, stride=b)
            rf = slab_rows(kind, bi, f_off)
            rb = slab_rows(kind, bi, b_off)
            if mode == "same":
                emit(rf, yf_scr[blk, :] + yb_scr[blk, :])
            elif mode == "first":
                acc_scr[rf, :] = yf_scr[blk, :]
                acc_scr[rb, :] = yb_scr[blk, :]
            else:
                emit(rf, acc_scr[rf, :] + yf_scr[blk, :])
                emit(rb, acc_scr[rb, :] + yb_scr[blk, :])
        return carry

    n_lat_chunks = t // ch
    half = n_lat_chunks // 2
    zero = jnp.zeros((b, LANES), F32)
    carry = pair(ctx, 0, 0, (zero, zero), "same")
    carry = lax.fori_loop(
        0, half,
        lambda q, hs: pair(lat, q * ch, (n_lat_chunks - 1 - q) * ch, hs, "first"), carry)
    lax.fori_loop(
        0, half,
        lambda q, hs: pair(lat, (half + q) * ch, (half - 1 - q) * ch, hs, "second"), carry)


def _lru(rest, conv_w, conv_b, w_bd, b_r, b_i, lam, *, cfg):
    rows = rest.shape[0]
    d_lru = conv_w.shape[1]
    ux_blk = cfg["ux_col"] // LANES
    ug_blk = cfg["ug_col"] // LANES
    nb = cfg["b"]
    assert cfg["tc"] == LRU_CHUNK and (cfg["t"] // LRU_CHUNK) % 2 == 0
    kern = functools.partial(_lru_kernel, b=nb, t=cfg["t"], tc=cfg["tc"])
    vec = lambda n: pl.BlockSpec((n, LANES), lambda c: (0, c))
    return pl.pallas_call(
        kern,
        grid=(d_lru // LANES,),
        in_specs=[
            pl.BlockSpec((rows, LANES), lambda c: (0, ux_blk + c), pipeline_mode=pl.Buffered(1)),
            pl.BlockSpec((rows, LANES), lambda c: (0, ug_blk + c), pipeline_mode=pl.Buffered(1)),
            vec(conv_w.shape[0]), vec(1),
            pl.BlockSpec((None, LANES, 4 * LANES), lambda c: (c, 0, 0)),
            vec(2), vec(2), vec(2),
        ],
        out_specs=pl.BlockSpec((rows, LANES), lambda c: (0, c)),
        out_shape=jax.ShapeDtypeStruct((rows, d_lru), BF16),
        scratch_shapes=[
            pltpu.VMEM((rows + (2 * nb + 1) * LRU_GAP, LANES), F32),
            pltpu.VMEM((rows, LANES), F32),
        ] + [pltpu.VMEM((nb * LRU_CHUNK, LANES), F32)] * 6,
        compiler_params=_params("parallel"),
        name="rglru",
    )(rest, rest, conv_w, conv_b.reshape(1, d_lru), w_bd, b_r, b_i, lam)


def _lru_block_diag(w_r, w_i):
    _, nb, bw, _ = w_r.shape
    per = LANES // bw
    eye = jnp.eye(per, dtype=w_r.dtype)

    def bd(w):
        w = w.reshape(nb // per, per, bw, bw)
        return jnp.einsum("cipq,ij->cipjq", w, eye).reshape(nb // per, LANES, LANES)

    return jnp.concatenate([bd(w_r[0]), bd(w_i[0]), bd(w_r[1]), bd(w_i[1])], axis=-1).astype(BF16)


def _fourier_kernel(u_ref, cst_ref, rev_ref, w2_ref, o_ref, p_scr, q_scr, *, t, n_groups, scale):
    half = t // 2
    for g in range(n_groups):
        cols = slice(g * FOURIER_GROUP, (g + 1) * FOURIER_GROUP)
        pq = _dot(u_ref[:, cols], w2_ref[...])
        p_scr[:, cols] = pq[:, :FOURIER_GROUP].astype(BF16)
        q_scr[:, cols] = pq[:, FOURIER_GROUP:].astype(BF16)
    a = _dot(cst_ref[:, :t], p_scr[...])
    bs = _dot(cst_ref[:, t:], q_scr[...])
    o_ref[0:half, :] = ((a - bs) * scale).astype(BF16)
    mirrored = _dot(rev_ref[...], ((a + bs) * scale).astype(BF16))
    p = p_scr[...].astype(F32)
    odd_row = (lax.broadcasted_iota(jnp.int32, p.shape, 0) & 1) == 1
    mid = jnp.sum(jnp.where(odd_row, -p, p), axis=0, keepdims=True) * scale
    first_row = lax.broadcasted_iota(jnp.int32, mirrored.shape, 0) == 0
    o_ref[half:t, :] = jnp.where(first_row, mid, mirrored).astype(BF16)


def _dft_tables(t):
    half = t // 2
    k = jnp.arange(half, dtype=jnp.int32)[:, None]
    s = jnp.arange(t, dtype=jnp.int32)[None, :]
    at = ((k * s) % t).astype(F32) * (2.0 * math.pi / t)
    cst = jnp.concatenate([jnp.cos(at), jnp.sin(at)], axis=1).astype(BF16)
    r = jnp.arange(half, dtype=jnp.int32)
    rev = (r[:, None] + r[None, :] == half).astype(BF16)
    c = jnp.arange(FOURIER_GROUP, dtype=jnp.int32)
    ac = ((c[:, None] * c[None, :]) % FOURIER_GROUP).astype(F32) * (2.0 * math.pi / FOURIER_GROUP)
    w2 = jnp.concatenate([jnp.cos(ac), jnp.sin(ac)], axis=1).astype(BF16)
    return cst, rev, w2


def _fourier(rest, cst, rev, w2, *, cfg, seq_len, row_blk0):
    d_f = cfg["d_branch"]
    wcols = min(d_f, 512)
    uf_blk = cfg["uf_col"] // wcols
    kern = functools.partial(_fourier_kernel, t=seq_len, n_groups=wcols // FOURIER_GROUP,
                             scale=(seq_len * FOURIER_GROUP) ** -0.5)
    const = lambda shape: pl.BlockSpec(shape, lambda bi, hf: (0, 0), pipeline_mode=pl.Buffered(1))
    return pl.pallas_call(
        kern,
        grid=(cfg["b"], d_f // wcols),
        in_specs=[
            pl.BlockSpec((seq_len, wcols), lambda bi, hf: (row_blk0 + bi, uf_blk + hf)),
            const(cst.shape), const(rev.shape), const(w2.shape),
        ],
        out_specs=pl.BlockSpec((seq_len, wcols), lambda bi, hf: (bi, hf)),
        out_shape=jax.ShapeDtypeStruct((cfg["b"] * seq_len, d_f), BF16),
        scratch_shapes=[pltpu.VMEM((seq_len, wcols), BF16)] * 2,
        compiler_params=_params("parallel", "parallel"),
        name="fourier",
    )(rest, cst, rev, w2)


def _merge_kernel(*refs, n_parts, n_lat_tiles):
    h_ref = refs[0]
    att_refs = refs[1:1 + n_parts]
    rec_ref = refs[1 + n_parts]
    fou_refs = refs[2 + n_parts:2 + 2 * n_parts]
    wgs, bgs, wbs = (refs[2 + 2 * n_parts + 3 * k:5 + 2 * n_parts + 3 * k] for k in range(3))
    o_ref = refs[-1]
    is_latent = pl.program_id(0) < n_lat_tiles

    def rows_of(parts):
        if len(parts) == 1:
            return parts[0][...]
        return jnp.where(is_latent, parts[0][...], parts[1][...])

    h = h_ref[...]
    acc = None
    for y, wg, bg, wb in zip((rows_of(att_refs), rec_ref[...], rows_of(fou_refs)), wgs, bgs, wbs):
        gate = jax.nn.sigmoid(_dot(h, wg[...]) + bg[...])
        term = gate * _dot(y, wb[...])
        acc = term if acc is None else acc + term
    o_ref[...] = acc.astype(BF16)


def _merge(h, att_parts, rec, fou_parts, w_gate, b_gate, w_branch, l, *, rows):
    d = h.shape[1]
    d_b = rec.shape[1]
    tm, tn = ROW_TILE, 512
    nj = d // tn
    n_parts = len(att_parts)
    n_lat_tiles = att_parts[0].shape[0] // tm
    y_spec = pl.BlockSpec((tm, d_b), lambda i, j: (i, 0))
    part_specs = [pl.BlockSpec((tm, d_b), lambda i, j: (jnp.minimum(i, n_lat_tiles - 1), 0)),
                  pl.BlockSpec((tm, d_b), lambda i, j: (jnp.maximum(i - n_lat_tiles, 0), 0))
                  ][:n_parts]
    wg_spec = lambda k: pl.BlockSpec((None, d, tn), lambda i, j: (l, 0, k * nj + j))
    bg_spec = lambda k: pl.BlockSpec((None, 1, tn), lambda i, j: (l, 0, k * nj + j))
    wb_spec = lambda k: pl.BlockSpec((None, None, d_b, tn), lambda i, j: (l, k, 0, j))
    bg = b_gate.reshape(b_gate.shape[0], 1, -1)
    return pl.pallas_call(
        functools.partial(_merge_kernel, n_parts=n_parts, n_lat_tiles=n_lat_tiles),
        grid=(rows // tm, nj),
        in_specs=[pl.BlockSpec((tm, d), lambda i, j: (i, 0))] + part_specs + [y_spec] + part_specs
                 + [wg_spec(0), wg_spec(1), wg_spec(2), bg_spec(0), bg_spec(1), bg_spec(2),
                    wb_spec(0), wb_spec(1), wb_spec(2)],
        out_specs=pl.BlockSpec((tm, tn), lambda i, j: (i, j)),
        out_shape=jax.ShapeDtypeStruct((rows, d), BF16),
        compiler_params=_params("parallel", "arbitrary"),
        name="gated_merge",
    )(h, *att_parts, rec, *fou_parts, w_gate, w_gate, w_gate, bg, bg, bg,
      w_branch, w_branch, w_branch)


def _out_proj_kernel(*refs, n_lat_tiles):
    x_refs, (m_ref, w_ref, mod_ref, g_ref, o_ref, h_ref) = refs[:-6], refs[-6:]
    delta = mod_ref[2:3, :] * _dot(m_ref[...], w_ref[...])

    def emit(r):
        y = r[...] + delta
        o_ref[...] = y
        h_ref[...] = _ada_norm(y, g_ref[...], mod_ref[3:4, :], mod_ref[4:5, :])

    if len(x_refs) == 2:
        _for_row_source(*x_refs, n_lat_tiles, emit)
    else:
        emit(x_refs[0])


def _out_proj(x_parts, m, w_out, l, mod_l, g_norm2, *, cfg, rows):
    d = m.shape[1]
    tm = ROW_TILE
    row_spec = pl.BlockSpec((tm, d), lambda i: (i, 0))
    n_lat_tiles = x_parts[0].shape[0] // tm
    x_specs = _split_row_specs(n_lat_tiles, tm, d) if len(x_parts) == 2 else [row_spec]
    return pl.pallas_call(
        functools.partial(_out_proj_kernel, n_lat_tiles=n_lat_tiles),
        grid=(rows // tm,),
        in_specs=x_specs + [
            row_spec,
            pl.BlockSpec((None, d, d), lambda i: (l, 0, 0), pipeline_mode=pl.Buffered(1)),
            pl.BlockSpec((None, 6, d), lambda i: (cfg["mod_row"](i), 0, 0)),
            pl.BlockSpec((1, d), lambda i: (0, 0)),
        ],
        out_specs=[row_spec, row_spec],
        out_shape=[jax.ShapeDtypeStruct((rows, d), F32), jax.ShapeDtypeStruct((rows, d), BF16)],
        compiler_params=_params("parallel"),
        name="out_proj",
    )(*x_parts, m, w_out, mod_l, g_norm2.reshape(1, d))


def _ffn_up_kernel(h_ref, wg_ref, wu_ref, *refs):
    n_cast = (len(refs) - 1) // 2
    a_ref = refs[n_cast]
    h = h_ref[...]
    gt = _dot(h, wg_ref[...])
    up = _dot(h, wu_ref[...])
    a_ref[...] = (gt * jax.nn.sigmoid(gt) * up).astype(BF16)
    for src, dst in zip(refs[:n_cast], refs[n_cast + 1:]):
        if len(dst.shape) == 2:
            dst[...] = src[...].astype(BF16)
        else:
            for g in range(dst.shape[0]):
                dst[g] = src[:, g * dst.shape[2]:(g + 1) * dst.shape[2]].astype(BF16)


def _col_blocked(w, n_groups):
    l, r, c = w.shape
    return w.reshape(l, r, n_groups, c // n_groups).transpose(0, 2, 1, 3)


def _ffn_up(h2, w_ffn_in, l, *, cfg, rows, cast_srcs=(), cast_layer=None):
    d = h2.shape[1]
    d_ff = w_ffn_in.shape[2] // 2
    tm, tf = FFN_UP_ROW_TILE, cfg["tf"]
    nk = d_ff // tf
    n_steps = (rows // tm) * nk
    cast_in, cast_out, cast_shape = [], [], []
    for w, groups in cast_srcs:
        _, r, c = w.shape
        br = next(x for x in (16, 32, 64, 128, 256) if r % x == 0 and r // x <= n_steps)
        blk = lambda i, k, n=r // br: jnp.minimum(i * nk + k, n - 1)
        cast_in.append(pl.BlockSpec((None, br, c), lambda i, k, blk=blk: (cast_layer, blk(i, k), 0)))
        if groups == 1:
            cast_out.append(pl.BlockSpec((None, br, c), lambda i, k, blk=blk: (0, blk(i, k), 0)))
            cast_shape.append(jax.ShapeDtypeStruct((1, r, c), BF16))
        else:
            cast_out.append(pl.BlockSpec((None, groups, br, c // groups),
                                         lambda i, k, blk=blk: (0, 0, blk(i, k), 0)))
            cast_shape.append(jax.ShapeDtypeStruct((1, groups, r, c // groups), BF16))
    return pl.pallas_call(
        _ffn_up_kernel,
        grid=(rows // tm, nk),
        in_specs=[
            pl.BlockSpec((tm, d), lambda i, k: (i, 0)),
            pl.BlockSpec((None, d, tf), lambda i, k: (l, 0, k)),
            pl.BlockSpec((None, d, tf), lambda i, k: (l, 0, nk + k)),
        ] + cast_in,
        out_specs=[pl.BlockSpec((tm, tf), lambda i, k: (i, k))] + cast_out,
        out_shape=[jax.ShapeDtypeStruct((rows, d_ff), BF16)] + cast_shape,
        compiler_params=_params("arbitrary", "arbitrary"),
        name="ffn_up",
    )(h2, w_ffn_in, w_ffn_in, *[w for w, _ in cast_srcs])


def _ffn_down_kernel(a_ref, x_ref, wo_ref, mod_ref, nmod_ref, g_ref, *refs, nj, d, final):
    out_refs, y_scr = refs[:-1], refs[-1]
    i, j = pl.program_id(0), pl.program_id(1)
    cur = i % 2

    @pl.when((i == 0) & (j == 0))
    def _():
        y_scr[1] = jnp.zeros(y_scr.shape[1:], F32)

    prv = y_scr.at[1 - cur]
    ssq = sum(jnp.sum(prv[jj] * prv[jj], axis=-1, keepdims=True) for jj in range(nj))
    normed = prv[j] * lax.rsqrt(ssq * (1.0 / d) + NORM_EPS) * g_ref[...]
    if final:
        out_refs[0][...] = normed
    else:
        out_refs[1][...] = (normed * (1.0 + nmod_ref[1:2, :]) + nmod_ref[0:1, :]).astype(BF16)

    jw = jnp.where(i == pl.num_programs(0) - 1, nj - 1, j)
    y = x_ref[...] + mod_ref[5:6, :] * _dot(a_ref[...], wo_ref[jw])
    if not final:
        out_refs[0][...] = y
    y_scr[cur, j] = y


def _ffn_down(a, xmid, w_ffn_out, l, mod_l, next_mod, gain, *, cfg, rows, final):
    d = xmid.shape[1]
    d_ff = a.shape[1]
    tm = ROW_TILE
    _, nj, _, tn = w_ffn_out.shape
    n_tiles = rows // tm
    cur = lambda i: jnp.minimum(i, n_tiles - 1)
    prev = lambda i: jnp.maximum(i - 1, 0)
    cur_j = lambda i, j: jnp.where(i == n_tiles, nj - 1, j)
    prev_j = lambda i, j: jnp.where(i == 0, 0, j)
    cur_spec = pl.BlockSpec((tm, tn), lambda i, j: (cur(i), cur_j(i, j)))
    prev_spec = pl.BlockSpec((tm, tn), lambda i, j: (prev(i), prev_j(i, j)))
    if final:
        out_specs, out_shape = [prev_spec], [jax.ShapeDtypeStruct((rows, d), F32)]
    else:
        out_specs = [cur_spec, prev_spec]
        out_shape = [jax.ShapeDtypeStruct((rows, d), F32), jax.ShapeDtypeStruct((rows, d), BF16)]
    return pl.pallas_call(
        functools.partial(_ffn_down_kernel, nj=nj, d=d, final=final),
        grid=(n_tiles + 1, nj),
        in_specs=[
            pl.BlockSpec((tm, d_ff), lambda i, j: (cur(i), 0)),
            cur_spec,
            pl.BlockSpec((None, nj, d_ff, tn), lambda i, j: (l, 0, 0, 0),
                         pipeline_mode=pl.Buffered(1)),
            pl.BlockSpec((None, 6, tn), lambda i, j: (cfg["mod_row"](cur(i)), 0, cur_j(i, j))),
            pl.BlockSpec((None, 6, tn), lambda i, j: (cfg["mod_row"](prev(i)), 0, j)),
            pl.BlockSpec((1, tn), lambda i, j: (0, j)),
        ],
        out_specs=out_specs,
        out_shape=out_shape,
        scratch_shapes=[pltpu.VMEM((2, nj, tm, tn), F32)],
        compiler_params=_params("arbitrary", "arbitrary"),
        name="ffn_down",
    )(a, xmid, w_ffn_out, mod_l, next_mod, gain.reshape(1, d))


def _rope_tables(t, tile_rows):
    pairs = HEAD_DIM // 4
    rows = t // GRID_W
    row = jnp.repeat(jnp.arange(rows, dtype=F32), GRID_W)
    col = jnp.tile(jnp.arange(GRID_W, dtype=F32), rows)
    inv = ROPE_THETA ** (-jnp.arange(pairs, dtype=F32) / pairs)
    ang = jnp.concatenate([row[:, None] * inv, col[:, None] * inv], axis=-1)
    cos, sin = jnp.cos(ang), jnp.sin(ang)
    cosf = jnp.concatenate([cos, cos], axis=-1)
    sinf = jnp.concatenate([-sin, sin], axis=-1)
    cosf = jnp.concatenate([cosf, jnp.ones((tile_rows, HEAD_DIM), F32)], axis=0)
    sinf = jnp.concatenate([sinf, jnp.zeros((tile_rows, HEAD_DIM), F32)], axis=0)
    return cosf, sinf


def kernel(x, c, ctx, c_ctx, w_mod, b_mod, g_norm1, g_norm2, w_in, q_gain, k_gain, conv_w, conv_b, lru_w_r, lru_b_r, lru_w_i, lru_b_i, lru_lambda, w_branch, w_gate, b_gate, w_out, w_ffn_in, w_ffn_out, g_final):
    b, t, d = x.shape
    tc = ctx.shape[1]
    depth = w_in.shape[0]
    d_branch = w_branch.shape[2]
    n_q_heads = d_branch // HEAD_DIM
    n_kv_heads = n_q_heads // Q_PER_KV
    d_kv = n_kv_heads * HEAD_DIM
    d_ff = w_ffn_out.shape[1]
    n_lat, n_ctx = b * t, b * tc
    assert b == SUBLANES and t % IN_PROJ_ROW_TILE == 0 and n_ctx % FFN_UP_ROW_TILE == 0
    assert tc % LRU_CHUNK == 0 and t % LRU_CHUNK == 0
    tiles_per_batch = t // ROW_TILE

    cfg = dict(
        b=b, t=t, tc=tc, d_branch=d_branch, n_q_heads=n_q_heads, n_kv_heads=n_kv_heads,
        tn_in=d_branch + 2 * d_kv,
        ux_col=0, ug_col=d_branch, uf_col=2 * d_branch,
        tf=512 if d_ff % 512 == 0 else 256,
        mod_row=lambda i: jnp.minimum(i // tiles_per_batch, b),
        rope_block=lambda i: jnp.where(i < n_lat // IN_PROJ_ROW_TILE, i % (t // IN_PROJ_ROW_TILE),
                                       t // IN_PROJ_ROW_TILE),
    )

    x_parts = (x.reshape(n_lat, d), ctx.reshape(n_ctx, d))
    cc = jnp.concatenate([c, c_ctx[None, :], jnp.zeros((2 * SUBLANES - b - 1, d), F32)], axis=0)
    mod = _modulation(cc, w_mod, b_mod).reshape(depth, 2 * SUBLANES, 6, d)

    w_f32 = ((w_in, 1), (w_gate, 1), (w_branch.reshape(depth, -1, d), 1), (w_out, 1),
             (w_ffn_in, 1), (w_ffn_out, FFN_DOWN_COL_BLOCKS))
    wb = [(w[0:1] if g == 1 else _col_blocked(w[0:1], g)).astype(BF16) for w, g in w_f32]

    cosf, sinf = _rope_tables(t, IN_PROJ_ROW_TILE)
    dft_lat = _dft_tables(t)
    dft_ctx = _dft_tables(tc)

    h1 = _prenorm(*x_parts, mod[0], g_norm1[0], cfg=cfg)
    out = None
    for l in range(depth):
        last = l == depth - 1
        rows = n_lat if last else n_lat + n_ctx
        w_in_b, w_gate_b, w_branch_b, w_out_b, w_ffn_in_b, w_ffn_out_b = wb
        w_branch_b = w_branch_b.reshape((1,) + w_branch.shape[1:])
        qkv, rest = _in_proj(h1, w_in_b, 0, cosf, sinf, q_gain[l], k_gain[l], cfg=cfg)
        att = (_attention(qkv, cfg=cfg, latent=True),)
        rec = _lru(rest, conv_w[l], conv_b[l], _lru_block_diag(lru_w_r[l], lru_w_i[l]),
                   lru_b_r[l], lru_b_i[l], lru_lambda[l], cfg=cfg)
        fou = (_fourier(rest, *dft_lat, cfg=cfg, seq_len=t, row_blk0=0),)
        if not last:
            att += (_attention(qkv, cfg=cfg, latent=False),)
            fou += (_fourier(rest, *dft_ctx, cfg=cfg, seq_len=tc, row_blk0=n_lat // tc),)
        m = _merge(h1, att, rec, fou, w_gate_b, b_gate[l:l + 1], w_branch_b, 0, rows=rows)
        xmid, h2 = _out_proj(x_parts, m, w_out_b, 0, mod[l], g_norm2[l], cfg=cfg, rows=rows)
        if last:
            (a,) = _ffn_up(h2, w_ffn_in_b, 0, cfg=cfg, rows=rows)
            (out,) = _ffn_down(a, xmid, w_ffn_out_b, 0, mod[l], mod[l], g_final,
                               cfg=cfg, rows=rows, final=True)
        else:
            a, *wb = _ffn_up(h2, w_ffn_in_b, 0, cfg=cfg, rows=rows, cast_srcs=w_f32,
                             cast_layer=l + 1)
            xall, h1 = _ffn_down(a, xmid, w_ffn_out_b, 0, mod[l], mod[l + 1], g_norm1[l + 1],
                                 cfg=cfg, rows=rows, final=False)
            x_parts = (xall,)
    return out.reshape(b, t, d)
```

```python
import functools
import math

import jax
import jax.numpy as jnp
from jax import lax
from jax.experimental import pallas as pl
from jax.experimental.pallas import tpu as pltpu

F32 = jnp.float32
BF16 = jnp.bfloat16

HEAD_DIM = 128
Q_PER_KV = 4
GRID_W = 64
ROPE_THETA = 10000.0
NORM_EPS = 1e-6
LRU_C = 8.0
FOURIER_GROUP = 128

LANES = 128
SUBLANES = 8
VMEM_LIMIT_BYTES = 56 * 1024 * 1024

ROW_TILE = 512
IN_PROJ_ROW_TILE = 1024
FFN_UP_ROW_TILE = 1024
FFN_DOWN_COL_BLOCKS = 4
ATTN_Q_TILE = 2048
ATTN_SUB_ROWS = 512
LRU_CHUNK = 256
LRU_GAP = SUBLANES


def _params(*semantics):
    return pltpu.CompilerParams(dimension_semantics=semantics,
                                vmem_limit_bytes=VMEM_LIMIT_BYTES)


def _dot(a, b):
    return jnp.dot(a, b, preferred_element_type=F32)


def _dot_nt(a, b):
    return lax.dot_general(a, b, (((1,), (1,)), ((), ())), preferred_element_type=F32)


def _rms(x):
    return x * lax.rsqrt(jnp.mean(x * x, axis=-1, keepdims=True) + NORM_EPS)


def _ada_norm(x, gain, shift, scale):
    return (_rms(x) * gain * (1.0 + scale) + shift).astype(BF16)


def _mod_kernel(c_ref, w_ref, b_ref, o_ref):
    c = c_ref[...]
    s = (c * jax.nn.sigmoid(c)).astype(BF16)
    o_ref[...] = _dot(s, w_ref[...].astype(BF16)) + b_ref[...]


def _modulation(cc, w_mod, b_mod):
    depth, d, n = w_mod.shape
    rows = cc.shape[0]
    tn = 1024
    return pl.pallas_call(
        _mod_kernel,
        grid=(depth, n // tn),
        in_specs=[
            pl.BlockSpec((rows, d), lambda l, j: (0, 0)),
            pl.BlockSpec((None, d, tn), lambda l, j: (l, 0, j)),
            pl.BlockSpec((None, 1, tn), lambda l, j: (l, 0, j)),
        ],
        out_specs=pl.BlockSpec((None, rows, tn), lambda l, j: (l, 0, j)),
        out_shape=jax.ShapeDtypeStruct((depth, rows, n), F32),
        compiler_params=_params("parallel", "parallel"),
        name="modulation",
    )(cc, w_mod, b_mod.reshape(depth, 1, n))


def _split_row_specs(n_lat_tiles, tm, d):
    return [pl.BlockSpec((tm, d), lambda i: (jnp.minimum(i, n_lat_tiles - 1), 0)),
            pl.BlockSpec((tm, d), lambda i: (jnp.maximum(i - n_lat_tiles, 0), 0))]


def _for_row_source(x_ref, c_ref, n_lat_tiles, fn):
    i = pl.program_id(0)
    pl.when(i < n_lat_tiles)(lambda: fn(x_ref))
    pl.when(i >= n_lat_tiles)(lambda: fn(c_ref))


def _prenorm_kernel(x_ref, c_ref, mod_ref, g_ref, h_ref, *, n_lat_tiles):
    def emit(r):
        h_ref[...] = _ada_norm(r[...], g_ref[...], mod_ref[0:1, :], mod_ref[1:2, :])
    _for_row_source(x_ref, c_ref, n_lat_tiles, emit)


def _prenorm(x2d, ctx2d, mod_l, g_norm, *, cfg):
    d = x2d.shape[1]
    rows = x2d.shape[0] + ctx2d.shape[0]
    tm = ROW_TILE
    n_lat_tiles = x2d.shape[0] // tm
    return pl.pallas_call(
        functools.partial(_prenorm_kernel, n_lat_tiles=n_lat_tiles),
        grid=(rows // tm,),
        in_specs=_split_row_specs(n_lat_tiles, tm, d) + [
            pl.BlockSpec((None, 6, d), lambda i: (cfg["mod_row"](i), 0, 0)),
            pl.BlockSpec((1, d), lambda i: (0, 0)),
        ],
        out_specs=pl.BlockSpec((tm, d), lambda i: (i, 0)),
        out_shape=jax.ShapeDtypeStruct((rows, d), BF16),
        compiler_params=_params("parallel"),
        name="prenorm",
    )(x2d, ctx2d, mod_l, g_norm.reshape(1, d))


def _in_proj_kernel(h_ref, w_ref, cos_ref, sin_ref, qg_ref, kg_ref, qkv_ref, rest_ref, qkv_scr,
                    *, n_q_heads, n_kv_heads):
    j = pl.program_id(1)

    @pl.when(j == 0)
    def _():
        qkv_scr[...] = _dot(h_ref[...], w_ref[...])

    @pl.when(j == 1)
    def _():
        rest_ref[...] = _dot(h_ref[...], w_ref[...]).astype(BF16)
        cos = cos_ref[...]
        sin = sin_ref[...]
        q_gain = qg_ref[...] * (HEAD_DIM ** -0.5 * math.log2(math.e))
        k_gain = kg_ref[...]
        n_rot = n_q_heads + n_kv_heads
        for hh in range(n_rot):
            cols = slice(hh * HEAD_DIM, (hh + 1) * HEAD_DIM)
            y = _rms(qkv_scr[:, cols]) * (q_gain if hh < n_q_heads else k_gain)
            y = y * cos + pltpu.roll(y, HEAD_DIM // 2, axis=1) * sin
            qkv_ref[:, cols] = y.astype(BF16)
        qkv_ref[:, n_rot * HEAD_DIM:] = qkv_scr[:, n_rot * HEAD_DIM:].astype(BF16)

    @pl.when(j > 1)
    def _():
        rest_ref[...] = _dot(h_ref[...], w_ref[...]).astype(BF16)


def _in_proj(h1, w_in, l, cosf, sinf, q_gain, k_gain, *, cfg):
    rows, d = h1.shape
    d_in = w_in.shape[2]
    tm, tn = IN_PROJ_ROW_TILE, cfg["tn_in"]
    assert d_in == 3 * tn
    kern = functools.partial(_in_proj_kernel, n_q_heads=cfg["n_q_heads"],
                             n_kv_heads=cfg["n_kv_heads"])
    return pl.pallas_call(
        kern,
        grid=(rows // tm, d_in // tn),
        in_specs=[
            pl.BlockSpec((tm, d), lambda i, j: (i, 0)),
            pl.BlockSpec((None, d, tn), lambda i, j: (l, 0, j)),
            pl.BlockSpec((tm, HEAD_DIM), lambda i, j: (cfg["rope_block"](i), 0)),
            pl.BlockSpec((tm, HEAD_DIM), lambda i, j: (cfg["rope_block"](i), 0)),
            pl.BlockSpec((1, HEAD_DIM), lambda i, j: (0, 0)),
            pl.BlockSpec((1, HEAD_DIM), lambda i, j: (0, 0)),
        ],
        out_specs=[
            pl.BlockSpec((tm, tn), lambda i, j: (i, 0)),
            pl.BlockSpec((tm, tn), lambda i, j: (i, jnp.maximum(j - 1, 0))),
        ],
        out_shape=[
            jax.ShapeDtypeStruct((rows, tn), BF16),
            jax.ShapeDtypeStruct((rows, d_in - tn), BF16),
        ],
        scratch_shapes=[pltpu.VMEM((tm, tn), F32)],
        compiler_params=_params("parallel", "arbitrary"),
        name="in_proj",
    )(h1, w_in, cosf, sinf, q_gain.reshape(1, HEAD_DIM), k_gain.reshape(1, HEAD_DIM))


def _attn_kernel(q_ref, *refs, latent):
    if latent:
        kl_ref, kc_ref, vl_ref, vc_ref, o_ref, vl_aug, vc_aug = refs
    else:
        kc_ref, vc_ref, o_ref, vc_aug = refs

    @pl.when(pl.program_id(2) == 0)
    def _():
        for v_ref, aug in ((vl_ref, vl_aug), (vc_ref, vc_aug)) if latent else ((vc_ref, vc_aug),):
            lane = lax.broadcasted_iota(jnp.int32, v_ref.shape, 1)
            aug[:, :HEAD_DIM] = v_ref[...]
            aug[:, HEAD_DIM:] = jnp.where(lane == 0, 1.0, 0.0).astype(BF16)

    kc = kc_ref[...]
    sub = min(q_ref.shape[0], ATTN_SUB_ROWS)
    for r0 in range(0, q_ref.shape[0], sub):
        for g in range(Q_PER_KV):
            cols = slice(g * HEAD_DIM, (g + 1) * HEAD_DIM)
            q = q_ref[r0:r0 + sub, cols]
            sc = _dot_nt(q, kc)
            m = jnp.max(sc, axis=-1, keepdims=True)
            if latent:
                sl = _dot_nt(q, kl_ref[...])
                m = jnp.maximum(m, jnp.max(sl, axis=-1, keepdims=True))
            o = _dot(jnp.exp2((sc - m).astype(BF16)), vc_aug[...])
            if latent:
                o = o + _dot(jnp.exp2((sl - m).astype(BF16)), vl_aug[...])
            o_ref[r0:r0 + sub, cols] = (
                o[:, :HEAD_DIM] / o[:, HEAD_DIM:HEAD_DIM + 1]).astype(BF16)


def _attention(qkv, *, cfg, latent):
    b, t, tc = cfg["b"], cfg["t"], cfg["tc"]
    tq = min(ATTN_Q_TILE, t) if latent else tc
    assert t % tq == 0
    n_q_tiles = t // tq if latent else 1
    q_blk0 = 0 if latent else b * t // tc
    n_kv = cfg["n_kv_heads"]
    gw = Q_PER_KV * HEAD_DIM
    k_col = cfg["n_q_heads"]
    v_col = k_col + n_kv
    ctx_blk0 = b * t // tc

    q_spec = pl.BlockSpec((tq, gw), lambda bi, h, qi: (q_blk0 + bi * n_q_tiles + qi, h))
    lat_spec = lambda col: pl.BlockSpec((t, HEAD_DIM), lambda bi, h, qi: (bi, col + h))
    ctx_spec = lambda col: pl.BlockSpec((tc, HEAD_DIM), lambda bi, h, qi: (ctx_blk0 + bi, col + h))
    aug = lambda n: pltpu.VMEM((n, 2 * HEAD_DIM), BF16)
    if latent:
        in_specs = [q_spec, lat_spec(k_col), ctx_spec(k_col), lat_spec(v_col), ctx_spec(v_col)]
        scratch = [aug(t), aug(tc)]
    else:
        in_specs = [q_spec, ctx_spec(k_col), ctx_spec(v_col)]
        scratch = [aug(tc)]
    return pl.pallas_call(
        functools.partial(_attn_kernel, latent=latent),
        grid=(b, n_kv, n_q_tiles),
        in_specs=in_specs,
        out_specs=pl.BlockSpec((tq, gw), lambda bi, h, qi: (bi * n_q_tiles + qi, h)),
        out_shape=jax.ShapeDtypeStruct((b * (t if latent else tc), cfg["n_q_heads"] * HEAD_DIM),
                                       BF16),
        scratch_shapes=scratch,
        compiler_params=_params("parallel", "parallel", "arbitrary"),
        name="attention",
    )(*[qkv] * len(in_specs))


def _lru_kernel(ux_ref, ug_ref, cw_ref, cb_ref, w_ref, br_ref, bi_ref, lam_ref, o_ref,
                u_scr, acc_scr, af_scr, bf_scr, yf_scr, ab_scr, bb_scr, yb_scr, *, b, t, tc):
    ch, gap = LRU_CHUNK, LRU_GAP
    n_lat = b * t
    conv_width = cw_ref.shape[0]
    conv_left = conv_width // 2

    lat = (gap, 0, t + gap, t)
    ctx = (gap + b * (t + gap), n_lat, tc + gap, tc)

    zeros_gap = jnp.zeros((gap, LANES), F32)
    for kind, n in ((lat, t), (ctx, tc)):
        for bi in range(b):
            u0 = kind[0] + bi * kind[2]
            r0 = kind[1] + bi * kind[3]
            u_scr[u0 - gap:u0, :] = zeros_gap
            u_scr[u0:u0 + n, :] = ux_ref[r0:r0 + n, :].astype(F32)
    u_end = ctx[0] + b * ctx[2] - gap
    u_scr[u_end:u_end + gap, :] = zeros_gap

    def conv_chunk(s0):
        u = cb_ref[...]
        for j in range(conv_width):
            tap0 = s0 - conv_left + j
            u = u + cw_ref[j:j + 1, :] * u_scr[tap0:tap0 + ch, :]
        return u

    for kind, n in ((lat, t), (ctx, tc)):
        for bi in range(b):
            u0 = kind[0] + bi * kind[2]
            held = None
            for c in range(n // ch):
                cur = conv_chunk(u0 + c * ch)
                if held is not None:
                    u_scr[u0 + (c - 1) * ch:u0 + c * ch, :] = held
                held = cur
            u_scr[u0 + n - ch:u0 + n, :] = held

    def slab_rows(kind, bi, off):
        start = kind[1] + bi * kind[3] + off
        if not isinstance(start, int):
            start = pl.multiple_of(start, ch)
        return pl.ds(start, ch)

    def coefficients(d, kind, off, a_scr, b_scr):
        lam = lam_ref[d:d + 1, :]
        softplus_neg_lam = jnp.maximum(-lam, 0.0) + jnp.log1p(jnp.exp(-jnp.abs(lam)))
        half_rate = -0.5 * LRU_C * softplus_neg_lam
        w_half = w_ref[:, d * 2 * LANES:(d + 1) * 2 * LANES] * 0.5
        br_half = 0.5 * br_ref[d:d + 1, :]
        bi_half = 0.5 * bi_ref[d:d + 1, :]
        for bi in range(b):
            s0 = kind[0] + bi * kind[2] + off
            if not isinstance(s0, int):
                s0 = pl.multiple_of(s0, SUBLANES)
            u = u_scr[pl.ds(s0, ch), :]
            gates = _dot(u.astype(BF16), w_half)
            log_a = jnp.tanh(gates[:, :LANES] + br_half) * half_rate + half_rate
            i_gate = 0.5 * jnp.tanh(gates[:, LANES:] + bi_half) + 0.5
            a = jnp.exp(log_a)
            one_minus_a2 = -jnp.tanh(log_a) * (a * a + 1.0)
            root = jnp.where(one_minus_a2 > 0.0, one_minus_a2 * lax.rsqrt(one_minus_a2), 0.0)
            a_scr[pl.ds(bi, ch, stride=b), :] = a
            b_scr[pl.ds(bi, ch, stride=b), :] = root * i_gate * u

    def emit(rows, y):
        gate = jax.nn.gelu(ug_ref[rows, :].astype(F32), approximate=True)
        o_ref[rows, :] = (y * gate).astype(BF16)

    def pair(kind, f_off, b_off, carry, mode):
        coefficients(0, kind, f_off, af_scr, bf_scr)
        coefficients(1, kind, b_off, ab_scr, bb_scr)

        def two_steps(h, a_scr, b_scr, y_scr, s0, s1):
            r0 = pl.ds(pl.multiple_of(s0 * b, b), b)
            r1 = pl.ds(pl.multiple_of(s1 * b, b), b)
            a0, b0, a1, b1 = a_scr[r0, :], b_scr[r0, :], a_scr[r1, :], b_scr[r1, :]
            y_scr[r0, :] = a0 * h + b0
            h = (a1 * a0) * h + (a1 * b0 + b1)
            y_scr[r1, :] = h
            return h

        def step(s, hs):
            hf, hb = hs
            hf = two_steps(hf, af_scr, bf_scr, yf_scr, 2 * s, 2 * s + 1)
            hb = two_steps(hb, ab_scr, bb_scr, yb_scr, ch - 1 - 2 * s, ch - 2 - 2 * s)
            return hf, hb

        carry = lax.fori_loop(0, ch // 2, step, carry, unroll=16)

        for bi in range(b):
            blk = pl.ds(bi, ch, stride=b)
            rf = slab_rows(kind, bi, f_off)
            rb = slab_rows(kind, bi, b_off)
            if mode == "same":
                emit(rf, yf_scr[blk, :] + yb_scr[blk, :])
            elif mode == "first":
                acc_scr[rf, :] = yf_scr[blk, :]
                acc_scr[rb, :] = yb_scr[blk, :]
            else:
                emit(rf, acc_scr[rf, :] + yf_scr[blk, :])
                emit(rb, acc_scr[rb, :] + yb_scr[blk, :])
        return carry

    n_lat_chunks = t // ch
    half = n_lat_chunks // 2
    zero = jnp.zeros((b, LANES), F32)
    carry = pair(ctx, 0, 0, (zero, zero), "same")
    carry = lax.fori_loop(
        0, half,
        lambda q, hs: pair(lat, q * ch, (n_lat_chunks - 1 - q) * ch, hs, "first"), carry)
    lax.fori_loop(
        0, half,
        lambda q, hs: pair(lat, (half + q) * ch, (half - 1 - q) * ch, hs, "second"), carry)


def _lru(rest, conv_w, conv_b, w_bd, b_r, b_i, lam, *, cfg):
    rows = rest.shape[0]
    d_lru = conv_w.shape[1]
    ux_blk = cfg["ux_col"] // LANES
    ug_blk = cfg["ug_col"] // LANES
    nb = cfg["b"]
    assert cfg["tc"] == LRU_CHUNK and (cfg["t"] // LRU_CHUNK) % 2 == 0
    kern = functools.partial(_lru_kernel, b=nb, t=cfg["t"], tc=cfg["tc"])
    vec = lambda n: pl.BlockSpec((n, LANES), lambda c: (0, c))
    return pl.pallas_call(
        kern,
        grid=(d_lru // LANES,),
        in_specs=[
            pl.BlockSpec((rows, LANES), lambda c: (0, ux_blk + c), pipeline_mode=pl.Buffered(1)),
            pl.BlockSpec((rows, LANES), lambda c: (0, ug_blk + c), pipeline_mode=pl.Buffered(1)),
            vec(conv_w.shape[0]), vec(1),
            pl.BlockSpec((None, LANES, 4 * LANES), lambda c: (c, 0, 0)),
            vec(2), vec(2), vec(2),
        ],
        out_specs=pl.BlockSpec((rows, LANES), lambda c: (0, c)),
        out_shape=jax.ShapeDtypeStruct((rows, d_lru), BF16),
        scratch_shapes=[
            pltpu.VMEM((rows + (2 * nb + 1) * LRU_GAP, LANES), F32),
            pltpu.VMEM((rows, LANES), F32),
        ] + [pltpu.VMEM((nb * LRU_CHUNK, LANES), F32)] * 6,
        compiler_params=_params("parallel"),
        name="rglru",
    )(rest, rest, conv_w, conv_b.reshape(1, d_lru), w_bd, b_r, b_i, lam)


def _lru_block_diag(w_r, w_i):
    _, nb, bw, _ = w_r.shape
    per = LANES // bw
    eye = jnp.eye(per, dtype=w_r.dtype)

    def bd(w):
        w = w.reshape(nb // per, per, bw, bw)
        return jnp.einsum("cipq,ij->cipjq", w, eye).reshape(nb // per, LANES, LANES)

    return jnp.concatenate([bd(w_r[0]), bd(w_i[0]), bd(w_r[1]), bd(w_i[1])], axis=-1).astype(BF16)


def _fourier_kernel(u_ref, cst_ref, rev_ref, w2_ref, o_ref, p_scr, q_scr, *, t, n_groups, scale):
    half = t // 2
    for g in range(n_groups):
        cols = slice(g * FOURIER_GROUP, (g + 1) * FOURIER_GROUP)
        pq = _dot(u_ref[:, cols], w2_ref[...])
        p_scr[:, cols] = pq[:, :FOURIER_GROUP].astype(BF16)
        q_scr[:, cols] = pq[:, FOURIER_GROUP:].astype(BF16)
    a = _dot(cst_ref[:, :t], p_scr[...])
    bs = _dot(cst_ref[:, t:], q_scr[...])
    o_ref[0:half, :] = ((a - bs) * scale).astype(BF16)
    mirrored = _dot(rev_ref[...], ((a + bs) * scale).astype(BF16))
    p = p_scr[...].astype(F32)
    odd_row = (lax.broadcasted_iota(jnp.int32, p.shape, 0) & 1) == 1
    mid = jnp.sum(jnp.where(odd_row, -p, p), axis=0, keepdims=True) * scale
    first_row = lax.broadcasted_iota(jnp.int32, mirrored.shape, 0) == 0
    o_ref[half:t, :] = jnp.where(first_row, mid, mirrored).astype(BF16)


def _dft_tables(t):
    half = t // 2
    k = jnp.arange(half, dtype=jnp.int32)[:, None]
    s = jnp.arange(t, dtype=jnp.int32)[None, :]
    at = ((k * s) % t).astype(F32) * (2.0 * math.pi / t)
    cst = jnp.concatenate([jnp.cos(at), jnp.sin(at)], axis=1).astype(BF16)
    r = jnp.arange(half, dtype=jnp.int32)
    rev = (r[:, None] + r[None, :] == half).astype(BF16)
    c = jnp.arange(FOURIER_GROUP, dtype=jnp.int32)
    ac = ((c[:, None] * c[None, :]) % FOURIER_GROUP).astype(F32) * (2.0 * math.pi / FOURIER_GROUP)
    w2 = jnp.concatenate([jnp.cos(ac), jnp.sin(ac)], axis=1).astype(BF16)
    return cst, rev, w2


def _fourier(rest, cst, rev, w2, *, cfg, seq_len, row_blk0):
    d_f = cfg["d_branch"]
    wcols = min(d_f, 512)
    uf_blk = cfg["uf_col"] // wcols
    kern = functools.partial(_fourier_kernel, t=seq_len, n_groups=wcols // FOURIER_GROUP,
                             scale=(seq_len * FOURIER_GROUP) ** -0.5)
    const = lambda shape: pl.BlockSpec(shape, lambda bi, hf: (0, 0), pipeline_mode=pl.Buffered(1))
    return pl.pallas_call(
        kern,
        grid=(cfg["b"], d_f // wcols),
        in_specs=[
            pl.BlockSpec((seq_len, wcols), lambda bi, hf: (row_blk0 + bi, uf_blk + hf)),
            const(cst.shape), const(rev.shape), const(w2.shape),
        ],
        out_specs=pl.BlockSpec((seq_len, wcols), lambda bi, hf: (bi, hf)),
        out_shape=jax.ShapeDtypeStruct((cfg["b"] * seq_len, d_f), BF16),
        scratch_shapes=[pltpu.VMEM((seq_len, wcols), BF16)] * 2,
        compiler_params=_params("parallel", "parallel"),
        name="fourier",
    )(rest, cst, rev, w2)


def _merge_kernel(*refs, n_parts, n_lat_tiles):
    h_ref = refs[0]
    att_refs = refs[1:1 + n_parts]
    rec_ref = refs[1 + n_parts]
    fou_refs = refs[2 + n_parts:2 + 2 * n_parts]
    wgs, bgs, wbs = (refs[2 + 2 * n_parts + 3 * k:5 + 2 * n_parts + 3 * k] for k in range(3))
    o_ref = refs[-1]
    is_latent = pl.program_id(0) < n_lat_tiles

    def rows_of(parts):
        if len(parts) == 1:
            return parts[0][...]
        return jnp.where(is_latent, parts[0][...], parts[1][...])

    h = h_ref[...]
    acc = None
    for y, wg, bg, wb in zip((rows_of(att_refs), rec_ref[...], rows_of(fou_refs)), wgs, bgs, wbs):
        gate = jax.nn.sigmoid(_dot(h, wg[...]) + bg[...])
        term = gate * _dot(y, wb[...])
        acc = term if acc is None else acc + term
    o_ref[...] = acc.astype(BF16)


def _merge(h, att_parts, rec, fou_parts, w_gate, b_gate, w_branch, l, *, rows):
    d = h.shape[1]
    d_b = rec.shape[1]
    tm, tn = ROW_TILE, 512
    nj = d // tn
    n_parts = len(att_parts)
    n_lat_tiles = att_parts[0].shape[0] // tm
    y_spec = pl.BlockSpec((tm, d_b), lambda i, j: (i, 0))
    part_specs = [pl.BlockSpec((tm, d_b), lambda i, j: (jnp.minimum(i, n_lat_tiles - 1), 0)),
                  pl.BlockSpec((tm, d_b), lambda i, j: (jnp.maximum(i - n_lat_tiles, 0), 0))
                  ][:n_parts]
    wg_spec = lambda k: pl.BlockSpec((None, d, tn), lambda i, j: (l, 0, k * nj + j))
    bg_spec = lambda k: pl.BlockSpec((None, 1, tn), lambda i, j: (l, 0, k * nj + j))
    wb_spec = lambda k: pl.BlockSpec((None, None, d_b, tn), lambda i, j: (l, k, 0, j))
    bg = b_gate.reshape(b_gate.shape[0], 1, -1)
    return pl.pallas_call(
        functools.partial(_merge_kernel, n_parts=n_parts, n_lat_tiles=n_lat_tiles),
        grid=(rows // tm, nj),
        in_specs=[pl.BlockSpec((tm, d), lambda i, j: (i, 0))] + part_specs + [y_spec] + part_specs
                 + [wg_spec(0), wg_spec(1), wg_spec(2), bg_spec(0), bg_spec(1), bg_spec(2),
                    wb_spec(0), wb_spec(1), wb_spec(2)],
        out_specs=pl.BlockSpec((tm, tn), lambda i, j: (i, j)),
        out_shape=jax.ShapeDtypeStruct((rows, d), BF16),
        compiler_params=_params("parallel", "arbitrary"),
        name="gated_merge",
    )(h, *att_parts, rec, *fou_parts, w_gate, w_gate, w_gate, bg, bg, bg,
      w_branch, w_branch, w_branch)


def _out_proj_kernel(*refs, n_lat_tiles):
    x_refs, (m_ref, w_ref, mod_ref, g_ref, o_ref, h_ref) = refs[:-6], refs[-6:]
    delta = mod_ref[2:3, :] * _dot(m_ref[...], w_ref[...])

    def emit(r):
        y = r[...] + delta
        o_ref[...] = y
        h_ref[...] = _ada_norm(y, g_ref[...], mod_ref[3:4, :], mod_ref[4:5, :])

    if len(x_refs) == 2:
        _for_row_source(*x_refs, n_lat_tiles, emit)
    else:
        emit(x_refs[0])


def _out_proj(x_parts, m, w_out, l, mod_l, g_norm2, *, cfg, rows):
    d = m.shape[1]
    tm = ROW_TILE
    row_spec = pl.BlockSpec((tm, d), lambda i: (i, 0))
    n_lat_tiles = x_parts[0].shape[0] // tm
    x_specs = _split_row_specs(n_lat_tiles, tm, d) if len(x_parts) == 2 else [row_spec]
    return pl.pallas_call(
        functools.partial(_out_proj_kernel, n_lat_tiles=n_lat_tiles),
        grid=(rows // tm,),
        in_specs=x_specs + [
            row_spec,
            pl.BlockSpec((None, d, d), lambda i: (l, 0, 0), pipeline_mode=pl.Buffered(1)),
            pl.BlockSpec((None, 6, d), lambda i: (cfg["mod_row"](i), 0, 0)),
            pl.BlockSpec((1, d), lambda i: (0, 0)),
        ],
        out_specs=[row_spec, row_spec],
        out_shape=[jax.ShapeDtypeStruct((rows, d), F32), jax.ShapeDtypeStruct((rows, d), BF16)],
        compiler_params=_params("parallel"),
        name="out_proj",
    )(*x_parts, m, w_out, mod_l, g_norm2.reshape(1, d))


def _ffn_up_kernel(h_ref, wg_ref, wu_ref, *refs):
    n_cast = (len(refs) - 1) // 2
    a_ref = refs[n_cast]
    h = h_ref[...]
    gt = _dot(h, wg_ref[...])
    up = _dot(h, wu_ref[...])
    a_ref[...] = (gt * jax.nn.sigmoid(gt) * up).astype(BF16)
    for src, dst in zip(refs[:n_cast], refs[n_cast + 1:]):
        if len(dst.shape) == 2:
            dst[...] = src[...].astype(BF16)
        else:
            for g in range(dst.shape[0]):
                dst[g] = src[:, g * dst.shape[2]:(g + 1) * dst.shape[2]].astype(BF16)


def _col_blocked(w, n_groups):
    l, r, c = w.shape
    return w.reshape(l, r, n_groups, c // n_groups).transpose(0, 2, 1, 3)


def _ffn_up(h2, w_ffn_in, l, *, cfg, rows, cast_srcs=(), cast_layer=None):
    d = h2.shape[1]
    d_ff = w_ffn_in.shape[2] // 2
    tm, tf = FFN_UP_ROW_TILE, cfg["tf"]
    nk = d_ff // tf
    n_steps = (rows // tm) * nk
    cast_in, cast_out, cast_shape = [], [], []
    for w, groups in cast_srcs:
        _, r, c = w.shape
        br = next(x for x in (16, 32, 64, 128, 256) if r % x == 0 and r // x <= n_steps)
        blk = lambda i, k, n=r // br: jnp.minimum(i * nk + k, n - 1)
        cast_in.append(pl.BlockSpec((None, br, c), lambda i, k, blk=blk: (cast_layer, blk(i, k), 0)))
        if groups == 1:
            cast_out.append(pl.BlockSpec((None, br, c), lambda i, k, blk=blk: (0, blk(i, k), 0)))
            cast_shape.append(jax.ShapeDtypeStruct((1, r, c), BF16))
        else:
            cast_out.append(pl.BlockSpec((None, groups, br, c // groups),
                                         lambda i, k, blk=blk: (0, 0, blk(i, k), 0)))
            cast_shape.append(jax.ShapeDtypeStruct((1, groups, r, c // groups), BF16))
    return pl.pallas_call(
        _ffn_up_kernel,
        grid=(rows // tm, nk),
        in_specs=[
            pl.BlockSpec((tm, d), lambda i, k: (i, 0)),
            pl.BlockSpec((None, d, tf), lambda i, k: (l, 0, k)),
            pl.BlockSpec((None, d, tf), lambda i, k: (l, 0, nk + k)),
        ] + cast_in,
        out_specs=[pl.BlockSpec((tm, tf), lambda i, k: (i, k))] + cast_out,
        out_shape=[jax.ShapeDtypeStruct((rows, d_ff), BF16)] + cast_shape,
        compiler_params=_params("arbitrary", "arbitrary"),
        name="ffn_up",
    )(h2, w_ffn_in, w_ffn_in, *[w for w, _ in cast_srcs])


def _ffn_down_kernel(a_ref, x_ref, wo_ref, mod_ref, nmod_ref, g_ref, *refs, nj, d, final):
    out_refs, y_scr = refs[:-1], refs[-1]
    i, j = pl.program_id(0), pl.program_id(1)
    cur = i % 2

    @pl.when((i == 0) & (j == 0))
    def _():
        y_scr[1] = jnp.zeros(y_scr.shape[1:], F32)

    prv = y_scr.at[1 - cur]
    ssq = sum(jnp.sum(prv[jj] * prv[jj], axis=-1, keepdims=True) for jj in range(nj))
    normed = prv[j] * lax.rsqrt(ssq * (1.0 / d) + NORM_EPS) * g_ref[...]
    if final:
        out_refs[0][...] = normed
    else:
        out_refs[1][...] = (normed * (1.0 + nmod_ref[1:2, :]) + nmod_ref[0:1, :]).astype(BF16)

    jw = jnp.where(i == pl.num_programs(0) - 1, nj - 1, j)
    y = x_ref[...] + mod_ref[5:6, :] * _dot(a_ref[...], wo_ref[jw])
    if not final:
        out_refs[0][...] = y
    y_scr[cur, j] = y


def _ffn_down(a, xmid, w_ffn_out, l, mod_l, next_mod, gain, *, cfg, rows, final):
    d = xmid.shape[1]
    d_ff = a.shape[1]
    tm = ROW_TILE
    _, nj, _, tn = w_ffn_out.shape
    n_tiles = rows // tm
    cur = lambda i: jnp.minimum(i, n_tiles - 1)
    prev = lambda i: jnp.maximum(i - 1, 0)
    cur_j = lambda i, j: jnp.where(i == n_tiles, nj - 1, j)
    prev_j = lambda i, j: jnp.where(i == 0, 0, j)
    cur_spec = pl.BlockSpec((tm, tn), lambda i, j: (cur(i), cur_j(i, j)))
    prev_spec = pl.BlockSpec((tm, tn), lambda i, j: (prev(i), prev_j(i, j)))
    if final:
        out_specs, out_shape = [prev_spec], [jax.ShapeDtypeStruct((rows, d), F32)]
    else:
        out_specs = [cur_spec, prev_spec]
        out_shape = [jax.ShapeDtypeStruct((rows, d), F32), jax.ShapeDtypeStruct((rows, d), BF16)]
    return pl.pallas_call(
        functools.partial(_ffn_down_kernel, nj=nj, d=d, final=final),
        grid=(n_tiles + 1, nj),
        in_specs=[
            pl.BlockSpec((tm, d_ff), lambda i, j: (cur(i), 0)),
            cur_spec,
            pl.BlockSpec((None, nj, d_ff, tn), lambda i, j: (l, 0, 0, 0),
                         pipeline_mode=pl.Buffered(1)),
            pl.BlockSpec((None, 6, tn), lambda i, j: (cfg["mod_row"](cur(i)), 0, cur_j(i, j))),
            pl.BlockSpec((None, 6, tn), lambda i, j: (cfg["mod_row"](prev(i)), 0, j)),
            pl.BlockSpec((1, tn), lambda i, j: (0, j)),
        ],
        out_specs=out_specs,
        out_shape=out_shape,
        scratch_shapes=[pltpu.VMEM((2, nj, tm, tn), F32)],
        compiler_params=_params("arbitrary", "arbitrary"),
        name="ffn_down",
    )(a, xmid, w_ffn_out, mod_l, next_mod, gain.reshape(1, d))


def _rope_tables(t, tile_rows):
    pairs = HEAD_DIM // 4
    rows = t // GRID_W
    row = jnp.repeat(jnp.arange(rows, dtype=F32), GRID_W)
    col = jnp.tile(jnp.arange(GRID_W, dtype=F32), rows)
    inv = ROPE_THETA ** (-jnp.arange(pairs, dtype=F32) / pairs)
    ang = jnp.concatenate([row[:, None] * inv, col[:, None] * inv], axis=-1)
    cos, sin = jnp.cos(ang), jnp.sin(ang)
    cosf = jnp.concatenate([cos, cos], axis=-1)
    sinf = jnp.concatenate([-sin, sin], axis=-1)
    cosf = jnp.concatenate([cosf, jnp.ones((tile_rows, HEAD_DIM), F32)], axis=0)
    sinf = jnp.concatenate([sinf, jnp.zeros((tile_rows, HEAD_DIM), F32)], axis=0)
    return cosf, sinf


def kernel(x, c, ctx, c_ctx, w_mod, b_mod, g_norm1, g_norm2, w_in, q_gain, k_gain, conv_w, conv_b, lru_w_r, lru_b_r, lru_w_i, lru_b_i, lru_lambda, w_branch, w_gate, b_gate, w_out, w_ffn_in, w_ffn_out, g_final):
    b, t, d = x.shape
    tc = ctx.shape[1]
    depth = w_in.shape[0]
    d_branch = w_branch.shape[2]
    n_q_heads = d_branch // HEAD_DIM
    n_kv_heads = n_q_heads // Q_PER_KV
    d_kv = n_kv_heads * HEAD_DIM
    d_ff = w_ffn_out.shape[1]
    n_lat, n_ctx = b * t, b * tc
    assert b == SUBLANES and t % IN_PROJ_ROW_TILE == 0 and n_ctx % FFN_UP_ROW_TILE == 0
    assert tc % LRU_CHUNK == 0 and t % LRU_CHUNK == 0
    tiles_per_batch = t // ROW_TILE

    cfg = dict(
        b=b, t=t, tc=tc, d_branch=d_branch, n_q_heads=n_q_heads, n_kv_heads=n_kv_heads,
        tn_in=d_branch + 2 * d_kv,
        ux_col=0, ug_col=d_branch, uf_col=2 * d_branch,
        tf=512 if d_ff % 512 == 0 else 256,
        mod_row=lambda i: jnp.minimum(i // tiles_per_batch, b),
        rope_block=lambda i: jnp.where(i < n_lat // IN_PROJ_ROW_TILE, i % (t // IN_PROJ_ROW_TILE),
                                       t // IN_PROJ_ROW_TILE),
    )

    x_parts = (x.reshape(n_lat, d), ctx.reshape(n_ctx, d))
    cc = jnp.concatenate([c, c_ctx[None, :], jnp.zeros((2 * SUBLANES - b - 1, d), F32)], axis=0)
    mod = _modulation(cc, w_mod, b_mod).reshape(depth, 2 * SUBLANES, 6, d)

    w_f32 = ((w_in, 1), (w_gate, 1), (w_branch.reshape(depth, -1, d), 1), (w_out, 1),
             (w_ffn_in, 1), (w_ffn_out, FFN_DOWN_COL_BLOCKS))
    wb = [(w[0:1] if g == 1 else _col_blocked(w[0:1], g)).astype(BF16) for w, g in w_f32]

    cosf, sinf = _rope_tables(t, IN_PROJ_ROW_TILE)
    dft_lat = _dft_tables(t)
    dft_ctx = _dft_tables(tc)

    h1 = _prenorm(*x_parts, mod[0], g_norm1[0], cfg=cfg)
    out = None
    for l in range(depth):
        last = l == depth - 1
        rows = n_lat if last else n_lat + n_ctx
        w_in_b, w_gate_b, w_branch_b, w_out_b, w_ffn_in_b, w_ffn_out_b = wb
        w_branch_b = w_branch_b.reshape((1,) + w_branch.shape[1:])
        qkv, rest = _in_proj(h1, w_in_b, 0, cosf, sinf, q_gain[l], k_gain[l], cfg=cfg)
        att = (_attention(qkv, cfg=cfg, latent=True),)
        rec = _lru(rest, conv_w[l], conv_b[l], _lru_block_diag(lru_w_r[l], lru_w_i[l]),
                   lru_b_r[l], lru_b_i[l], lru_lambda[l], cfg=cfg)
        fou = (_fourier(rest, *dft_lat, cfg=cfg, seq_len=t, row_blk0=0),)
        if not last:
            att += (_attention(qkv, cfg=cfg, latent=False),)
            fou += (_fourier(rest, *dft_ctx, cfg=cfg, seq_len=tc, row_blk0=n_lat // tc),)
        m = _merge(h1, att, rec, fou, w_gate_b, b_gate[l:l + 1], w_branch_b, 0, rows=rows)
        xmid, h2 = _out_proj(x_parts, m, w_out_b, 0, mod[l], g_norm2[l], cfg=cfg, rows=rows)
        if last:
            (a,) = _ffn_up(h2, w_ffn_in_b, 0, cfg=cfg, rows=rows)
            (out,) = _ffn_down(a, xmid, w_ffn_out_b, 0, mod[l], mod[l], g_final,
                               cfg=cfg, rows=rows, final=True)
        else:
            a, *wb = _ffn_up(h2, w_ffn_in_b, 0, cfg=cfg, rows=rows, cast_srcs=w_f32,
                             cast_layer=l + 1)
            xall, h1 = _ffn_down(a, xmid, w_ffn_out_b, 0, mod[l], mod[l + 1], g_norm1[l + 1],
                                 cfg=cfg, rows=rows, final=False)
            x_parts = (xall,)
    return out.reshape(b, t, d)
```

```python
import functools
import math

import jax
import jax.numpy as jnp
from jax import lax
from jax.experimental import pallas as pl
from jax.experimental.pallas import tpu as pltpu

F32 = jnp.float32
BF16 = jnp.bfloat16

HEAD_DIM = 128
Q_PER_KV = 4
GRID_W = 64
ROPE_THETA = 10000.0
NORM_EPS = 1e-6
LRU_C = 8.0
FOURIER_GROUP = 128

LANES = 128
SUBLANES = 8
VMEM_LIMIT_BYTES = 56 * 1024 * 1024

ROW_TILE = 512
IN_PROJ_ROW_TILE = 1024
FFN_UP_ROW_TILE = 1024
FFN_DOWN_COL_BLOCKS = 4
ATTN_Q_TILE = 2048
ATTN_SUB_ROWS = 512
LRU_CHUNK = 256
LRU_GAP = SUBLANES


def _params(*semantics):
    return pltpu.CompilerParams(dimension_semantics=semantics,
                                vmem_limit_bytes=VMEM_LIMIT_BYTES)


def _dot(a, b):
    return jnp.dot(a, b, preferred_element_type=F32)


def _dot_nt(a, b):
    return lax.dot_general(a, b, (((1,), (1,)), ((), ())), preferred_element_type=F32)


def _rms(x):
    return x * lax.rsqrt(jnp.mean(x * x, axis=-1, keepdims=True) + NORM_EPS)


def _ada_norm(x, gain, shift, scale):
    return (_rms(x) * gain * (1.0 + scale) + shift).astype(BF16)


def _mod_kernel(c_ref, w_ref, b_ref, o_ref):
    c = c_ref[...]
    s = (c * jax.nn.sigmoid(c)).astype(BF16)
    o_ref[...] = _dot(s, w_ref[...].astype(BF16)) + b_ref[...]


def _modulation(cc, w_mod, b_mod):
    depth, d, n = w_mod.shape
    rows = cc.shape[0]
    tn = 1024
    return pl.pallas_call(
        _mod_kernel,
        grid=(depth, n // tn),
        in_specs=[
            pl.BlockSpec((rows, d), lambda l, j: (0, 0)),
            pl.BlockSpec((None, d, tn), lambda l, j: (l, 0, j)),
            pl.BlockSpec((None, 1, tn), lambda l, j: (l, 0, j)),
        ],
        out_specs=pl.BlockSpec((None, rows, tn), lambda l, j: (l, 0, j)),
        out_shape=jax.ShapeDtypeStruct((depth, rows, n), F32),
        compiler_params=_params("parallel", "parallel"),
        name="modulation",
    )(cc, w_mod, b_mod.reshape(depth, 1, n))


def _split_row_specs(n_lat_tiles, tm, d):
    return [pl.BlockSpec((tm, d), lambda i: (jnp.minimum(i, n_lat_tiles - 1), 0)),
            pl.BlockSpec((tm, d), lambda i: (jnp.maximum(i - n_lat_tiles, 0), 0))]


def _for_row_source(x_ref, c_ref, n_lat_tiles, fn):
    i = pl.program_id(0)
    pl.when(i < n_lat_tiles)(lambda: fn(x_ref))
    pl.when(i >= n_lat_tiles)(lambda: fn(c_ref))


def _prenorm_kernel(x_ref, c_ref, mod_ref, g_ref, h_ref, *, n_lat_tiles):
    def emit(r):
        h_ref[...] = _ada_norm(r[...], g_ref[...], mod_ref[0:1, :], mod_ref[1:2, :])
    _for_row_source(x_ref, c_ref, n_lat_tiles, emit)


def _prenorm(x2d, ctx2d, mod_l, g_norm, *, cfg):
    d = x2d.shape[1]
    rows = x2d.shape[0] + ctx2d.shape[0]
    tm = ROW_TILE
    n_lat_tiles = x2d.shape[0] // tm
    return pl.pallas_call(
        functools.partial(_prenorm_kernel, n_lat_tiles=n_lat_tiles),
        grid=(rows // tm,),
        in_specs=_split_row_specs(n_lat_tiles, tm, d) + [
            pl.BlockSpec((None, 6, d), lambda i: (cfg["mod_row"](i), 0, 0)),
            pl.BlockSpec((1, d), lambda i: (0, 0)),
        ],
        out_specs=pl.BlockSpec((tm, d), lambda i: (i, 0)),
        out_shape=jax.ShapeDtypeStruct((rows, d), BF16),
        compiler_params=_params("parallel"),
        name="prenorm",
    )(x2d, ctx2d, mod_l, g_norm.reshape(1, d))


def _in_proj_kernel(h_ref, w_ref, cos_ref, sin_ref, qg_ref, kg_ref, qkv_ref, rest_ref, qkv_scr,
                    *, n_q_heads, n_kv_heads):
    j = pl.program_id(1)

    @pl.when(j == 0)
    def _():
        qkv_scr[...] = _dot(h_ref[...], w_ref[...])

    @pl.when(j == 1)
    def _():
        rest_ref[...] = _dot(h_ref[...], w_ref[...]).astype(BF16)
        cos = cos_ref[...]
        sin = sin_ref[...]
        q_gain = qg_ref[...] * (HEAD_DIM ** -0.5 * math.log2(math.e))
        k_gain = kg_ref[...]
        n_rot = n_q_heads + n_kv_heads
        for hh in range(n_rot):
            cols = slice(hh * HEAD_DIM, (hh + 1) * HEAD_DIM)
            y = _rms(qkv_scr[:, cols]) * (q_gain if hh < n_q_heads else k_gain)
            y = y * cos + pltpu.roll(y, HEAD_DIM // 2, axis=1) * sin
            qkv_ref[:, cols] = y.astype(BF16)
        qkv_ref[:, n_rot * HEAD_DIM:] = qkv_scr[:, n_rot * HEAD_DIM:].astype(BF16)

    @pl.when(j > 1)
    def _():
        rest_ref[...] = _dot(h_ref[...], w_ref[...]).astype(BF16)


def _in_proj(h1, w_in, l, cosf, sinf, q_gain, k_gain, *, cfg):
    rows, d = h1.shape
    d_in = w_in.shape[2]
    tm, tn = IN_PROJ_ROW_TILE, cfg["tn_in"]
    assert d_in == 3 * tn
    kern = functools.partial(_in_proj_kernel, n_q_heads=cfg["n_q_heads"],
                             n_kv_heads=cfg["n_kv_heads"])
    return pl.pallas_call(
        kern,
        grid=(rows // tm, d_in // tn),
        in_specs=[
            pl.BlockSpec((tm, d), lambda i, j: (i, 0)),
            pl.BlockSpec((None, d, tn), lambda i, j: (l, 0, j)),
            pl.BlockSpec((tm, HEAD_DIM), lambda i, j: (cfg["rope_block"](i), 0)),
            pl.BlockSpec((tm, HEAD_DIM), lambda i, j: (cfg["rope_block"](i), 0)),
            pl.BlockSpec((1, HEAD_DIM), lambda i, j: (0, 0)),
            pl.BlockSpec((1, HEAD_DIM), lambda i, j: (0, 0)),
        ],
        out_specs=[
            pl.BlockSpec((tm, tn), lambda i, j: (i, 0)),
            pl.BlockSpec((tm, tn), lambda i, j: (i, jnp.maximum(j - 1, 0))),
        ],
        out_shape=[
            jax.ShapeDtypeStruct((rows, tn), BF16),
            jax.ShapeDtypeStruct((rows, d_in - tn), BF16),
        ],
        scratch_shapes=[pltpu.VMEM((tm, tn), F32)],
        compiler_params=_params("parallel", "arbitrary"),
        name="in_proj",
    )(h1, w_in, cosf, sinf, q_gain.reshape(1, HEAD_DIM), k_gain.reshape(1, HEAD_DIM))


def _attn_kernel(q_ref, *refs, latent):
    if latent:
        kl_ref, kc_ref, vl_ref, vc_ref, o_ref, vl_aug, vc_aug = refs
    else:
        kc_ref, vc_ref, o_ref, vc_aug = refs

    @pl.when(pl.program_id(2) == 0)
    def _():
        for v_ref, aug in ((vl_ref, vl_aug), (vc_ref, vc_aug)) if latent else ((vc_ref, vc_aug),):
            lane = lax.broadcasted_iota(jnp.int32, v_ref.shape, 1)
            aug[:, :HEAD_DIM] = v_ref[...]
            aug[:, HEAD_DIM:] = jnp.where(lane == 0, 1.0, 0.0).astype(BF16)

    kc = kc_ref[...]
    sub = min(q_ref.shape[0], ATTN_SUB_ROWS)
    for r0 in range(0, q_ref.shape[0], sub):
        for g in range(Q_PER_KV):
            cols = slice(g * HEAD_DIM, (g + 1) * HEAD_DIM)
            q = q_ref[r0:r0 + sub, cols]
            sc = _dot_nt(q, kc)
            m = jnp.max(sc, axis=-1, keepdims=True)
            if latent:
                sl = _dot_nt(q, kl_ref[...])
                m = jnp.maximum(m, jnp.max(sl, axis=-1, keepdims=True))
            o = _dot(jnp.exp2((sc - m).astype(BF16)), vc_aug[...])
            if latent:
                o = o + _dot(jnp.exp2((sl - m).astype(BF16)), vl_aug[...])
            o_ref[r0:r0 + sub, cols] = (
                o[:, :HEAD_DIM] / o[:, HEAD_DIM:HEAD_DIM + 1]).astype(BF16)


def _attention(qkv, *, cfg, latent):
    b, t, tc = cfg["b"], cfg["t"], cfg["tc"]
    tq = min(ATTN_Q_TILE, t) if latent else tc
    assert t % tq == 0
    n_q_tiles = t // tq if latent else 1
    q_blk0 = 0 if latent else b * t // tc
    n_kv = cfg["n_kv_heads"]
    gw = Q_PER_KV * HEAD_DIM
    k_col = cfg["n_q_heads"]
    v_col = k_col + n_kv
    ctx_blk0 = b * t // tc

    q_spec = pl.BlockSpec((tq, gw), lambda bi, h, qi: (q_blk0 + bi * n_q_tiles + qi, h))
    lat_spec = lambda col: pl.BlockSpec((t, HEAD_DIM), lambda bi, h, qi: (bi, col + h))
    ctx_spec = lambda col: pl.BlockSpec((tc, HEAD_DIM), lambda bi, h, qi: (ctx_blk0 + bi, col + h))
    aug = lambda n: pltpu.VMEM((n, 2 * HEAD_DIM), BF16)
    if latent:
        in_specs = [q_spec, lat_spec(k_col), ctx_spec(k_col), lat_spec(v_col), ctx_spec(v_col)]
        scratch = [aug(t), aug(tc)]
    else:
        in_specs = [q_spec, ctx_spec(k_col), ctx_spec(v_col)]
        scratch = [aug(tc)]
    return pl.pallas_call(
        functools.partial(_attn_kernel, latent=latent),
        grid=(b, n_kv, n_q_tiles),
        in_specs=in_specs,
        out_specs=pl.BlockSpec((tq, gw), lambda bi, h, qi: (bi * n_q_tiles + qi, h)),
        out_shape=jax.ShapeDtypeStruct((b * (t if latent else tc), cfg["n_q_heads"] * HEAD_DIM),
                                       BF16),
        scratch_shapes=scratch,
        compiler_params=_params("parallel", "parallel", "arbitrary"),
        name="attention",
    )(*[qkv] * len(in_specs))


def _lru_kernel(ux_ref, ug_ref, cw_ref, cb_ref, w_ref, br_ref, bi_ref, lam_ref, o_ref,
                u_scr, af_scr, bf_scr, yf_scr, ab_scr, bb_scr, yb_scr, *, b, t, tc):
    ch, gap = LRU_CHUNK, LRU_GAP
    n_lat = b * t
    conv_width = cw_ref.shape[0]
    conv_left = conv_width // 2

    lat = (gap, 0, t + gap, t)
    ctx = (gap + b * (t + gap), n_lat, tc + gap, tc)

    zeros_gap = jnp.zeros((gap, LANES), F32)
    for kind, n in ((lat, t), (ctx, tc)):
        for bi in range(b):
            u0 = kind[0] + bi * kind[2]
            r0 = kind[1] + bi * kind[3]
            u_scr[u0 - gap:u0, :] = zeros_gap
            u_scr[u0:u0 + n, :] = ux_ref[r0:r0 + n, :].astype(F32)
    u_end = ctx[0] + b * ctx[2] - gap
    u_scr[u_end:u_end + gap, :] = zeros_gap

    def conv_chunk(s0):
        u = cb_ref[...]
        for j in range(conv_width):
            tap0 = s0 - conv_left + j
            u = u + cw_ref[j:j + 1, :] * u_scr[tap0:tap0 + ch, :]
        return u

    for kind, n in ((lat, t), (ctx, tc)):
        for bi in range(b):
            u0 = kind[0] + bi * kind[2]
            held = None
            for c in range(n // ch):
                cur = conv_chunk(u0 + c * ch)
                if held is not None:
                    u_scr[u0 + (c - 1) * ch:u0 + c * ch, :] = held
                held = cur
            u_scr[u0 + n - ch:u0 + n, :] = held

    def slab_rows(kind, bi, off):
        start = kind[1] + bi * kind[3] + off
        if not isinstance(start, int):
            start = pl.multiple_of(start, ch)
        return pl.ds(start, ch)

    def coefficients(d, kind, off, a_scr, b_scr):
        lam = lam_ref[d:d + 1, :]
        softplus_neg_lam = jnp.maximum(-lam, 0.0) + jnp.log1p(jnp.exp(-jnp.abs(lam)))
        half_rate = -0.5 * LRU_C * softplus_neg_lam
        w_half = w_ref[:, d * 2 * LANES:(d + 1) * 2 * LANES] * 0.5
        br_half = 0.5 * br_ref[d:d + 1, :]
        bi_half = 0.5 * bi_ref[d:d + 1, :]
        for bi in range(b):
            s0 = kind[0] + bi * kind[2] + off
            if not isinstance(s0, int):
                s0 = pl.multiple_of(s0, SUBLANES)
            u = u_scr[pl.ds(s0, ch), :]
            gates = _dot(u.astype(BF16), w_half)
            log_a = jnp.tanh(gates[:, :LANES] + br_half) * half_rate + half_rate
            i_gate = 0.5 * jnp.tanh(gates[:, LANES:] + bi_half) + 0.5
            a = jnp.exp(log_a)
            one_minus_a2 = -jnp.tanh(log_a) * (a * a + 1.0)
            root = jnp.where(one_minus_a2 > 0.0, one_minus_a2 * lax.rsqrt(one_minus_a2), 0.0)
            a_scr[pl.ds(bi, ch, stride=b), :] = a
            b_scr[pl.ds(bi, ch, stride=b), :] = root * i_gate * u

    def emit(rows, y):
        gate = jax.nn.gelu(ug_ref[rows, :].astype(F32), approximate=True)
        o_ref[rows, :] = (y * gate).astype(BF16)

    def pair(kind, f_off, b_off, carry, mode):
        coefficients(0, kind, f_off, af_scr, bf_scr)
        coefficients(1, kind, b_off, ab_scr, bb_scr)

        def two_steps(h, a_scr, b_scr, y_scr, s0, s1):
            r0 = pl.ds(pl.multiple_of(s0 * b, b), b)
            r1 = pl.ds(pl.multiple_of(s1 * b, b), b)
            a0, b0, a1, b1 = a_scr[r0, :], b_scr[r0, :], a_scr[r1, :], b_scr[r1, :]
            y_scr[r0, :] = a0 * h + b0
            h = (a1 * a0) * h + (a1 * b0 + b1)
            y_scr[r1, :] = h
            return h

        def step(s, hs):
            hf, hb = hs
            hf = two_steps(hf, af_scr, bf_scr, yf_scr, 2 * s, 2 * s + 1)
            hb = two_steps(hb, ab_scr, bb_scr, yb_scr, ch - 1 - 2 * s, ch - 2 - 2 * s)
            return hf, hb

        carry = lax.fori_loop(0, ch // 2, step, carry, unroll=16)

        for bi in range(b):
            blk = pl.ds(bi, ch, stride=b)
            rf = slab_rows(kind, bi, f_off)
            rb = slab_rows(kind, bi, b_off)
            if mode == "same":
                emit(rf, yf_scr[blk, :] + yb_scr[blk, :])
            elif mode == "first":
                o_ref[rf, :] = yf_scr[blk, :].astype(BF16)
                o_ref[rb, :] = yb_scr[blk, :].astype(BF16)
            else:
                emit(rf, o_ref[rf, :].astype(F32) + yf_scr[blk, :])
                emit(rb, o_ref[rb, :].astype(F32) + yb_scr[blk, :])
        return carry

    n_lat_chunks = t // ch
    half = n_lat_chunks // 2
    zero = jnp.zeros((b, LANES), F32)
    carry = pair(ctx, 0, 0, (zero, zero), "same")
    carry = lax.fori_loop(
        0, half,
        lambda q, hs: pair(lat, q * ch, (n_lat_chunks - 1 - q) * ch, hs, "first"), carry)
    lax.fori_loop(
        0, half,
        lambda q, hs: pair(lat, (half + q) * ch, (half - 1 - q) * ch, hs, "second"), carry)


def _lru(rest, conv_w, conv_b, w_bd, b_r, b_i, lam, *, cfg):
    rows = rest.shape[0]
    d_lru = conv_w.shape[1]
    ux_blk = cfg["ux_col"] // LANES
    ug_blk = cfg["ug_col"] // LANES
    nb = cfg["b"]
    assert cfg["tc"] == LRU_CHUNK and (cfg["t"] // LRU_CHUNK) % 2 == 0
    kern = functools.partial(_lru_kernel, b=nb, t=cfg["t"], tc=cfg["tc"])
    vec = lambda n: pl.BlockSpec((n, LANES), lambda c: (0, c))
    return pl.pallas_call(
        kern,
        grid=(d_lru // LANES,),
        in_specs=[
            pl.BlockSpec((rows, LANES), lambda c: (0, ux_blk + c)),
            pl.BlockSpec((rows, LANES), lambda c: (0, ug_blk + c)),
            vec(conv_w.shape[0]), vec(1),
            pl.BlockSpec((None, LANES, 4 * LANES), lambda c: (c, 0, 0)),
            vec(2), vec(2), vec(2),
        ],
        out_specs=pl.BlockSpec((rows, LANES), lambda c: (0, c)),
        out_shape=jax.ShapeDtypeStruct((rows, d_lru), BF16),
        scratch_shapes=[
            pltpu.VMEM((rows + (2 * nb + 1) * LRU_GAP, LANES), F32),
        ] + [pltpu.VMEM((nb * LRU_CHUNK, LANES), F32)] * 6,
        compiler_params=_params("parallel"),
        name="rglru",
    )(rest, rest, conv_w, conv_b.reshape(1, d_lru), w_bd, b_r, b_i, lam)


def _lru_block_diag(w_r, w_i):
    _, nb, bw, _ = w_r.shape
    per = LANES // bw
    eye = jnp.eye(per, dtype=w_r.dtype)

    def bd(w):
        w = w.reshape(nb // per, per, bw, bw)
        return jnp.einsum("cipq,ij->cipjq", w, eye).reshape(nb // per, LANES, LANES)

    return jnp.concatenate([bd(w_r[0]), bd(w_i[0]), bd(w_r[1]), bd(w_i[1])], axis=-1).astype(BF16)


def _fourier_kernel(u_ref, cst_ref, rev_ref, w2_ref, o_ref, p_scr, q_scr, *, t, n_groups, scale):
    half = t // 2
    for g in range(n_groups):
        cols = slice(g * FOURIER_GROUP, (g + 1) * FOURIER_GROUP)
        pq = _dot(u_ref[:, cols], w2_ref[...])
        p_scr[:, cols] = pq[:, :FOURIER_GROUP].astype(BF16)
        q_scr[:, cols] = pq[:, FOURIER_GROUP:].astype(BF16)
    a = _dot(cst_ref[:, :t], p_scr[...])
    bs = _dot(cst_ref[:, t:], q_scr[...])
    o_ref[0:half, :] = ((a - bs) * scale).astype(BF16)
    mirrored = _dot(rev_ref[...], ((a + bs) * scale).astype(BF16))
    p = p_scr[...].astype(F32)
    odd_row = (lax.broadcasted_iota(jnp.int32, p.shape, 0) & 1) == 1
    mid = jnp.sum(jnp.where(odd_row, -p, p), axis=0, keepdims=True) * scale
    first_row = lax.broadcasted_iota(jnp.int32, mirrored.shape, 0) == 0
    o_ref[half:t, :] = jnp.where(first_row, mid, mirrored).astype(BF16)


def _dft_tables(t):
    half = t // 2
    k = jnp.arange(half, dtype=jnp.int32)[:, None]
    s = jnp.arange(t, dtype=jnp.int32)[None, :]
    at = ((k * s) % t).astype(F32) * (2.0 * math.pi / t)
    cst = jnp.concatenate([jnp.cos(at), jnp.sin(at)], axis=1).astype(BF16)
    r = jnp.arange(half, dtype=jnp.int32)
    rev = (r[:, None] + r[None, :] == half).astype(BF16)
    c = jnp.arange(FOURIER_GROUP, dtype=jnp.int32)
    ac = ((c[:, None] * c[None, :]) % FOURIER_GROUP).astype(F32) * (2.0 * math.pi / FOURIER_GROUP)
    w2 = jnp.concatenate([jnp.cos(ac), jnp.sin(ac)], axis=1).astype(BF16)
    return cst, rev, w2


def _fourier(rest, cst, rev, w2, *, cfg, seq_len, row_blk0):
    d_f = cfg["d_branch"]
    wcols = min(d_f, 512)
    uf_blk = cfg["uf_col"] // wcols
    kern = functools.partial(_fourier_kernel, t=seq_len, n_groups=wcols // FOURIER_GROUP,
                             scale=(seq_len * FOURIER_GROUP) ** -0.5)
    const = lambda shape: pl.BlockSpec(shape, lambda bi, hf: (0, 0), pipeline_mode=pl.Buffered(1))
    return pl.pallas_call(
        kern,
        grid=(cfg["b"], d_f // wcols),
        in_specs=[
            pl.BlockSpec((seq_len, wcols), lambda bi, hf: (row_blk0 + bi, uf_blk + hf)),
            const(cst.shape), const(rev.shape), const(w2.shape),
        ],
        out_specs=pl.BlockSpec((seq_len, wcols), lambda bi, hf: (bi, hf)),
        out_shape=jax.ShapeDtypeStruct((cfg["b"] * seq_len, d_f), BF16),
        scratch_shapes=[pltpu.VMEM((seq_len, wcols), BF16)] * 2,
        compiler_params=_params("parallel", "parallel"),
        name="fourier",
    )(rest, cst, rev, w2)


def _merge_kernel(*refs, n_parts, n_lat_tiles):
    h_ref = refs[0]
    att_refs = refs[1:1 + n_parts]
    rec_ref = refs[1 + n_parts]
    fou_refs = refs[2 + n_parts:2 + 2 * n_parts]
    wgs, bgs, wbs = (refs[2 + 2 * n_parts + 3 * k:5 + 2 * n_parts + 3 * k] for k in range(3))
    o_ref = refs[-1]
    is_latent = pl.program_id(0) < n_lat_tiles

    def rows_of(parts):
        if len(parts) == 1:
            return parts[0][...]
        return jnp.where(is_latent, parts[0][...], parts[1][...])

    h = h_ref[...]
    acc = None
    for y, wg, bg, wb in zip((rows_of(att_refs), rec_ref[...], rows_of(fou_refs)), wgs, bgs, wbs):
        gate = jax.nn.sigmoid(_dot(h, wg[...]) + bg[...])
        term = gate * _dot(y, wb[...])
        acc = term if acc is None else acc + term
    o_ref[...] = acc.astype(BF16)


def _merge(h, att_parts, rec, fou_parts, w_gate, b_gate, w_branch, l, *, rows):
    d = h.shape[1]
    d_b = rec.shape[1]
    tm, tn = ROW_TILE, 512
    nj = d // tn
    n_parts = len(att_parts)
    n_lat_tiles = att_parts[0].shape[0] // tm
    y_spec = pl.BlockSpec((tm, d_b), lambda i, j: (i, 0))
    part_specs = [pl.BlockSpec((tm, d_b), lambda i, j: (jnp.minimum(i, n_lat_tiles - 1), 0)),
                  pl.BlockSpec((tm, d_b), lambda i, j: (jnp.maximum(i - n_lat_tiles, 0), 0))
                  ][:n_parts]
    wg_spec = lambda k: pl.BlockSpec((None, d, tn), lambda i, j: (l, 0, k * nj + j))
    bg_spec = lambda k: pl.BlockSpec((None, 1, tn), lambda i, j: (l, 0, k * nj + j))
    wb_spec = lambda k: pl.BlockSpec((None, None, d_b, tn), lambda i, j: (l, k, 0, j))
    bg = b_gate.reshape(b_gate.shape[0], 1, -1)
    return pl.pallas_call(
        functools.partial(_merge_kernel, n_parts=n_parts, n_lat_tiles=n_lat_tiles),
        grid=(rows // tm, nj),
        in_specs=[pl.BlockSpec((tm, d), lambda i, j: (i, 0))] + part_specs + [y_spec] + part_specs
                 + [wg_spec(0), wg_spec(1), wg_spec(2), bg_spec(0), bg_spec(1), bg_spec(2),
                    wb_spec(0), wb_spec(1), wb_spec(2)],
        out_specs=pl.BlockSpec((tm, tn), lambda i, j: (i, j)),
        out_shape=jax.ShapeDtypeStruct((rows, d), BF16),
        compiler_params=_params("parallel", "arbitrary"),
        name="gated_merge",
    )(h, *att_parts, rec, *fou_parts, w_gate, w_gate, w_gate, bg, bg, bg,
      w_branch, w_branch, w_branch)


def _out_proj_kernel(*refs, n_lat_tiles):
    x_refs, (m_ref, w_ref, mod_ref, g_ref, o_ref, h_ref) = refs[:-6], refs[-6:]
    delta = mod_ref[2:3, :] * _dot(m_ref[...], w_ref[...])

    def emit(r):
        y = r[...] + delta
        o_ref[...] = y
        h_ref[...] = _ada_norm(y, g_ref[...], mod_ref[3:4, :], mod_ref[4:5, :])

    if len(x_refs) == 2:
        _for_row_source(*x_refs, n_lat_tiles, emit)
    else:
        emit(x_refs[0])


def _out_proj(x_parts, m, w_out, l, mod_l, g_norm2, *, cfg, rows):
    d = m.shape[1]
    tm = ROW_TILE
    row_spec = pl.BlockSpec((tm, d), lambda i: (i, 0))
    n_lat_tiles = x_parts[0].shape[0] // tm
    x_specs = _split_row_specs(n_lat_tiles, tm, d) if len(x_parts) == 2 else [row_spec]
    return pl.pallas_call(
        functools.partial(_out_proj_kernel, n_lat_tiles=n_lat_tiles),
        grid=(rows // tm,),
        in_specs=x_specs + [
            row_spec,
            pl.BlockSpec((None, d, d), lambda i: (l, 0, 0), pipeline_mode=pl.Buffered(1)),
            pl.BlockSpec((None, 6, d), lambda i: (cfg["mod_row"](i), 0, 0)),
            pl.BlockSpec((1, d), lambda i: (0, 0)),
        ],
        out_specs=[row_spec, row_spec],
        out_shape=[jax.ShapeDtypeStruct((rows, d), F32), jax.ShapeDtypeStruct((rows, d), BF16)],
        compiler_params=_params("parallel"),
        name="out_proj",
    )(*x_parts, m, w_out, mod_l, g_norm2.reshape(1, d))


def _ffn_up_kernel(h_ref, wg_ref, wu_ref, *refs):
    n_cast = (len(refs) - 1) // 2
    a_ref = refs[n_cast]
    h = h_ref[...]
    gt = _dot(h, wg_ref[...])
    up = _dot(h, wu_ref[...])
    a_ref[...] = (gt * jax.nn.sigmoid(gt) * up).astype(BF16)
    for src, dst in zip(refs[:n_cast], refs[n_cast + 1:]):
        if len(dst.shape) == 2:
            dst[...] = src[...].astype(BF16)
        else:
            for g in range(dst.shape[0]):
                dst[g] = src[:, g * dst.shape[2]:(g + 1) * dst.shape[2]].astype(BF16)


def _col_blocked(w, n_groups):
    l, r, c = w.shape
    return w.reshape(l, r, n_groups, c // n_groups).transpose(0, 2, 1, 3)


def _ffn_up(h2, w_ffn_in, l, *, cfg, rows, cast_srcs=(), cast_layer=None):
    d = h2.shape[1]
    d_ff = w_ffn_in.shape[2] // 2
    tm, tf = FFN_UP_ROW_TILE, cfg["tf"]
    nk = d_ff // tf
    n_steps = (rows // tm) * nk
    cast_in, cast_out, cast_shape = [], [], []
    for w, groups in cast_srcs:
        _, r, c = w.shape
        br = next(x for x in (16, 32, 64, 128, 256) if r % x == 0 and r // x <= n_steps)
        blk = lambda i, k, n=r // br: jnp.minimum(i * nk + k, n - 1)
        cast_in.append(pl.BlockSpec((None, br, c), lambda i, k, blk=blk: (cast_layer, blk(i, k), 0)))
        if groups == 1:
            cast_out.append(pl.BlockSpec((None, br, c), lambda i, k, blk=blk: (0, blk(i, k), 0)))
            cast_shape.append(jax.ShapeDtypeStruct((1, r, c), BF16))
        else:
            cast_out.append(pl.BlockSpec((None, groups, br, c // groups),
                                         lambda i, k, blk=blk: (0, 0, blk(i, k), 0)))
            cast_shape.append(jax.ShapeDtypeStruct((1, groups, r, c // groups), BF16))
    return pl.pallas_call(
        _ffn_up_kernel,
        grid=(rows // tm, nk),
        in_specs=[
            pl.BlockSpec((tm, d), lambda i, k: (i, 0)),
            pl.BlockSpec((None, d, tf), lambda i, k: (l, 0, k)),
            pl.BlockSpec((None, d, tf), lambda i, k: (l, 0, nk + k)),
        ] + cast_in,
        out_specs=[pl.BlockSpec((tm, tf), lambda i, k: (i, k))] + cast_out,
        out_shape=[jax.ShapeDtypeStruct((rows, d_ff), BF16)] + cast_shape,
        compiler_params=_params("arbitrary", "arbitrary"),
        name="ffn_up",
    )(h2, w_ffn_in, w_ffn_in, *[w for w, _ in cast_srcs])


def _ffn_down_kernel(a_ref, x_ref, wo_ref, mod_ref, nmod_ref, g_ref, *refs, nj, d, final):
    out_refs, y_scr = refs[:-1], refs[-1]
    i, j = pl.program_id(0), pl.program_id(1)
    cur = i % 2

    @pl.when((i == 0) & (j == 0))
    def _():
        y_scr[1] = jnp.zeros(y_scr.shape[1:], F32)

    prv = y_scr.at[1 - cur]
    ssq = sum(jnp.sum(prv[jj] * prv[jj], axis=-1, keepdims=True) for jj in range(nj))
    normed = prv[j] * lax.rsqrt(ssq * (1.0 / d) + NORM_EPS) * g_ref[...]
    if final:
        out_refs[0][...] = normed
    else:
        out_refs[1][...] = (normed * (1.0 + nmod_ref[1:2, :]) + nmod_ref[0:1, :]).astype(BF16)

    jw = jnp.where(i == pl.num_programs(0) - 1, nj - 1, j)
    y = x_ref[...] + mod_ref[5:6, :] * _dot(a_ref[...], wo_ref[jw])
    if not final:
        out_refs[0][...] = y
    y_scr[cur, j] = y


def _ffn_down(a, xmid, w_ffn_out, l, mod_l, next_mod, gain, *, cfg, rows, final):
    d = xmid.shape[1]
    d_ff = a.shape[1]
    tm = ROW_TILE
    _, nj, _, tn = w_ffn_out.shape
    n_tiles = rows // tm
    cur = lambda i: jnp.minimum(i, n_tiles - 1)
    prev = lambda i: jnp.maximum(i - 1, 0)
    cur_j = lambda i, j: jnp.where(i == n_tiles, nj - 1, j)
    prev_j = lambda i, j: jnp.where(i == 0, 0, j)
    cur_spec = pl.BlockSpec((tm, tn), lambda i, j: (cur(i), cur_j(i, j)))
    prev_spec = pl.BlockSpec((tm, tn), lambda i, j: (prev(i), prev_j(i, j)))
    if final:
        out_specs, out_shape = [prev_spec], [jax.ShapeDtypeStruct((rows, d), F32)]
    else:
        out_specs = [cur_spec, prev_spec]
        out_shape = [jax.ShapeDtypeStruct((rows, d), F32), jax.ShapeDtypeStruct((rows, d), BF16)]
    return pl.pallas_call(
        functools.partial(_ffn_down_kernel, nj=nj, d=d, final=final),
        grid=(n_tiles + 1, nj),
        in_specs=[
            pl.BlockSpec((tm, d_ff), lambda i, j: (cur(i), 0)),
            cur_spec,
            pl.BlockSpec((None, nj, d_ff, tn), lambda i, j: (l, 0, 0, 0),
                         pipeline_mode=pl.Buffered(1)),
            pl.BlockSpec((None, 6, tn), lambda i, j: (cfg["mod_row"](cur(i)), 0, cur_j(i, j))),
            pl.BlockSpec((None, 6, tn), lambda i, j: (cfg["mod_row"](prev(i)), 0, j)),
            pl.BlockSpec((1, tn), lambda i, j: (0, j)),
        ],
        out_specs=out_specs,
        out_shape=out_shape,
        scratch_shapes=[pltpu.VMEM((2, nj, tm, tn), F32)],
        compiler_params=_params("arbitrary", "arbitrary"),
        name="ffn_down",
    )(a, xmid, w_ffn_out, mod_l, next_mod, gain.reshape(1, d))


def _rope_tables(t, tile_rows):
    pairs = HEAD_DIM // 4
    rows = t // GRID_W
    row = jnp.repeat(jnp.arange(rows, dtype=F32), GRID_W)
    col = jnp.tile(jnp.arange(GRID_W, dtype=F32), rows)
    inv = ROPE_THETA ** (-jnp.arange(pairs, dtype=F32) / pairs)
    ang = jnp.concatenate([row[:, None] * inv, col[:, None] * inv], axis=-1)
    cos, sin = jnp.cos(ang), jnp.sin(ang)
    cosf = jnp.concatenate([cos, cos], axis=-1)
    sinf = jnp.concatenate([-sin, sin], axis=-1)
    cosf = jnp.concatenate([cosf, jnp.ones((tile_rows, HEAD_DIM), F32)], axis=0)
    sinf = jnp.concatenate([sinf, jnp.zeros((tile_rows, HEAD_DIM), F32)], axis=0)
    return cosf, sinf


def kernel(x, c, ctx, c_ctx, w_mod, b_mod, g_norm1, g_norm2, w_in, q_gain, k_gain, conv_w, conv_b, lru_w_r, lru_b_r, lru_w_i, lru_b_i, lru_lambda, w_branch, w_gate, b_gate, w_out, w_ffn_in, w_ffn_out, g_final):
    b, t, d = x.shape
    tc = ctx.shape[1]
    depth = w_in.shape[0]
    d_branch = w_branch.shape[2]
    n_q_heads = d_branch // HEAD_DIM
    n_kv_heads = n_q_heads // Q_PER_KV
    d_kv = n_kv_heads * HEAD_DIM
    d_ff = w_ffn_out.shape[1]
    n_lat, n_ctx = b * t, b * tc
    assert b == SUBLANES and t % IN_PROJ_ROW_TILE == 0 and n_ctx % FFN_UP_ROW_TILE == 0
    assert tc % LRU_CHUNK == 0 and t % LRU_CHUNK == 0
    tiles_per_batch = t // ROW_TILE

    cfg = dict(
        b=b, t=t, tc=tc, d_branch=d_branch, n_q_heads=n_q_heads, n_kv_heads=n_kv_heads,
        tn_in=d_branch + 2 * d_kv,
        ux_col=0, ug_col=d_branch, uf_col=2 * d_branch,
        tf=512 if d_ff % 512 == 0 else 256,
        mod_row=lambda i: jnp.minimum(i // tiles_per_batch, b),
        rope_block=lambda i: jnp.where(i < n_lat // IN_PROJ_ROW_TILE, i % (t // IN_PROJ_ROW_TILE),
                                       t // IN_PROJ_ROW_TILE),
    )

    x_parts = (x.reshape(n_lat, d), ctx.reshape(n_ctx, d))
    cc = jnp.concatenate([c, c_ctx[None, :], jnp.zeros((2 * SUBLANES - b - 1, d), F32)], axis=0)
    mod = _modulation(cc, w_mod, b_mod).reshape(depth, 2 * SUBLANES, 6, d)

    w_f32 = ((w_in, 1), (w_gate, 1), (w_branch.reshape(depth, -1, d), 1), (w_out, 1),
             (w_ffn_in, 1), (w_ffn_out, FFN_DOWN_COL_BLOCKS))
    wb = [(w[0:1] if g == 1 else _col_blocked(w[0:1], g)).astype(BF16) for w, g in w_f32]

    cosf, sinf = _rope_tables(t, IN_PROJ_ROW_TILE)
    dft_lat = _dft_tables(t)
    dft_ctx = _dft_tables(tc)

    h1 = _prenorm(*x_parts, mod[0], g_norm1[0], cfg=cfg)
    out = None
    for l in range(depth):
        last = l == depth - 1
        rows = n_lat if last else n_lat + n_ctx
        w_in_b, w_gate_b, w_branch_b, w_out_b, w_ffn_in_b, w_ffn_out_b = wb
        w_branch_b = w_branch_b.reshape((1,) + w_branch.shape[1:])
        qkv, rest = _in_proj(h1, w_in_b, 0, cosf, sinf, q_gain[l], k_gain[l], cfg=cfg)
        att = (_attention(qkv, cfg=cfg, latent=True),)
        rec = _lru(rest, conv_w[l], conv_b[l], _lru_block_diag(lru_w_r[l], lru_w_i[l]),
                   lru_b_r[l], lru_b_i[l], lru_lambda[l], cfg=cfg)
        fou = (_fourier(rest, *dft_lat, cfg=cfg, seq_len=t, row_blk0=0),)
        if not last:
            att += (_attention(qkv, cfg=cfg, latent=False),)
            fou += (_fourier(rest, *dft_ctx, cfg=cfg, seq_len=tc, row_blk0=n_lat // tc),)
        m = _merge(h1, att, rec, fou, w_gate_b, b_gate[l:l + 1], w_branch_b, 0, rows=rows)
        xmid, h2 = _out_proj(x_parts, m, w_out_b, 0, mod[l], g_norm2[l], cfg=cfg, rows=rows)
        if last:
            (a,) = _ffn_up(h2, w_ffn_in_b, 0, cfg=cfg, rows=rows)
            (out,) = _ffn_down(a, xmid, w_ffn_out_b, 0, mod[l], mod[l], g_final,
                               cfg=cfg, rows=rows, final=True)
        else:
            a, *wb = _ffn_up(h2, w_ffn_in_b, 0, cfg=cfg, rows=rows, cast_srcs=w_f32,
                             cast_layer=l + 1)
            xall, h1 = _ffn_down(a, xmid, w_ffn_out_b, 0, mod[l], mod[l + 1], g_norm1[l + 1],
                                 cfg=cfg, rows=rows, final=False)
            x_parts = (xall,)
    return out.reshape(b, t, d)
```

```python
import functools
import math

import jax
import jax.numpy as jnp
from jax import lax
from jax.experimental import pallas as pl
from jax.experimental.pallas import tpu as pltpu

F32 = jnp.float32
BF16 = jnp.bfloat16

HEAD_DIM = 128
Q_PER_KV = 4
GRID_W = 64
ROPE_THETA = 10000.0
NORM_EPS = 1e-6
LRU_C = 8.0
FOURIER_GROUP = 128

LANES = 128
SUBLANES = 8
VMEM_LIMIT_BYTES = 56 * 1024 * 1024

ROW_TILE = 512
IN_PROJ_ROW_TILE = 1024
MERGE_ROW_TILE = 1024
FFN_UP_ROW_TILE = 1024
FFN_DOWN_COL_BLOCKS = 4
ATTN_Q_TILE = 2048
ATTN_SUB_ROWS = 512
LRU_CHUNK = 256
LRU_GAP = SUBLANES


def _params(*semantics):
    return pltpu.CompilerParams(dimension_semantics=semantics,
                                vmem_limit_bytes=VMEM_LIMIT_BYTES)


def _dot(a, b):
    return jnp.dot(a, b, preferred_element_type=F32)


def _dot_nt(a, b):
    return lax.dot_general(a, b, (((1,), (1,)), ((), ())), preferred_element_type=F32)


def _rms(x):
    return x * lax.rsqrt(jnp.mean(x * x, axis=-1, keepdims=True) + NORM_EPS)


def _ada_norm(x, gain, shift, scale):
    return (_rms(x) * gain * (1.0 + scale) + shift).astype(BF16)


def _mod_kernel(c_ref, w_ref, b_ref, o_ref):
    c = c_ref[...]
    s = (c * jax.nn.sigmoid(c)).astype(BF16)
    o_ref[...] = _dot(s, w_ref[...].astype(BF16)) + b_ref[...]


def _modulation(cc, w_mod, b_mod):
    depth, d, n = w_mod.shape
    rows = cc.shape[0]
    tn = 1024
    return pl.pallas_call(
        _mod_kernel,
        grid=(depth, n // tn),
        in_specs=[
            pl.BlockSpec((rows, d), lambda l, j: (0, 0)),
            pl.BlockSpec((None, d, tn), lambda l, j: (l, 0, j)),
            pl.BlockSpec((None, 1, tn), lambda l, j: (l, 0, j)),
        ],
        out_specs=pl.BlockSpec((None, rows, tn), lambda l, j: (l, 0, j)),
        out_shape=jax.ShapeDtypeStruct((depth, rows, n), F32),
        compiler_params=_params("parallel", "parallel"),
        name="modulation",
    )(cc, w_mod, b_mod.reshape(depth, 1, n))


def _split_row_specs(n_lat_tiles, tm, d):
    return [pl.BlockSpec((tm, d), lambda i: (jnp.minimum(i, n_lat_tiles - 1), 0)),
            pl.BlockSpec((tm, d), lambda i: (jnp.maximum(i - n_lat_tiles, 0), 0))]


def _for_row_source(x_ref, c_ref, n_lat_tiles, fn):
    i = pl.program_id(0)
    pl.when(i < n_lat_tiles)(lambda: fn(x_ref))
    pl.when(i >= n_lat_tiles)(lambda: fn(c_ref))


def _prenorm_kernel(x_ref, c_ref, mod_ref, g_ref, h_ref, *, n_lat_tiles):
    def emit(r):
        h_ref[...] = _ada_norm(r[...], g_ref[...], mod_ref[0:1, :], mod_ref[1:2, :])
    _for_row_source(x_ref, c_ref, n_lat_tiles, emit)


def _prenorm(x2d, ctx2d, mod_l, g_norm, *, cfg):
    d = x2d.shape[1]
    rows = x2d.shape[0] + ctx2d.shape[0]
    tm = ROW_TILE
    n_lat_tiles = x2d.shape[0] // tm
    return pl.pallas_call(
        functools.partial(_prenorm_kernel, n_lat_tiles=n_lat_tiles),
        grid=(rows // tm,),
        in_specs=_split_row_specs(n_lat_tiles, tm, d) + [
            pl.BlockSpec((None, 6, d), lambda i: (cfg["mod_row"](i), 0, 0)),
            pl.BlockSpec((1, d), lambda i: (0, 0)),
        ],
        out_specs=pl.BlockSpec((tm, d), lambda i: (i, 0)),
        out_shape=jax.ShapeDtypeStruct((rows, d), BF16),
        compiler_params=_params("parallel"),
        name="prenorm",
    )(x2d, ctx2d, mod_l, g_norm.reshape(1, d))


def _in_proj_kernel(h_ref, w_ref, cos_ref, sin_ref, qg_ref, kg_ref, qkv_ref, rest_ref, qkv_scr,
                    *, n_q_heads, n_kv_heads):
    j = pl.program_id(1)

    @pl.when(j == 0)
    def _():
        qkv_scr[...] = _dot(h_ref[...], w_ref[...])

    @pl.when(j == 1)
    def _():
        rest_ref[...] = _dot(h_ref[...], w_ref[...]).astype(BF16)
        cos = cos_ref[...]
        sin = sin_ref[...]
        q_gain = qg_ref[...] * (HEAD_DIM ** -0.5 * math.log2(math.e))
        k_gain = kg_ref[...]
        n_rot = n_q_heads + n_kv_heads
        for hh in range(n_rot):
            cols = slice(hh * HEAD_DIM, (hh + 1) * HEAD_DIM)
            y = _rms(qkv_scr[:, cols]) * (q_gain if hh < n_q_heads else k_gain)
            y = y * cos + pltpu.roll(y, HEAD_DIM // 2, axis=1) * sin
            qkv_ref[:, cols] = y.astype(BF16)
        qkv_ref[:, n_rot * HEAD_DIM:] = qkv_scr[:, n_rot * HEAD_DIM:].astype(BF16)

    @pl.when(j > 1)
    def _():
        rest_ref[...] = _dot(h_ref[...], w_ref[...]).astype(BF16)


def _in_proj(h1, w_in, l, cosf, sinf, q_gain, k_gain, *, cfg):
    rows, d = h1.shape
    d_in = w_in.shape[2]
    tm, tn = IN_PROJ_ROW_TILE, cfg["tn_in"]
    assert d_in == 3 * tn
    kern = functools.partial(_in_proj_kernel, n_q_heads=cfg["n_q_heads"],
                             n_kv_heads=cfg["n_kv_heads"])
    return pl.pallas_call(
        kern,
        grid=(rows // tm, d_in // tn),
        in_specs=[
            pl.BlockSpec((tm, d), lambda i, j: (i, 0)),
            pl.BlockSpec((None, d, tn), lambda i, j: (l, 0, j)),
            pl.BlockSpec((tm, HEAD_DIM), lambda i, j: (cfg["rope_block"](i), 0)),
            pl.BlockSpec((tm, HEAD_DIM), lambda i, j: (cfg["rope_block"](i), 0)),
            pl.BlockSpec((1, HEAD_DIM), lambda i, j: (0, 0)),
            pl.BlockSpec((1, HEAD_DIM), lambda i, j: (0, 0)),
        ],
        out_specs=[
            pl.BlockSpec((tm, tn), lambda i, j: (i, 0)),
            pl.BlockSpec((tm, tn), lambda i, j: (i, jnp.maximum(j - 1, 0))),
        ],
        out_shape=[
            jax.ShapeDtypeStruct((rows, tn), BF16),
            jax.ShapeDtypeStruct((rows, d_in - tn), BF16),
        ],
        scratch_shapes=[pltpu.VMEM((tm, tn), F32)],
        compiler_params=_params("parallel", "arbitrary"),
        name="in_proj",
    )(h1, w_in, cosf, sinf, q_gain.reshape(1, HEAD_DIM), k_gain.reshape(1, HEAD_DIM))


def _attn_kernel(q_ref, *refs, latent):
    if latent:
        kl_ref, kc_ref, vl_ref, vc_ref, o_ref, vl_aug, vc_aug = refs
    else:
        kc_ref, vc_ref, o_ref, vc_aug = refs

    @pl.when(pl.program_id(2) == 0)
    def _():
        for v_ref, aug in ((vl_ref, vl_aug), (vc_ref, vc_aug)) if latent else ((vc_ref, vc_aug),):
            lane = lax.broadcasted_iota(jnp.int32, v_ref.shape, 1)
            aug[:, :HEAD_DIM] = v_ref[...]
            aug[:, HEAD_DIM:] = jnp.where(lane == 0, 1.0, 0.0).astype(BF16)

    kc = kc_ref[...]
    sub = min(q_ref.shape[0], ATTN_SUB_ROWS)
    for r0 in range(0, q_ref.shape[0], sub):
        for g in range(Q_PER_KV):
            cols = slice(g * HEAD_DIM, (g + 1) * HEAD_DIM)
            q = q_ref[r0:r0 + sub, cols]
            sc = _dot_nt(q, kc)
            m = jnp.max(sc, axis=-1, keepdims=True)
            if latent:
                sl = _dot_nt(q, kl_ref[...])
                m = jnp.maximum(m, jnp.max(sl, axis=-1, keepdims=True))
            o = _dot(jnp.exp2((sc - m).astype(BF16)), vc_aug[...])
            if latent:
                o = o + _dot(jnp.exp2((sl - m).astype(BF16)), vl_aug[...])
            o_ref[r0:r0 + sub, cols] = (
                o[:, :HEAD_DIM] / o[:, HEAD_DIM:HEAD_DIM + 1]).astype(BF16)


def _attention(qkv, *, cfg, latent):
    b, t, tc = cfg["b"], cfg["t"], cfg["tc"]
    tq = min(ATTN_Q_TILE, t) if latent else tc
    assert t % tq == 0
    n_q_tiles = t // tq if latent else 1
    q_blk0 = 0 if latent else b * t // tc
    n_kv = cfg["n_kv_heads"]
    gw = Q_PER_KV * HEAD_DIM
    k_col = cfg["n_q_heads"]
    v_col = k_col + n_kv
    ctx_blk0 = b * t // tc

    q_spec = pl.BlockSpec((tq, gw), lambda bi, h, qi: (q_blk0 + bi * n_q_tiles + qi, h))
    lat_spec = lambda col: pl.BlockSpec((t, HEAD_DIM), lambda bi, h, qi: (bi, col + h))
    ctx_spec = lambda col: pl.BlockSpec((tc, HEAD_DIM), lambda bi, h, qi: (ctx_blk0 + bi, col + h))
    aug = lambda n: pltpu.VMEM((n, 2 * HEAD_DIM), BF16)
    if latent:
        in_specs = [q_spec, lat_spec(k_col), ctx_spec(k_col), lat_spec(v_col), ctx_spec(v_col)]
        scratch = [aug(t), aug(tc)]
    else:
        in_specs = [q_spec, ctx_spec(k_col), ctx_spec(v_col)]
        scratch = [aug(tc)]
    return pl.pallas_call(
        functools.partial(_attn_kernel, latent=latent),
        grid=(b, n_kv, n_q_tiles),
        in_specs=in_specs,
        out_specs=pl.BlockSpec((tq, gw), lambda bi, h, qi: (bi * n_q_tiles + qi, h)),
        out_shape=jax.ShapeDtypeStruct((b * (t if latent else tc), cfg["n_q_heads"] * HEAD_DIM),
                                       BF16),
        scratch_shapes=scratch,
        compiler_params=_params("parallel", "parallel", "arbitrary"),
        name="attention",
    )(*[qkv] * len(in_specs))


def _lru_kernel(ux_ref, ug_ref, cw_ref, cb_ref, w_ref, br_ref, bi_ref, lam_ref, o_ref,
                u_scr, af_scr, bf_scr, yf_scr, ab_scr, bb_scr, yb_scr, *, b, t, tc):
    ch, gap = LRU_CHUNK, LRU_GAP
    n_lat = b * t
    conv_width = cw_ref.shape[0]
    conv_left = conv_width // 2

    lat = (gap, 0, t + gap, t)
    ctx = (gap + b * (t + gap), n_lat, tc + gap, tc)

    zeros_gap = jnp.zeros((gap, LANES), F32)
    for kind, n in ((lat, t), (ctx, tc)):
        for bi in range(b):
            u0 = kind[0] + bi * kind[2]
            r0 = kind[1] + bi * kind[3]
            u_scr[u0 - gap:u0, :] = zeros_gap
            u_scr[u0:u0 + n, :] = ux_ref[r0:r0 + n, :].astype(F32)
    u_end = ctx[0] + b * ctx[2] - gap
    u_scr[u_end:u_end + gap, :] = zeros_gap

    def conv_chunk(s0):
        u = cb_ref[...]
        for j in range(conv_width):
            tap0 = s0 - conv_left + j
            u = u + cw_ref[j:j + 1, :] * u_scr[tap0:tap0 + ch, :]
        return u

    for kind, n in ((lat, t), (ctx, tc)):
        for bi in range(b):
            u0 = kind[0] + bi * kind[2]
            held = None
            for c in range(n // ch):
                cur = conv_chunk(u0 + c * ch)
                if held is not None:
                    u_scr[u0 + (c - 1) * ch:u0 + c * ch, :] = held
                held = cur
            u_scr[u0 + n - ch:u0 + n, :] = held

    def slab_rows(kind, bi, off):
        start = kind[1] + bi * kind[3] + off
        if not isinstance(start, int):
            start = pl.multiple_of(start, ch)
        return pl.ds(start, ch)

    def coefficients(d, kind, off, a_scr, b_scr):
        lam = lam_ref[d:d + 1, :]
        softplus_neg_lam = jnp.maximum(-lam, 0.0) + jnp.log1p(jnp.exp(-jnp.abs(lam)))
        half_rate = -0.5 * LRU_C * softplus_neg_lam
        w_half = w_ref[:, d * 2 * LANES:(d + 1) * 2 * LANES] * 0.5
        br_half = 0.5 * br_ref[d:d + 1, :]
        bi_half = 0.5 * bi_ref[d:d + 1, :]
        for bi in range(b):
            s0 = kind[0] + bi * kind[2] + off
            if not isinstance(s0, int):
                s0 = pl.multiple_of(s0, SUBLANES)
            u = u_scr[pl.ds(s0, ch), :]
            gates = _dot(u.astype(BF16), w_half)
            log_a = jnp.tanh(gates[:, :LANES] + br_half) * half_rate + half_rate
            i_gate = 0.5 * jnp.tanh(gates[:, LANES:] + bi_half) + 0.5
            a = jnp.exp(log_a)
            one_minus_a2 = -jnp.tanh(log_a) * (a * a + 1.0)
            root = jnp.where(one_minus_a2 > 0.0, one_minus_a2 * lax.rsqrt(one_minus_a2), 0.0)
            a_scr[pl.ds(bi, ch, stride=b), :] = a
            b_scr[pl.ds(bi, ch, stride=b), :] = root * i_gate * u

    def emit(rows, y):
        gate = jax.nn.gelu(ug_ref[rows, :].astype(F32), approximate=True)
        o_ref[rows, :] = (y * gate).astype(BF16)

    def pair(kind, f_off, b_off, carry, mode):
        coefficients(0, kind, f_off, af_scr, bf_scr)
        coefficients(1, kind, b_off, ab_scr, bb_scr)

        def two_steps(h, a_scr, b_scr, y_scr, s0, s1):
            r0 = pl.ds(pl.multiple_of(s0 * b, b), b)
            r1 = pl.ds(pl.multiple_of(s1 * b, b), b)
            a0, b0, a1, b1 = a_scr[r0, :], b_scr[r0, :], a_scr[r1, :], b_scr[r1, :]
            y_scr[r0, :] = a0 * h + b0
            h = (a1 * a0) * h + (a1 * b0 + b1)
            y_scr[r1, :] = h
            return h

        def step(s, hs):
            hf, hb = hs
            hf = two_steps(hf, af_scr, bf_scr, yf_scr, 2 * s, 2 * s + 1)
            hb = two_steps(hb, ab_scr, bb_scr, yb_scr, ch - 1 - 2 * s, ch - 2 - 2 * s)
            return hf, hb

        carry = lax.fori_loop(0, ch // 2, step, carry, unroll=16)

        for bi in range(b):
            blk = pl.ds(bi, ch, stride=b)
            rf = slab_rows(kind, bi, f_off)
            rb = slab_rows(kind, bi, b_off)
            if mode == "same":
                emit(rf, yf_scr[blk, :] + yb_scr[blk, :])
            elif mode == "first":
                o_ref[rf, :] = yf_scr[blk, :].astype(BF16)
                o_ref[rb, :] = yb_scr[blk, :].astype(BF16)
            else:
                emit(rf, o_ref[rf, :].astype(F32) + yf_scr[blk, :])
                emit(rb, o_ref[rb, :].astype(F32) + yb_scr[blk, :])
        return carry

    n_lat_chunks = t // ch
    half = n_lat_chunks // 2
    zero = jnp.zeros((b, LANES), F32)
    carry = pair(ctx, 0, 0, (zero, zero), "same")
    carry = lax.fori_loop(
        0, half,
        lambda q, hs: pair(lat, q * ch, (n_lat_chunks - 1 - q) * ch, hs, "first"), carry)
    lax.fori_loop(
        0, half,
        lambda q, hs: pair(lat, (half + q) * ch, (half - 1 - q) * ch, hs, "second"), carry)


def _lru(rest, conv_w, conv_b, w_bd, b_r, b_i, lam, *, cfg):
    rows = rest.shape[0]
    d_lru = conv_w.shape[1]
    ux_blk = cfg["ux_col"] // LANES
    ug_blk = cfg["ug_col"] // LANES
    nb = cfg["b"]
    assert cfg["tc"] == LRU_CHUNK and (cfg["t"] // LRU_CHUNK) % 2 == 0
    kern = functools.partial(_lru_kernel, b=nb, t=cfg["t"], tc=cfg["tc"])
    vec = lambda n: pl.BlockSpec((n, LANES), lambda c: (0, c))
    return pl.pallas_call(
        kern,
        grid=(d_lru // LANES,),
        in_specs=[
            pl.BlockSpec((rows, LANES), lambda c: (0, ux_blk + c)),
            pl.BlockSpec((rows, LANES), lambda c: (0, ug_blk + c)),
            vec(conv_w.shape[0]), vec(1),
            pl.BlockSpec((None, LANES, 4 * LANES), lambda c: (c, 0, 0)),
            vec(2), vec(2), vec(2),
        ],
        out_specs=pl.BlockSpec((rows, LANES), lambda c: (0, c)),
        out_shape=jax.ShapeDtypeStruct((rows, d_lru), BF16),
        scratch_shapes=[
            pltpu.VMEM((rows + (2 * nb + 1) * LRU_GAP, LANES), F32),
        ] + [pltpu.VMEM((nb * LRU_CHUNK, LANES), F32)] * 6,
        compiler_params=_params("parallel"),
        name="rglru",
    )(rest, rest, conv_w, conv_b.reshape(1, d_lru), w_bd, b_r, b_i, lam)


def _lru_block_diag(w_r, w_i):
    _, nb, bw, _ = w_r.shape
    per = LANES // bw
    eye = jnp.eye(per, dtype=w_r.dtype)

    def bd(w):
        w = w.reshape(nb // per, per, bw, bw)
        return jnp.einsum("cipq,ij->cipjq", w, eye).reshape(nb // per, LANES, LANES)

    return jnp.concatenate([bd(w_r[0]), bd(w_i[0]), bd(w_r[1]), bd(w_i[1])], axis=-1).astype(BF16)


def _fourier_kernel(u_ref, cst_ref, rev_ref, w2_ref, o_ref, p_scr, q_scr, *, t, n_groups, scale):
    half = t // 2
    for g in range(n_groups):
        cols = slice(g * FOURIER_GROUP, (g + 1) * FOURIER_GROUP)
        pq = _dot(u_ref[:, cols], w2_ref[...])
        p_scr[:, cols] = pq[:, :FOURIER_GROUP].astype(BF16)
        q_scr[:, cols] = pq[:, FOURIER_GROUP:].astype(BF16)
    a = _dot(cst_ref[:, :t], p_scr[...])
    bs = _dot(cst_ref[:, t:], q_scr[...])
    o_ref[0:half, :] = ((a - bs) * scale).astype(BF16)
    mirrored = _dot(rev_ref[...], ((a + bs) * scale).astype(BF16))
    p = p_scr[...].astype(F32)
    odd_row = (lax.broadcasted_iota(jnp.int32, p.shape, 0) & 1) == 1
    mid = jnp.sum(jnp.where(odd_row, -p, p), axis=0, keepdims=True) * scale
    first_row = lax.broadcasted_iota(jnp.int32, mirrored.shape, 0) == 0
    o_ref[half:t, :] = jnp.where(first_row, mid, mirrored).astype(BF16)


def _dft_tables(t):
    half = t // 2
    k = jnp.arange(half, dtype=jnp.int32)[:, None]
    s = jnp.arange(t, dtype=jnp.int32)[None, :]
    at = ((k * s) % t).astype(F32) * (2.0 * math.pi / t)
    cst = jnp.concatenate([jnp.cos(at), jnp.sin(at)], axis=1).astype(BF16)
    r = jnp.arange(half, dtype=jnp.int32)
    rev = (r[:, None] + r[None, :] == half).astype(BF16)
    c = jnp.arange(FOURIER_GROUP, dtype=jnp.int32)
    ac = ((c[:, None] * c[None, :]) % FOURIER_GROUP).astype(F32) * (2.0 * math.pi / FOURIER_GROUP)
    w2 = jnp.concatenate([jnp.cos(ac), jnp.sin(ac)], axis=1).astype(BF16)
    return cst, rev, w2


def _fourier(rest, cst, rev, w2, *, cfg, seq_len, row_blk0):
    d_f = cfg["d_branch"]
    wcols = min(d_f, 512)
    uf_blk = cfg["uf_col"] // wcols
    kern = functools.partial(_fourier_kernel, t=seq_len, n_groups=wcols // FOURIER_GROUP,
                             scale=(seq_len * FOURIER_GROUP) ** -0.5)
    const = lambda shape: pl.BlockSpec(shape, lambda bi, hf: (0, 0), pipeline_mode=pl.Buffered(1))
    return pl.pallas_call(
        kern,
        grid=(cfg["b"], d_f // wcols),
        in_specs=[
            pl.BlockSpec((seq_len, wcols), lambda bi, hf: (row_blk0 + bi, uf_blk + hf)),
            const(cst.shape), const(rev.shape), const(w2.shape),
        ],
        out_specs=pl.BlockSpec((seq_len, wcols), lambda bi, hf: (bi, hf)),
        out_shape=jax.ShapeDtypeStruct((cfg["b"] * seq_len, d_f), BF16),
        scratch_shapes=[pltpu.VMEM((seq_len, wcols), BF16)] * 2,
        compiler_params=_params("parallel", "parallel"),
        name="fourier",
    )(rest, cst, rev, w2)


def _merge_kernel(*refs, n_parts, n_lat_tiles):
    h_ref = refs[0]
    att_refs = refs[1:1 + n_parts]
    rec_ref = refs[1 + n_parts]
    fou_refs = refs[2 + n_parts:2 + 2 * n_parts]
    wgs, bgs, wbs = (refs[2 + 2 * n_parts + 3 * k:5 + 2 * n_parts + 3 * k] for k in range(3))
    o_ref = refs[-1]
    is_latent = pl.program_id(0) < n_lat_tiles

    def rows_of(parts):
        if len(parts) == 1:
            return parts[0][...]
        return jnp.where(is_latent, parts[0][...], parts[1][...])

    h = h_ref[...]
    acc = None
    for y, wg, bg, wb in zip((rows_of(att_refs), rec_ref[...], rows_of(fou_refs)), wgs, bgs, wbs):
        gate = jax.nn.sigmoid(_dot(h, wg[...]) + bg[...])
        term = gate * _dot(y, wb[...])
        acc = term if acc is None else acc + term
    o_ref[...] = acc.astype(BF16)


def _merge(h, att_parts, rec, fou_parts, w_gate, b_gate, w_branch, l, *, rows):
    d = h.shape[1]
    d_b = rec.shape[1]
    tm, tn = MERGE_ROW_TILE, 512
    nj = d // tn
    n_parts = len(att_parts)
    n_lat_tiles = att_parts[0].shape[0] // tm
    y_spec = pl.BlockSpec((tm, d_b), lambda i, j: (i, 0))
    part_specs = [pl.BlockSpec((tm, d_b), lambda i, j: (jnp.minimum(i, n_lat_tiles - 1), 0)),
                  pl.BlockSpec((tm, d_b), lambda i, j: (jnp.maximum(i - n_lat_tiles, 0), 0),
                               pipeline_mode=pl.Buffered(1))][:n_parts]
    wg_spec = lambda k: pl.BlockSpec((None, d, tn), lambda i, j: (l, 0, k * nj + j))
    bg_spec = lambda k: pl.BlockSpec((None, 1, tn), lambda i, j: (l, 0, k * nj + j))
    wb_spec = lambda k: pl.BlockSpec((None, None, d_b, tn), lambda i, j: (l, k, 0, j))
    bg = b_gate.reshape(b_gate.shape[0], 1, -1)
    return pl.pallas_call(
        functools.partial(_merge_kernel, n_parts=n_parts, n_lat_tiles=n_lat_tiles),
        grid=(rows // tm, nj),
        in_specs=[pl.BlockSpec((tm, d), lambda i, j: (i, 0))] + part_specs + [y_spec] + part_specs
                 + [wg_spec(0), wg_spec(1), wg_spec(2), bg_spec(0), bg_spec(1), bg_spec(2),
                    wb_spec(0), wb_spec(1), wb_spec(2)],
        out_specs=pl.BlockSpec((tm, tn), lambda i, j: (i, j)),
        out_shape=jax.ShapeDtypeStruct((rows, d), BF16),
        compiler_params=_params("parallel", "arbitrary"),
        name="gated_merge",
    )(h, *att_parts, rec, *fou_parts, w_gate, w_gate, w_gate, bg, bg, bg,
      w_branch, w_branch, w_branch)


def _out_proj_kernel(*refs, n_lat_tiles):
    x_refs, (m_ref, w_ref, mod_ref, g_ref, o_ref, h_ref) = refs[:-6], refs[-6:]
    delta = mod_ref[2:3, :] * _dot(m_ref[...], w_ref[...])

    def emit(r):
        y = r[...] + delta
        o_ref[...] = y
        h_ref[...] = _ada_norm(y, g_ref[...], mod_ref[3:4, :], mod_ref[4:5, :])

    if len(x_refs) == 2:
        _for_row_source(*x_refs, n_lat_tiles, emit)
    else:
        emit(x_refs[0])


def _out_proj(x_parts, m, w_out, l, mod_l, g_norm2, *, cfg, rows):
    d = m.shape[1]
    tm = ROW_TILE
    row_spec = pl.BlockSpec((tm, d), lambda i: (i, 0))
    n_lat_tiles = x_parts[0].shape[0] // tm
    x_specs = _split_row_specs(n_lat_tiles, tm, d) if len(x_parts) == 2 else [row_spec]
    return pl.pallas_call(
        functools.partial(_out_proj_kernel, n_lat_tiles=n_lat_tiles),
        grid=(rows // tm,),
        in_specs=x_specs + [
            row_spec,
            pl.BlockSpec((None, d, d), lambda i: (l, 0, 0), pipeline_mode=pl.Buffered(1)),
            pl.BlockSpec((None, 6, d), lambda i: (cfg["mod_row"](i), 0, 0)),
            pl.BlockSpec((1, d), lambda i: (0, 0)),
        ],
        out_specs=[row_spec, row_spec],
        out_shape=[jax.ShapeDtypeStruct((rows, d), F32), jax.ShapeDtypeStruct((rows, d), BF16)],
        compiler_params=_params("parallel"),
        name="out_proj",
    )(*x_parts, m, w_out, mod_l, g_norm2.reshape(1, d))


def _ffn_up_kernel(h_ref, wg_ref, wu_ref, *refs):
    n_cast = (len(refs) - 1) // 2
    a_ref = refs[n_cast]
    h = h_ref[...]
    gt = _dot(h, wg_ref[...])
    up = _dot(h, wu_ref[...])
    a_ref[...] = (gt * jax.nn.sigmoid(gt) * up).astype(BF16)
    for src, dst in zip(refs[:n_cast], refs[n_cast + 1:]):
        if len(dst.shape) == 2:
            dst[...] = src[...].astype(BF16)
        else:
            for g in range(dst.shape[0]):
                dst[g] = src[:, g * dst.shape[2]:(g + 1) * dst.shape[2]].astype(BF16)


def _col_blocked(w, n_groups):
    l, r, c = w.shape
    return w.reshape(l, r, n_groups, c // n_groups).transpose(0, 2, 1, 3)


def _ffn_up(h2, w_ffn_in, l, *, cfg, rows, cast_srcs=(), cast_layer=None):
    d = h2.shape[1]
    d_ff = w_ffn_in.shape[2] // 2
    tm, tf = FFN_UP_ROW_TILE, cfg["tf"]
    nk = d_ff // tf
    n_steps = (rows // tm) * nk
    cast_in, cast_out, cast_shape = [], [], []
    for w, groups in cast_srcs:
        _, r, c = w.shape
        br = next(x for x in (16, 32, 64, 128, 256) if r % x == 0 and r // x <= n_steps)
        blk = lambda i, k, n=r // br: jnp.minimum(i * nk + k, n - 1)
        cast_in.append(pl.BlockSpec((None, br, c), lambda i, k, blk=blk: (cast_layer, blk(i, k), 0)))
        if groups == 1:
            cast_out.append(pl.BlockSpec((None, br, c), lambda i, k, blk=blk: (0, blk(i, k), 0)))
            cast_shape.append(jax.ShapeDtypeStruct((1, r, c), BF16))
        else:
            cast_out.append(pl.BlockSpec((None, groups, br, c // groups),
                                         lambda i, k, blk=blk: (0, 0, blk(i, k), 0)))
            cast_shape.append(jax.ShapeDtypeStruct((1, groups, r, c // groups), BF16))
    return pl.pallas_call(
        _ffn_up_kernel,
        grid=(rows // tm, nk),
        in_specs=[
            pl.BlockSpec((tm, d), lambda i, k: (i, 0)),
            pl.BlockSpec((None, d, tf), lambda i, k: (l, 0, k)),
            pl.BlockSpec((None, d, tf), lambda i, k: (l, 0, nk + k)),
        ] + cast_in,
        out_specs=[pl.BlockSpec((tm, tf), lambda i, k: (i, k))] + cast_out,
        out_shape=[jax.ShapeDtypeStruct((rows, d_ff), BF16)] + cast_shape,
        compiler_params=_params("arbitrary", "arbitrary"),
        name="ffn_up",
    )(h2, w_ffn_in, w_ffn_in, *[w for w, _ in cast_srcs])


def _ffn_down_kernel(a_ref, x_ref, wo_ref, mod_ref, nmod_ref, g_ref, *refs, nj, d, final):
    out_refs, y_scr = refs[:-1], refs[-1]
    i, j = pl.program_id(0), pl.program_id(1)
    cur = i % 2

    @pl.when((i == 0) & (j == 0))
    def _():
        y_scr[1] = jnp.zeros(y_scr.shape[1:], F32)

    prv = y_scr.at[1 - cur]
    ssq = sum(jnp.sum(prv[jj] * prv[jj], axis=-1, keepdims=True) for jj in range(nj))
    normed = prv[j] * lax.rsqrt(ssq * (1.0 / d) + NORM_EPS) * g_ref[...]
    if final:
        out_refs[0][...] = normed
    else:
        out_refs[1][...] = (normed * (1.0 + nmod_ref[1:2, :]) + nmod_ref[0:1, :]).astype(BF16)

    jw = jnp.where(i == pl.num_programs(0) - 1, nj - 1, j)
    y = x_ref[...] + mod_ref[5:6, :] * _dot(a_ref[...], wo_ref[jw])
    if not final:
        out_refs[0][...] = y
    y_scr[cur, j] = y


def _ffn_down(a, xmid, w_ffn_out, l, mod_l, next_mod, gain, *, cfg, rows, final):
    d = xmid.shape[1]
    d_ff = a.shape[1]
    tm = ROW_TILE
    _, nj, _, tn = w_ffn_out.shape
    n_tiles = rows // tm
    cur = lambda i: jnp.minimum(i, n_tiles - 1)
    prev = lambda i: jnp.maximum(i - 1, 0)
    cur_j = lambda i, j: jnp.where(i == n_tiles, nj - 1, j)
    prev_j = lambda i, j: jnp.where(i == 0, 0, j)
    cur_spec = pl.BlockSpec((tm, tn), lambda i, j: (cur(i), cur_j(i, j)))
    prev_spec = pl.BlockSpec((tm, tn), lambda i, j: (prev(i), prev_j(i, j)))
    if final:
        out_specs, out_shape = [prev_spec], [jax.ShapeDtypeStruct((rows, d), F32)]
    else:
        out_specs = [cur_spec, prev_spec]
        out_shape = [jax.ShapeDtypeStruct((rows, d), F32), jax.ShapeDtypeStruct((rows, d), BF16)]
    return pl.pallas_call(
        functools.partial(_ffn_down_kernel, nj=nj, d=d, final=final),
        grid=(n_tiles + 1, nj),
        in_specs=[
            pl.BlockSpec((tm, d_ff), lambda i, j: (cur(i), 0)),
            cur_spec,
            pl.BlockSpec((None, nj, d_ff, tn), lambda i, j: (l, 0, 0, 0),
                         pipeline_mode=pl.Buffered(1)),
            pl.BlockSpec((None, 6, tn), lambda i, j: (cfg["mod_row"](cur(i)), 0, cur_j(i, j))),
            pl.BlockSpec((None, 6, tn), lambda i, j: (cfg["mod_row"](prev(i)), 0, j)),
            pl.BlockSpec((1, tn), lambda i, j: (0, j)),
        ],
        out_specs=out_specs,
        out_shape=out_shape,
        scratch_shapes=[pltpu.VMEM((2, nj, tm, tn), F32)],
        compiler_params=_params("arbitrary", "arbitrary"),
        name="ffn_down",
    )(a, xmid, w_ffn_out, mod_l, next_mod, gain.reshape(1, d))


def _rope_tables(t, tile_rows):
    pairs = HEAD_DIM // 4
    rows = t // GRID_W
    row = jnp.repeat(jnp.arange(rows, dtype=F32), GRID_W)
    col = jnp.tile(jnp.arange(GRID_W, dtype=F32), rows)
    inv = ROPE_THETA ** (-jnp.arange(pairs, dtype=F32) / pairs)
    ang = jnp.concatenate([row[:, None] * inv, col[:, None] * inv], axis=-1)
    cos, sin = jnp.cos(ang), jnp.sin(ang)
    cosf = jnp.concatenate([cos, cos], axis=-1)
    sinf = jnp.concatenate([-sin, sin], axis=-1)
    cosf = jnp.concatenate([cosf, jnp.ones((tile_rows, HEAD_DIM), F32)], axis=0)
    sinf = jnp.concatenate([sinf, jnp.zeros((tile_rows, HEAD_DIM), F32)], axis=0)
    return cosf, sinf


def kernel(x, c, ctx, c_ctx, w_mod, b_mod, g_norm1, g_norm2, w_in, q_gain, k_gain, conv_w, conv_b, lru_w_r, lru_b_r, lru_w_i, lru_b_i, lru_lambda, w_branch, w_gate, b_gate, w_out, w_ffn_in, w_ffn_out, g_final):
    b, t, d = x.shape
    tc = ctx.shape[1]
    depth = w_in.shape[0]
    d_branch = w_branch.shape[2]
    n_q_heads = d_branch // HEAD_DIM
    n_kv_heads = n_q_heads // Q_PER_KV
    d_kv = n_kv_heads * HEAD_DIM
    d_ff = w_ffn_out.shape[1]
    n_lat, n_ctx = b * t, b * tc
    assert b == SUBLANES and t % IN_PROJ_ROW_TILE == 0 and n_ctx % FFN_UP_ROW_TILE == 0
    assert tc % LRU_CHUNK == 0 and t % LRU_CHUNK == 0
    tiles_per_batch = t // ROW_TILE

    cfg = dict(
        b=b, t=t, tc=tc, d_branch=d_branch, n_q_heads=n_q_heads, n_kv_heads=n_kv_heads,
        tn_in=d_branch + 2 * d_kv,
        ux_col=0, ug_col=d_branch, uf_col=2 * d_branch,
        tf=512 if d_ff % 512 == 0 else 256,
        mod_row=lambda i: jnp.minimum(i // tiles_per_batch, b),
        rope_block=lambda i: jnp.where(i < n_lat // IN_PROJ_ROW_TILE, i % (t // IN_PROJ_ROW_TILE),
                                       t // IN_PROJ_ROW_TILE),
    )

    x_parts = (x.reshape(n_lat, d), ctx.reshape(n_ctx, d))
    cc = jnp.concatenate([c, c_ctx[None, :], jnp.zeros((2 * SUBLANES - b - 1, d), F32)], axis=0)
    mod = _modulation(cc, w_mod, b_mod).reshape(depth, 2 * SUBLANES, 6, d)

    w_f32 = ((w_in, 1), (w_gate, 1), (w_branch.reshape(depth, -1, d), 1), (w_out, 1),
             (w_ffn_in, 1), (w_ffn_out, FFN_DOWN_COL_BLOCKS))
    wb = [(w[0:1] if g == 1 else _col_blocked(w[0:1], g)).astype(BF16) for w, g in w_f32]

    cosf, sinf = _rope_tables(t, IN_PROJ_ROW_TILE)
    dft_lat = _dft_tables(t)
    dft_ctx = _dft_tables(tc)

    h1 = _prenorm(*x_parts, mod[0], g_norm1[0], cfg=cfg)
    out = None
    for l in range(depth):
        last = l == depth - 1
        rows = n_lat if last else n_lat + n_ctx
        w_in_b, w_gate_b, w_branch_b, w_out_b, w_ffn_in_b, w_ffn_out_b = wb
        w_branch_b = w_branch_b.reshape((1,) + w_branch.shape[1:])
        qkv, rest = _in_proj(h1, w_in_b, 0, cosf, sinf, q_gain[l], k_gain[l], cfg=cfg)
        att = (_attention(qkv, cfg=cfg, latent=True),)
        rec = _lru(rest, conv_w[l], conv_b[l], _lru_block_diag(lru_w_r[l], lru_w_i[l]),
                   lru_b_r[l], lru_b_i[l], lru_lambda[l], cfg=cfg)
        fou = (_fourier(rest, *dft_lat, cfg=cfg, seq_len=t, row_blk0=0),)
        if not last:
            att += (_attention(qkv, cfg=cfg, latent=False),)
            fou += (_fourier(rest, *dft_ctx, cfg=cfg, seq_len=tc, row_blk0=n_lat // tc),)
        m = _merge(h1, att, rec, fou, w_gate_b, b_gate[l:l + 1], w_branch_b, 0, rows=rows)
        xmid, h2 = _out_proj(x_parts, m, w_out_b, 0, mod[l], g_norm2[l], cfg=cfg, rows=rows)
        if last:
            (a,) = _ffn_up(h2, w_ffn_in_b, 0, cfg=cfg, rows=rows)
            (out,) = _ffn_down(a, xmid, w_ffn_out_b, 0, mod[l], mod[l], g_final,
                               cfg=cfg, rows=rows, final=True)
        else:
            a, *wb = _ffn_up(h2, w_ffn_in_b, 0, cfg=cfg, rows=rows, cast_srcs=w_f32,
                             cast_layer=l + 1)
            xall, h1 = _ffn_down(a, xmid, w_ffn_out_b, 0, mod[l], mod[l + 1], g_norm1[l + 1],
                                 cfg=cfg, rows=rows, final=False)
            x_parts = (xall,)
    return out.reshape(b, t, d)
```

```python
import functools
import math

import jax
import jax.numpy as jnp
from jax import lax
from jax.experimental import pallas as pl
from jax.experimental.pallas import tpu as pltpu

F32 = jnp.float32
BF16 = jnp.bfloat16

HEAD_DIM = 128
Q_PER_KV = 4
GRID_W = 64
ROPE_THETA = 10000.0
NORM_EPS = 1e-6
LRU_C = 8.0
FOURIER_GROUP = 128

LANES = 128
SUBLANES = 8
V7X_VMEM_BYTES = 64 * 1024 * 1024
VMEM_LIMIT_BYTES = V7X_VMEM_BYTES * 7 // 8

ROW_TILE = 512
COL_TILE = 512
IN_PROJ_ROW_TILE = 1024
MERGE_ROW_TILE = 1024
FFN_UP_ROW_TILE = 1024
ATTN_Q_TILE = 2048
ATTN_SUB_ROWS = 512
LRU_CHUNK = 256
LRU_GAP = SUBLANES


def _params(*semantics):
    return pltpu.CompilerParams(dimension_semantics=semantics,
                                vmem_limit_bytes=VMEM_LIMIT_BYTES)


def _dot(a, b):
    return jnp.dot(a, b, preferred_element_type=F32)


def _dot_nt(a, b):
    return lax.dot_general(a, b, (((1,), (1,)), ((), ())), preferred_element_type=F32)


def _rms(x):
    return x * lax.rsqrt(jnp.mean(x * x, axis=-1, keepdims=True) + NORM_EPS)


def _ada_norm(x, gain, shift, scale):
    return (_rms(x) * gain * (1.0 + scale) + shift).astype(BF16)


def _mod_kernel(c_ref, w_ref, b_ref, o_ref):
    c = c_ref[...]
    s = (c * jax.nn.sigmoid(c)).astype(BF16)
    o_ref[...] = _dot(s, w_ref[...].astype(BF16)) + b_ref[...]


def _modulation(cc, w_mod, b_mod):
    depth, d, n = w_mod.shape
    rows = cc.shape[0]
    tn = 2 * COL_TILE
    return pl.pallas_call(
        _mod_kernel,
        grid=(depth, n // tn),
        in_specs=[
            pl.BlockSpec((rows, d), lambda l, j: (0, 0)),
            pl.BlockSpec((None, d, tn), lambda l, j: (l, 0, j)),
            pl.BlockSpec((None, 1, tn), lambda l, j: (l, 0, j)),
        ],
        out_specs=pl.BlockSpec((None, rows, tn), lambda l, j: (l, 0, j)),
        out_shape=jax.ShapeDtypeStruct((depth, rows, n), F32),
        compiler_params=_params("parallel", "parallel"),
        name="modulation",
    )(cc, w_mod, b_mod.reshape(depth, 1, n))


def _split_row_specs(n_lat_tiles, tm, d):
    return [pl.BlockSpec((tm, d), lambda i: (jnp.minimum(i, n_lat_tiles - 1), 0)),
            pl.BlockSpec((tm, d), lambda i: (jnp.maximum(i - n_lat_tiles, 0), 0))]


def _for_row_source(x_ref, c_ref, n_lat_tiles, fn):
    i = pl.program_id(0)
    pl.when(i < n_lat_tiles)(lambda: fn(x_ref))
    pl.when(i >= n_lat_tiles)(lambda: fn(c_ref))


def _prenorm_kernel(x_ref, c_ref, mod_ref, g_ref, h_ref, *, n_lat_tiles):
    def emit(r):
        h_ref[...] = _ada_norm(r[...], g_ref[...], mod_ref[0:1, :], mod_ref[1:2, :])
    _for_row_source(x_ref, c_ref, n_lat_tiles, emit)


def _prenorm(x2d, ctx2d, mod_l, g_norm, *, cfg):
    d = x2d.shape[1]
    rows = x2d.shape[0] + ctx2d.shape[0]
    tm = ROW_TILE
    n_lat_tiles = x2d.shape[0] // tm
    return pl.pallas_call(
        functools.partial(_prenorm_kernel, n_lat_tiles=n_lat_tiles),
        grid=(rows // tm,),
        in_specs=_split_row_specs(n_lat_tiles, tm, d) + [
            pl.BlockSpec((None, 6, d), lambda i: (cfg["mod_row"](i), 0, 0)),
            pl.BlockSpec((1, d), lambda i: (0, 0)),
        ],
        out_specs=pl.BlockSpec((tm, d), lambda i: (i, 0)),
        out_shape=jax.ShapeDtypeStruct((rows, d), BF16),
        compiler_params=_params("parallel"),
        name="prenorm",
    )(x2d, ctx2d, mod_l, g_norm.reshape(1, d))


def _in_proj_kernel(h_ref, w_ref, cos_ref, sin_ref, qg_ref, kg_ref, qkv_ref, rest_ref, qkv_scr,
                    *, n_q_heads, n_kv_heads):
    j = pl.program_id(1)

    @pl.when(j == 0)
    def _():
        qkv_scr[...] = _dot(h_ref[...], w_ref[...])

    @pl.when(j == 1)
    def _():
        rest_ref[...] = _dot(h_ref[...], w_ref[...]).astype(BF16)
        cos = cos_ref[...]
        sin = sin_ref[...]
        q_gain = qg_ref[...] * (HEAD_DIM ** -0.5 * math.log2(math.e))
        k_gain = kg_ref[...]
        n_rot = n_q_heads + n_kv_heads
        for hh in range(n_rot):
            cols = slice(hh * HEAD_DIM, (hh + 1) * HEAD_DIM)
            y = _rms(qkv_scr[:, cols]) * (q_gain if hh < n_q_heads else k_gain)
            y = y * cos + pltpu.roll(y, HEAD_DIM // 2, axis=1) * sin
            qkv_ref[:, cols] = y.astype(BF16)
        qkv_ref[:, n_rot * HEAD_DIM:] = qkv_scr[:, n_rot * HEAD_DIM:].astype(BF16)

    @pl.when(j > 1)
    def _():
        rest_ref[...] = _dot(h_ref[...], w_ref[...]).astype(BF16)


def _in_proj(h1, w_in, l, cosf, sinf, q_gain, k_gain, *, cfg):
    rows, d = h1.shape
    d_in = w_in.shape[2]
    tm, tn = IN_PROJ_ROW_TILE, cfg["tn_in"]
    assert d_in == 3 * tn
    kern = functools.partial(_in_proj_kernel, n_q_heads=cfg["n_q_heads"],
                             n_kv_heads=cfg["n_kv_heads"])
    return pl.pallas_call(
        kern,
        grid=(rows // tm, d_in // tn),
        in_specs=[
            pl.BlockSpec((tm, d), lambda i, j: (i, 0)),
            pl.BlockSpec((None, d, tn), lambda i, j: (l, 0, j)),
            pl.BlockSpec((tm, HEAD_DIM), lambda i, j: (cfg["rope_block"](i), 0)),
            pl.BlockSpec((tm, HEAD_DIM), lambda i, j: (cfg["rope_block"](i), 0)),
            pl.BlockSpec((1, HEAD_DIM), lambda i, j: (0, 0)),
            pl.BlockSpec((1, HEAD_DIM), lambda i, j: (0, 0)),
        ],
        out_specs=[
            pl.BlockSpec((tm, tn), lambda i, j: (i, 0)),
            pl.BlockSpec((tm, tn), lambda i, j: (i, jnp.maximum(j - 1, 0))),
        ],
        out_shape=[
            jax.ShapeDtypeStruct((rows, tn), BF16),
            jax.ShapeDtypeStruct((rows, d_in - tn), BF16),
        ],
        scratch_shapes=[pltpu.VMEM((tm, tn), F32)],
        compiler_params=_params("parallel", "arbitrary"),
        name="in_proj",
    )(h1, w_in, cosf, sinf, q_gain.reshape(1, HEAD_DIM), k_gain.reshape(1, HEAD_DIM))


def _attn_kernel(q_ref, *refs, latent):
    if latent:
        kl_ref, kc_ref, vl_ref, vc_ref, o_ref, vl_aug, vc_aug = refs
    else:
        kc_ref, vc_ref, o_ref, vc_aug = refs

    @pl.when(pl.program_id(2) == 0)
    def _():
        for v_ref, aug in ((vl_ref, vl_aug), (vc_ref, vc_aug)) if latent else ((vc_ref, vc_aug),):
            lane = lax.broadcasted_iota(jnp.int32, v_ref.shape, 1)
            aug[:, :HEAD_DIM] = v_ref[...]
            aug[:, HEAD_DIM:] = jnp.where(lane == 0, 1.0, 0.0).astype(BF16)

    kc = kc_ref[...]
    sub = min(q_ref.shape[0], ATTN_SUB_ROWS)
    for r0 in range(0, q_ref.shape[0], sub):
        for g in range(Q_PER_KV):
            cols = slice(g * HEAD_DIM, (g + 1) * HEAD_DIM)
            q = q_ref[r0:r0 + sub, cols]
            sc = _dot_nt(q, kc)
            m = jnp.max(sc, axis=-1, keepdims=True)
            if latent:
                sl = _dot_nt(q, kl_ref[...])
                m = jnp.maximum(m, jnp.max(sl, axis=-1, keepdims=True))
            o = _dot(jnp.exp2((sc - m).astype(BF16)), vc_aug[...])
            if latent:
                o = o + _dot(jnp.exp2((sl - m).astype(BF16)), vl_aug[...])
            o_ref[r0:r0 + sub, cols] = (
                o[:, :HEAD_DIM] / o[:, HEAD_DIM:HEAD_DIM + 1]).astype(BF16)


def _attention(qkv, *, cfg, latent):
    b, t, tc = cfg["b"], cfg["t"], cfg["tc"]
    tq = min(ATTN_Q_TILE, t) if latent else tc
    assert t % tq == 0
    n_q_tiles = t // tq if latent else 1
    q_blk0 = 0 if latent else b * t // tc
    n_kv = cfg["n_kv_heads"]
    gw = Q_PER_KV * HEAD_DIM
    k_col = cfg["n_q_heads"]
    v_col = k_col + n_kv
    ctx_blk0 = b * t // tc

    q_spec = pl.BlockSpec((tq, gw), lambda bi, h, qi: (q_blk0 + bi * n_q_tiles + qi, h))
    lat_spec = lambda col: pl.BlockSpec((t, HEAD_DIM), lambda bi, h, qi: (bi, col + h))
    ctx_spec = lambda col: pl.BlockSpec((tc, HEAD_DIM), lambda bi, h, qi: (ctx_blk0 + bi, col + h))
    aug = lambda n: pltpu.VMEM((n, 2 * HEAD_DIM), BF16)
    if latent:
        in_specs = [q_spec, lat_spec(k_col), ctx_spec(k_col), lat_spec(v_col), ctx_spec(v_col)]
        scratch = [aug(t), aug(tc)]
    else:
        in_specs = [q_spec, ctx_spec(k_col), ctx_spec(v_col)]
        scratch = [aug(tc)]
    return pl.pallas_call(
        functools.partial(_attn_kernel, latent=latent),
        grid=(b, n_kv, n_q_tiles),
        in_specs=in_specs,
        out_specs=pl.BlockSpec((tq, gw), lambda bi, h, qi: (bi * n_q_tiles + qi, h)),
        out_shape=jax.ShapeDtypeStruct((b * (t if latent else tc), cfg["n_q_heads"] * HEAD_DIM),
                                       BF16),
        scratch_shapes=scratch,
        compiler_params=_params("parallel", "parallel", "arbitrary"),
        name="attention",
    )(*[qkv] * len(in_specs))


def _lru_kernel(ux_ref, ug_ref, cw_ref, cb_ref, w_ref, br_ref, bi_ref, lam_ref, o_ref,
                u_scr, af_scr, bf_scr, yf_scr, ab_scr, bb_scr, yb_scr, *, b, t, tc):
    ch, gap = LRU_CHUNK, LRU_GAP
    n_lat = b * t
    conv_width = cw_ref.shape[0]
    conv_left = conv_width // 2

    lat = (gap, 0, t + gap, t)
    ctx = (gap + b * (t + gap), n_lat, tc + gap, tc)

    zeros_gap = jnp.zeros((gap, LANES), F32)
    for kind, n in ((lat, t), (ctx, tc)):
        for bi in range(b):
            u0 = kind[0] + bi * kind[2]
            r0 = kind[1] + bi * kind[3]
            u_scr[u0 - gap:u0, :] = zeros_gap
            u_scr[u0:u0 + n, :] = ux_ref[r0:r0 + n, :].astype(F32)
    u_end = ctx[0] + b * ctx[2] - gap
    u_scr[u_end:u_end + gap, :] = zeros_gap

    def conv_chunk(s0):
        u = cb_ref[...]
        for j in range(conv_width):
            tap0 = s0 - conv_left + j
            u = u + cw_ref[j:j + 1, :] * u_scr[tap0:tap0 + ch, :]
        return u

    for kind, n in ((lat, t), (ctx, tc)):
        for bi in range(b):
            u0 = kind[0] + bi * kind[2]
            held = None
            for c in range(n // ch):
                cur = conv_chunk(u0 + c * ch)
                if held is not None:
                    u_scr[u0 + (c - 1) * ch:u0 + c * ch, :] = held
                held = cur
            u_scr[u0 + n - ch:u0 + n, :] = held

    def slab_rows(kind, bi, off):
        start = kind[1] + bi * kind[3] + off
        if not isinstance(start, int):
            start = pl.multiple_of(start, ch)
        return pl.ds(start, ch)

    def coefficients(d, kind, off, a_scr, b_scr):
        lam = lam_ref[d:d + 1, :]
        softplus_neg_lam = jnp.maximum(-lam, 0.0) + jnp.log1p(jnp.exp(-jnp.abs(lam)))
        half_rate = -0.5 * LRU_C * softplus_neg_lam
        w_half = w_ref[:, d * 2 * LANES:(d + 1) * 2 * LANES] * 0.5
        br_half = 0.5 * br_ref[d:d + 1, :]
        bi_half = 0.5 * bi_ref[d:d + 1, :]
        for bi in range(b):
            s0 = kind[0] + bi * kind[2] + off
            if not isinstance(s0, int):
                s0 = pl.multiple_of(s0, SUBLANES)
            u = u_scr[pl.ds(s0, ch), :]
            gates = _dot(u.astype(BF16), w_half)
            log_a = jnp.tanh(gates[:, :LANES] + br_half) * half_rate + half_rate
            i_gate = 0.5 * jnp.tanh(gates[:, LANES:] + bi_half) + 0.5
            a = jnp.exp(log_a)
            one_minus_a2 = -jnp.tanh(log_a) * (a * a + 1.0)
            root = jnp.where(one_minus_a2 > 0.0, one_minus_a2 * lax.rsqrt(one_minus_a2), 0.0)
            a_scr[pl.ds(bi, ch, stride=b), :] = a
            b_scr[pl.ds(bi, ch, stride=b), :] = root * i_gate * u

    def emit(rows, y):
        gate = jax.nn.gelu(ug_ref[rows, :].astype(F32), approximate=True)
        o_ref[rows, :] = (y * gate).astype(BF16)

    def pair(kind, f_off, b_off, carry, mode):
        coefficients(0, kind, f_off, af_scr, bf_scr)
        coefficients(1, kind, b_off, ab_scr, bb_scr)

        def two_steps(h, a_scr, b_scr, y_scr, s0, s1):
            r0 = pl.ds(pl.multiple_of(s0 * b, b), b)
            r1 = pl.ds(pl.multiple_of(s1 * b, b), b)
            a0, b0, a1, b1 = a_scr[r0, :], b_scr[r0, :], a_scr[r1, :], b_scr[r1, :]
            y_scr[r0, :] = a0 * h + b0
            h = (a1 * a0) * h + (a1 * b0 + b1)
            y_scr[r1, :] = h
            return h

        def step(s, hs):
            hf, hb = hs
            hf = two_steps(hf, af_scr, bf_scr, yf_scr, 2 * s, 2 * s + 1)
            hb = two_steps(hb, ab_scr, bb_scr, yb_scr, ch - 1 - 2 * s, ch - 2 - 2 * s)
            return hf, hb

        carry = lax.fori_loop(0, ch // 2, step, carry, unroll=16)

        for bi in range(b):
            blk = pl.ds(bi, ch, stride=b)
            rf = slab_rows(kind, bi, f_off)
            rb = slab_rows(kind, bi, b_off)
            if mode == "same":
                emit(rf, yf_scr[blk, :] + yb_scr[blk, :])
            elif mode == "first":
                o_ref[rf, :] = yf_scr[blk, :].astype(BF16)
                o_ref[rb, :] = yb_scr[blk, :].astype(BF16)
            else:
                emit(rf, o_ref[rf, :].astype(F32) + yf_scr[blk, :])
                emit(rb, o_ref[rb, :].astype(F32) + yb_scr[blk, :])
        return carry

    n_lat_chunks = t // ch
    half = n_lat_chunks // 2
    zero = jnp.zeros((b, LANES), F32)
    carry = pair(ctx, 0, 0, (zero, zero), "same")
    carry = lax.fori_loop(
        0, half,
        lambda q, hs: pair(lat, q * ch, (n_lat_chunks - 1 - q) * ch, hs, "first"), carry)
    lax.fori_loop(
        0, half,
        lambda q, hs: pair(lat, (half + q) * ch, (half - 1 - q) * ch, hs, "second"), carry)


def _lru(rest, conv_w, conv_b, w_bd, b_r, b_i, lam, *, cfg):
    rows = rest.shape[0]
    d_lru = conv_w.shape[1]
    ux_blk = cfg["ux_col"] // LANES
    ug_blk = cfg["ug_col"] // LANES
    nb = cfg["b"]
    assert cfg["tc"] == LRU_CHUNK and (cfg["t"] // LRU_CHUNK) % 2 == 0
    kern = functools.partial(_lru_kernel, b=nb, t=cfg["t"], tc=cfg["tc"])
    vec = lambda n: pl.BlockSpec((n, LANES), lambda c: (0, c))
    return pl.pallas_call(
        kern,
        grid=(d_lru // LANES,),
        in_specs=[
            pl.BlockSpec((rows, LANES), lambda c: (0, ux_blk + c)),
            pl.BlockSpec((rows, LANES), lambda c: (0, ug_blk + c)),
            vec(conv_w.shape[0]), vec(1),
            pl.BlockSpec((None, LANES, 4 * LANES), lambda c: (c, 0, 0)),
            vec(2), vec(2), vec(2),
        ],
        out_specs=pl.BlockSpec((rows, LANES), lambda c: (0, c)),
        out_shape=jax.ShapeDtypeStruct((rows, d_lru), BF16),
        scratch_shapes=[
            pltpu.VMEM((rows + (2 * nb + 1) * LRU_GAP, LANES), F32),
        ] + [pltpu.VMEM((nb * LRU_CHUNK, LANES), F32)] * 6,
        compiler_params=_params("parallel"),
        name="rglru",
    )(rest, rest, conv_w, conv_b.reshape(1, d_lru), w_bd, b_r, b_i, lam)


def _lru_block_diag(w_r, w_i):
    _, nb, bw, _ = w_r.shape
    per = LANES // bw
    eye = jnp.eye(per, dtype=w_r.dtype)

    def bd(w):
        w = w.reshape(nb // per, per, bw, bw)
        return jnp.einsum("cipq,ij->cipjq", w, eye).reshape(nb // per, LANES, LANES)

    return jnp.concatenate([bd(w_r[0]), bd(w_i[0]), bd(w_r[1]), bd(w_i[1])], axis=-1).astype(BF16)


def _fourier_kernel(u_ref, cst_ref, rev_ref, w2_ref, o_ref, p_scr, q_scr, *, t, n_groups, scale):
    half = t // 2
    for g in range(n_groups):
        cols = slice(g * FOURIER_GROUP, (g + 1) * FOURIER_GROUP)
        pq = _dot(u_ref[:, cols], w2_ref[...])
        p_scr[:, cols] = pq[:, :FOURIER_GROUP].astype(BF16)
        q_scr[:, cols] = pq[:, FOURIER_GROUP:].astype(BF16)
    a = _dot(cst_ref[:, :t], p_scr[...])
    bs = _dot(cst_ref[:, t:], q_scr[...])
    o_ref[0:half, :] = ((a - bs) * scale).astype(BF16)
    mirrored = _dot(rev_ref[...], ((a + bs) * scale).astype(BF16))
    p = p_scr[...].astype(F32)
    odd_row = (lax.broadcasted_iota(jnp.int32, p.shape, 0) & 1) == 1
    mid = jnp.sum(jnp.where(odd_row, -p, p), axis=0, keepdims=True) * scale
    first_row = lax.broadcasted_iota(jnp.int32, mirrored.shape, 0) == 0
    o_ref[half:t, :] = jnp.where(first_row, mid, mirrored).astype(BF16)


def _dft_tables(t):
    half = t // 2
    k = jnp.arange(half, dtype=jnp.int32)[:, None]
    s = jnp.arange(t, dtype=jnp.int32)[None, :]
    at = ((k * s) % t).astype(F32) * (2.0 * math.pi / t)
    cst = jnp.concatenate([jnp.cos(at), jnp.sin(at)], axis=1).astype(BF16)
    r = jnp.arange(half, dtype=jnp.int32)
    rev = (r[:, None] + r[None, :] == half).astype(BF16)
    c = jnp.arange(FOURIER_GROUP, dtype=jnp.int32)
    ac = ((c[:, None] * c[None, :]) % FOURIER_GROUP).astype(F32) * (2.0 * math.pi / FOURIER_GROUP)
    w2 = jnp.concatenate([jnp.cos(ac), jnp.sin(ac)], axis=1).astype(BF16)
    return cst, rev, w2


def _fourier(rest, cst, rev, w2, *, cfg, seq_len, row_blk0):
    d_f = cfg["d_branch"]
    wcols = min(d_f, COL_TILE)
    uf_blk = cfg["uf_col"] // wcols
    kern = functools.partial(_fourier_kernel, t=seq_len, n_groups=wcols // FOURIER_GROUP,
                             scale=(seq_len * FOURIER_GROUP) ** -0.5)
    const = lambda shape: pl.BlockSpec(shape, lambda bi, hf: (0, 0), pipeline_mode=pl.Buffered(1))
    return pl.pallas_call(
        kern,
        grid=(cfg["b"], d_f // wcols),
        in_specs=[
            pl.BlockSpec((seq_len, wcols), lambda bi, hf: (row_blk0 + bi, uf_blk + hf)),
            const(cst.shape), const(rev.shape), const(w2.shape),
        ],
        out_specs=pl.BlockSpec((seq_len, wcols), lambda bi, hf: (bi, hf)),
        out_shape=jax.ShapeDtypeStruct((cfg["b"] * seq_len, d_f), BF16),
        scratch_shapes=[pltpu.VMEM((seq_len, wcols), BF16)] * 2,
        compiler_params=_params("parallel", "parallel"),
        name="fourier",
    )(rest, cst, rev, w2)


def _merge_kernel(*refs, n_parts, n_lat_tiles):
    h_ref = refs[0]
    att_refs = refs[1:1 + n_parts]
    rec_ref = refs[1 + n_parts]
    fou_refs = refs[2 + n_parts:2 + 2 * n_parts]
    wgs, bgs, wbs = (refs[2 + 2 * n_parts + 3 * k:5 + 2 * n_parts + 3 * k] for k in range(3))
    o_ref = refs[-1]
    is_latent = pl.program_id(0) < n_lat_tiles

    def rows_of(parts):
        if len(parts) == 1:
            return parts[0][...]
        return jnp.where(is_latent, parts[0][...], parts[1][...])

    h = h_ref[...]
    acc = None
    for y, wg, bg, wb in zip((rows_of(att_refs), rec_ref[...], rows_of(fou_refs)), wgs, bgs, wbs):
        gate = jax.nn.sigmoid(_dot(h, wg[...]) + bg[...])
        term = gate * _dot(y, wb[...])
        acc = term if acc is None else acc + term
    o_ref[...] = acc.astype(BF16)


def _merge(h, att_parts, rec, fou_parts, w_gate, b_gate, w_branch, l, *, rows):
    d = h.shape[1]
    d_b = rec.shape[1]
    tm, tn = MERGE_ROW_TILE, COL_TILE
    nj = d // tn
    n_parts = len(att_parts)
    n_lat_tiles = att_parts[0].shape[0] // tm
    y_spec = pl.BlockSpec((tm, d_b), lambda i, j: (i, 0))
    part_specs = [pl.BlockSpec((tm, d_b), lambda i, j: (jnp.minimum(i, n_lat_tiles - 1), 0)),
                  pl.BlockSpec((tm, d_b), lambda i, j: (jnp.maximum(i - n_lat_tiles, 0), 0),
                               pipeline_mode=pl.Buffered(1))][:n_parts]
    wg_spec = lambda k: pl.BlockSpec((None, d, tn), lambda i, j: (l, 0, k * nj + j))
    bg_spec = lambda k: pl.BlockSpec((None, 1, tn), lambda i, j: (l, 0, k * nj + j))
    wb_spec = lambda k: pl.BlockSpec((None, None, d_b, tn), lambda i, j: (l, k, 0, j))
    bg = b_gate.reshape(b_gate.shape[0], 1, -1)
    return pl.pallas_call(
        functools.partial(_merge_kernel, n_parts=n_parts, n_lat_tiles=n_lat_tiles),
        grid=(rows // tm, nj),
        in_specs=[pl.BlockSpec((tm, d), lambda i, j: (i, 0))] + part_specs + [y_spec] + part_specs
                 + [wg_spec(0), wg_spec(1), wg_spec(2), bg_spec(0), bg_spec(1), bg_spec(2),
                    wb_spec(0), wb_spec(1), wb_spec(2)],
        out_specs=pl.BlockSpec((tm, tn), lambda i, j: (i, j)),
        out_shape=jax.ShapeDtypeStruct((rows, d), BF16),
        compiler_params=_params("parallel", "arbitrary"),
        name="gated_merge",
    )(h, *att_parts, rec, *fou_parts, w_gate, w_gate, w_gate, bg, bg, bg,
      w_branch, w_branch, w_branch)


def _out_proj_kernel(*refs, n_lat_tiles):
    x_refs, (m_ref, w_ref, mod_ref, g_ref, o_ref, h_ref) = refs[:-6], refs[-6:]
    delta = mod_ref[2:3, :] * _dot(m_ref[...], w_ref[...])

    def emit(r):
        y = r[...] + delta
        o_ref[...] = y
        h_ref[...] = _ada_norm(y, g_ref[...], mod_ref[3:4, :], mod_ref[4:5, :])

    if len(x_refs) == 2:
        _for_row_source(*x_refs, n_lat_tiles, emit)
    else:
        emit(x_refs[0])


def _out_proj(x_parts, m, w_out, l, mod_l, g_norm2, *, cfg, rows):
    d = m.shape[1]
    tm = ROW_TILE
    row_spec = pl.BlockSpec((tm, d), lambda i: (i, 0))
    n_lat_tiles = x_parts[0].shape[0] // tm
    x_specs = _split_row_specs(n_lat_tiles, tm, d) if len(x_parts) == 2 else [row_spec]
    return pl.pallas_call(
        functools.partial(_out_proj_kernel, n_lat_tiles=n_lat_tiles),
        grid=(rows // tm,),
        in_specs=x_specs + [
            row_spec,
            pl.BlockSpec((None, d, d), lambda i: (l, 0, 0), pipeline_mode=pl.Buffered(1)),
            pl.BlockSpec((None, 6, d), lambda i: (cfg["mod_row"](i), 0, 0)),
            pl.BlockSpec((1, d), lambda i: (0, 0)),
        ],
        out_specs=[row_spec, row_spec],
        out_shape=[jax.ShapeDtypeStruct((rows, d), F32), jax.ShapeDtypeStruct((rows, d), BF16)],
        compiler_params=_params("parallel"),
        name="out_proj",
    )(*x_parts, m, w_out, mod_l, g_norm2.reshape(1, d))


def _ffn_up_kernel(h_ref, wg_ref, wu_ref, *refs):
    n_cast = (len(refs) - 1) // 2
    a_ref = refs[n_cast]
    h = h_ref[...]
    gt = _dot(h, wg_ref[...])
    up = _dot(h, wu_ref[...])
    a_ref[...] = (gt * jax.nn.sigmoid(gt) * up).astype(BF16)
    for src, dst in zip(refs[:n_cast], refs[n_cast + 1:]):
        if len(dst.shape) == 2:
            dst[...] = src[...].astype(BF16)
        else:
            for g in range(dst.shape[0]):
                dst[g] = src[:, g * dst.shape[2]:(g + 1) * dst.shape[2]].astype(BF16)


def _col_blocked(w, n_groups):
    l, r, c = w.shape
    return w.reshape(l, r, n_groups, c // n_groups).transpose(0, 2, 1, 3)


def _ffn_up(h2, w_ffn_in, l, *, cfg, rows, cast_srcs=(), cast_layer=None):
    d = h2.shape[1]
    d_ff = w_ffn_in.shape[2] // 2
    tm, tf = FFN_UP_ROW_TILE, cfg["tf"]
    nk = d_ff // tf
    n_steps = (rows // tm) * nk
    cast_in, cast_out, cast_shape = [], [], []
    for w, groups in cast_srcs:
        _, r, c = w.shape
        br = next(x for x in (16, 32, 64, 128, 256) if r % x == 0 and r // x <= n_steps)
        blk = lambda i, k, n=r // br: jnp.minimum(i * nk + k, n - 1)
        cast_in.append(pl.BlockSpec((None, br, c), lambda i, k, blk=blk: (cast_layer, blk(i, k), 0)))
        if groups == 1:
            cast_out.append(pl.BlockSpec((None, br, c), lambda i, k, blk=blk: (0, blk(i, k), 0)))
            cast_shape.append(jax.ShapeDtypeStruct((1, r, c), BF16))
        else:
            cast_out.append(pl.BlockSpec((None, groups, br, c // groups),
                                         lambda i, k, blk=blk: (0, 0, blk(i, k), 0)))
            cast_shape.append(jax.ShapeDtypeStruct((1, groups, r, c // groups), BF16))
    return pl.pallas_call(
        _ffn_up_kernel,
        grid=(rows // tm, nk),
        in_specs=[
            pl.BlockSpec((tm, d), lambda i, k: (i, 0)),
            pl.BlockSpec((None, d, tf), lambda i, k: (l, 0, k)),
            pl.BlockSpec((None, d, tf), lambda i, k: (l, 0, nk + k)),
        ] + cast_in,
        out_specs=[pl.BlockSpec((tm, tf), lambda i, k: (i, k))] + cast_out,
        out_shape=[jax.ShapeDtypeStruct((rows, d_ff), BF16)] + cast_shape,
        compiler_params=_params("arbitrary", "arbitrary"),
        name="ffn_up",
    )(h2, w_ffn_in, w_ffn_in, *[w for w, _ in cast_srcs])


def _ffn_down_kernel(a_ref, x_ref, wo_ref, mod_ref, nmod_ref, g_ref, *refs, nj, d, final):
    out_refs, y_scr = refs[:-1], refs[-1]
    i, j = pl.program_id(0), pl.program_id(1)
    cur = i % 2

    @pl.when((i == 0) & (j == 0))
    def _():
        y_scr[1] = jnp.zeros(y_scr.shape[1:], F32)

    prv = y_scr.at[1 - cur]
    ssq = sum(jnp.sum(prv[jj] * prv[jj], axis=-1, keepdims=True) for jj in range(nj))
    normed = prv[j] * lax.rsqrt(ssq * (1.0 / d) + NORM_EPS) * g_ref[...]
    if final:
        out_refs[0][...] = normed
    else:
        out_refs[1][...] = (normed * (1.0 + nmod_ref[1:2, :]) + nmod_ref[0:1, :]).astype(BF16)

    jw = jnp.where(i == pl.num_programs(0) - 1, nj - 1, j)
    y = x_ref[...] + mod_ref[5:6, :] * _dot(a_ref[...], wo_ref[jw])
    if not final:
        out_refs[0][...] = y
    y_scr[cur, j] = y


def _ffn_down(a, xmid, w_ffn_out, l, mod_l, next_mod, gain, *, cfg, rows, final):
    d = xmid.shape[1]
    d_ff = a.shape[1]
    tm = ROW_TILE
    _, nj, _, tn = w_ffn_out.shape
    n_tiles = rows // tm
    cur = lambda i: jnp.minimum(i, n_tiles - 1)
    prev = lambda i: jnp.maximum(i - 1, 0)
    cur_j = lambda i, j: jnp.where(i == n_tiles, nj - 1, j)
    prev_j = lambda i, j: jnp.where(i == 0, 0, j)
    cur_spec = pl.BlockSpec((tm, tn), lambda i, j: (cur(i), cur_j(i, j)))
    prev_spec = pl.BlockSpec((tm, tn), lambda i, j: (prev(i), prev_j(i, j)))
    if final:
        out_specs, out_shape = [prev_spec], [jax.ShapeDtypeStruct((rows, d), F32)]
    else:
        out_specs = [cur_spec, prev_spec]
        out_shape = [jax.ShapeDtypeStruct((rows, d), F32), jax.ShapeDtypeStruct((rows, d), BF16)]
    return pl.pallas_call(
        functools.partial(_ffn_down_kernel, nj=nj, d=d, final=final),
        grid=(n_tiles + 1, nj),
        in_specs=[
            pl.BlockSpec((tm, d_ff), lambda i, j: (cur(i), 0)),
            cur_spec,
            pl.BlockSpec((None, nj, d_ff, tn), lambda i, j: (l, 0, 0, 0),
                         pipeline_mode=pl.Buffered(1)),
            pl.BlockSpec((None, 6, tn), lambda i, j: (cfg["mod_row"](cur(i)), 0, cur_j(i, j))),
            pl.BlockSpec((None, 6, tn), lambda i, j: (cfg["mod_row"](prev(i)), 0, j)),
            pl.BlockSpec((1, tn), lambda i, j: (0, j)),
        ],
        out_specs=out_specs,
        out_shape=out_shape,
        scratch_shapes=[pltpu.VMEM((2, nj, tm, tn), F32)],
        compiler_params=_params("arbitrary", "arbitrary"),
        name="ffn_down",
    )(a, xmid, w_ffn_out, mod_l, next_mod, gain.reshape(1, d))


def _rope_tables(t, tile_rows):
    pairs = HEAD_DIM // 4
    rows = t // GRID_W
    row = jnp.repeat(jnp.arange(rows, dtype=F32), GRID_W)
    col = jnp.tile(jnp.arange(GRID_W, dtype=F32), rows)
    inv = ROPE_THETA ** (-jnp.arange(pairs, dtype=F32) / pairs)
    ang = jnp.concatenate([row[:, None] * inv, col[:, None] * inv], axis=-1)
    cos, sin = jnp.cos(ang), jnp.sin(ang)
    cosf = jnp.concatenate([cos, cos], axis=-1)
    sinf = jnp.concatenate([-sin, sin], axis=-1)
    cosf = jnp.concatenate([cosf, jnp.ones((tile_rows, HEAD_DIM), F32)], axis=0)
    sinf = jnp.concatenate([sinf, jnp.zeros((tile_rows, HEAD_DIM), F32)], axis=0)
    return cosf, sinf


def kernel(x, c, ctx, c_ctx, w_mod, b_mod, g_norm1, g_norm2, w_in, q_gain, k_gain, conv_w, conv_b, lru_w_r, lru_b_r, lru_w_i, lru_b_i, lru_lambda, w_branch, w_gate, b_gate, w_out, w_ffn_in, w_ffn_out, g_final):
    b, t, d = x.shape
    tc = ctx.shape[1]
    depth = w_in.shape[0]
    d_branch = w_branch.shape[2]
    n_q_heads = d_branch // HEAD_DIM
    n_kv_heads = n_q_heads // Q_PER_KV
    d_kv = n_kv_heads * HEAD_DIM
    d_ff = w_ffn_out.shape[1]
    n_lat, n_ctx = b * t, b * tc
    assert b == SUBLANES and t % IN_PROJ_ROW_TILE == 0 and n_ctx % FFN_UP_ROW_TILE == 0
    assert tc % LRU_CHUNK == 0 and t % LRU_CHUNK == 0
    tiles_per_batch = t // ROW_TILE

    cfg = dict(
        b=b, t=t, tc=tc, d_branch=d_branch, n_q_heads=n_q_heads, n_kv_heads=n_kv_heads,
        tn_in=d_branch + 2 * d_kv,
        ux_col=0, ug_col=d_branch, uf_col=2 * d_branch,
        tf=COL_TILE if d_ff % COL_TILE == 0 else COL_TILE // 2,
        mod_row=lambda i: jnp.minimum(i // tiles_per_batch, b),
        rope_block=lambda i: jnp.where(i < n_lat // IN_PROJ_ROW_TILE, i % (t // IN_PROJ_ROW_TILE),
                                       t // IN_PROJ_ROW_TILE),
    )

    x_parts = (x.reshape(n_lat, d), ctx.reshape(n_ctx, d))
    cc = jnp.concatenate([c, c_ctx[None, :], jnp.zeros((2 * SUBLANES - b - 1, d), F32)], axis=0)
    mod = _modulation(cc, w_mod, b_mod).reshape(depth, 2 * SUBLANES, 6, d)

    w_f32 = ((w_in, 1), (w_gate, 1), (w_branch.reshape(depth, -1, d), 1), (w_out, 1),
             (w_ffn_in, 1), (w_ffn_out, d // COL_TILE))
    wb = [(w[0:1] if g == 1 else _col_blocked(w[0:1], g)).astype(BF16) for w, g in w_f32]

    cosf, sinf = _rope_tables(t, IN_PROJ_ROW_TILE)
    dft_lat = _dft_tables(t)
    dft_ctx = _dft_tables(tc)

    h1 = _prenorm(*x_parts, mod[0], g_norm1[0], cfg=cfg)
    out = None
    for l in range(depth):
        last = l == depth - 1
        rows = n_lat if last else n_lat + n_ctx
        w_in_b, w_gate_b, w_branch_b, w_out_b, w_ffn_in_b, w_ffn_out_b = wb
        w_branch_b = w_branch_b.reshape((1,) + w_branch.shape[1:])
        qkv, rest = _in_proj(h1, w_in_b, 0, cosf, sinf, q_gain[l], k_gain[l], cfg=cfg)
        att = (_attention(qkv, cfg=cfg, latent=True),)
        rec = _lru(rest, conv_w[l], conv_b[l], _lru_block_diag(lru_w_r[l], lru_w_i[l]),
                   lru_b_r[l], lru_b_i[l], lru_lambda[l], cfg=cfg)
        fou = (_fourier(rest, *dft_lat, cfg=cfg, seq_len=t, row_blk0=0),)
        if not last:
            att += (_attention(qkv, cfg=cfg, latent=False),)
            fou += (_fourier(rest, *dft_ctx, cfg=cfg, seq_len=tc, row_blk0=n_lat // tc),)
        m = _merge(h1, att, rec, fou, w_gate_b, b_gate[l:l + 1], w_branch_b, 0, rows=rows)
        xmid, h2 = _out_proj(x_parts, m, w_out_b, 0, mod[l], g_norm2[l], cfg=cfg, rows=rows)
        if last:
            (a,) = _ffn_up(h2, w_ffn_in_b, 0, cfg=cfg, rows=rows)
            (out,) = _ffn_down(a, xmid, w_ffn_out_b, 0, mod[l], mod[l], g_final,
                               cfg=cfg, rows=rows, final=True)
        else:
            a, *wb = _ffn_up(h2, w_ffn_in_b, 0, cfg=cfg, rows=rows, cast_srcs=w_f32,
                             cast_layer=l + 1)
            xall, h1 = _ffn_down(a, xmid, w_ffn_out_b, 0, mod[l], mod[l + 1], g_norm1[l + 1],
                                 cfg=cfg, rows=rows, final=False)
            x_parts = (xall,)
    return out.reshape(b, t, d)
```

```python
import functools
import math

import jax
import jax.numpy as jnp
from jax import lax
from jax.experimental import pallas as pl
from jax.experimental.pallas import tpu as pltpu

F32 = jnp.float32
BF16 = jnp.bfloat16

HEAD_DIM = 128
Q_PER_KV = 4
GRID_W = 64
ROPE_THETA = 10000.0
NORM_EPS = 1e-6
LRU_C = 8.0
FOURIER_GROUP = 128

LANES = 128
SUBLANES = 8
V7X_VMEM_BYTES = 64 * 1024 * 1024
VMEM_LIMIT_BYTES = V7X_VMEM_BYTES * 7 // 8

ROW_TILE = 512
COL_TILE = 512
IN_PROJ_ROW_TILE = 1024
MERGE_ROW_TILE = 1024
FFN_UP_ROW_TILE = 1024
ATTN_Q_TILE = 2048
ATTN_SUB_ROWS = 512
LRU_CHUNK = 256
LRU_GAP = SUBLANES


def _params(*semantics):
    return pltpu.CompilerParams(dimension_semantics=semantics,
                                vmem_limit_bytes=VMEM_LIMIT_BYTES)


def _dot(a, b):
    return jnp.dot(a, b, preferred_element_type=F32)


def _dot_nt(a, b):
    return lax.dot_general(a, b, (((1,), (1,)), ((), ())), preferred_element_type=F32)


def _rms(x):
    return x * lax.rsqrt(jnp.mean(x * x, axis=-1, keepdims=True) + NORM_EPS)


def _ada_norm(x, gain, shift, scale):
    return (_rms(x) * gain * (1.0 + scale) + shift).astype(BF16)


def _mod_kernel(c_ref, w_ref, b_ref, o_ref):
    c = c_ref[...]
    s = (c * jax.nn.sigmoid(c)).astype(BF16)
    o_ref[...] = _dot(s, w_ref[...].astype(BF16)) + b_ref[...]


def _modulation(cc, w_mod, b_mod):
    depth, d, n = w_mod.shape
    rows = cc.shape[0]
    tn = 2 * COL_TILE
    return pl.pallas_call(
        _mod_kernel,
        grid=(depth, n // tn),
        in_specs=[
            pl.BlockSpec((rows, d), lambda l, j: (0, 0)),
            pl.BlockSpec((None, d, tn), lambda l, j: (l, 0, j)),
            pl.BlockSpec((None, 1, tn), lambda l, j: (l, 0, j)),
        ],
        out_specs=pl.BlockSpec((None, rows, tn), lambda l, j: (l, 0, j)),
        out_shape=jax.ShapeDtypeStruct((depth, rows, n), F32),
        compiler_params=_params("parallel", "parallel"),
        name="modulation",
    )(cc, w_mod, b_mod.reshape(depth, 1, n))


def _split_row_specs(n_lat_tiles, tm, d):
    return [pl.BlockSpec((tm, d), lambda i: (jnp.minimum(i, n_lat_tiles - 1), 0)),
            pl.BlockSpec((tm, d), lambda i: (jnp.maximum(i - n_lat_tiles, 0), 0))]


def _for_row_source(x_ref, c_ref, n_lat_tiles, fn):
    i = pl.program_id(0)
    pl.when(i < n_lat_tiles)(lambda: fn(x_ref))
    pl.when(i >= n_lat_tiles)(lambda: fn(c_ref))


def _prenorm_kernel(x_ref, c_ref, mod_ref, g_ref, h_ref, *, n_lat_tiles):
    def emit(r):
        h_ref[...] = _ada_norm(r[...], g_ref[...], mod_ref[0:1, :], mod_ref[1:2, :])
    _for_row_source(x_ref, c_ref, n_lat_tiles, emit)


def _prenorm(x2d, ctx2d, mod_l, g_norm, *, cfg):
    d = x2d.shape[1]
    rows = x2d.shape[0] + ctx2d.shape[0]
    tm = ROW_TILE
    n_lat_tiles = x2d.shape[0] // tm
    return pl.pallas_call(
        functools.partial(_prenorm_kernel, n_lat_tiles=n_lat_tiles),
        grid=(rows // tm,),
        in_specs=_split_row_specs(n_lat_tiles, tm, d) + [
            pl.BlockSpec((None, 6, d), lambda i: (cfg["mod_row"](i), 0, 0)),
            pl.BlockSpec((1, d), lambda i: (0, 0)),
        ],
        out_specs=pl.BlockSpec((tm, d), lambda i: (i, 0)),
        out_shape=jax.ShapeDtypeStruct((rows, d), BF16),
        compiler_params=_params("parallel"),
        name="prenorm",
    )(x2d, ctx2d, mod_l, g_norm.reshape(1, d))


def _in_proj_kernel(h_ref, w_ref, cos_ref, sin_ref, qg_ref, kg_ref, qkv_ref, rest_ref, qkv_scr,
                    *, n_q_heads, n_kv_heads):
    j = pl.program_id(1)

    @pl.when(j == 0)
    def _():
        qkv_scr[...] = _dot(h_ref[...], w_ref[...])

    @pl.when(j == 1)
    def _():
        rest_ref[...] = _dot(h_ref[...], w_ref[...]).astype(BF16)
        cos = cos_ref[...]
        sin = sin_ref[...]
        q_gain = qg_ref[...] * (HEAD_DIM ** -0.5 * math.log2(math.e))
        k_gain = kg_ref[...]
        n_rot = n_q_heads + n_kv_heads
        for hh in range(n_rot):
            cols = slice(hh * HEAD_DIM, (hh + 1) * HEAD_DIM)
            y = _rms(qkv_scr[:, cols]) * (q_gain if hh < n_q_heads else k_gain)
            y = y * cos + pltpu.roll(y, HEAD_DIM // 2, axis=1) * sin
            qkv_ref[:, cols] = y.astype(BF16)
        qkv_ref[:, n_rot * HEAD_DIM:] = qkv_scr[:, n_rot * HEAD_DIM:].astype(BF16)

    @pl.when(j > 1)
    def _():
        rest_ref[...] = _dot(h_ref[...], w_ref[...]).astype(BF16)


def _in_proj(h1, w_in, l, cosf, sinf, q_gain, k_gain, *, cfg):
    rows, d = h1.shape
    d_in = w_in.shape[2]
    tm, tn = IN_PROJ_ROW_TILE, cfg["tn_in"]
    assert d_in == 3 * tn
    kern = functools.partial(_in_proj_kernel, n_q_heads=cfg["n_q_heads"],
                             n_kv_heads=cfg["n_kv_heads"])
    return pl.pallas_call(
        kern,
        grid=(rows // tm, d_in // tn),
        in_specs=[
            pl.BlockSpec((tm, d), lambda i, j: (i, 0)),
            pl.BlockSpec((None, d, tn), lambda i, j: (l, 0, j)),
            pl.BlockSpec((tm, HEAD_DIM), lambda i, j: (cfg["rope_block"](i), 0)),
            pl.BlockSpec((tm, HEAD_DIM), lambda i, j: (cfg["rope_block"](i), 0)),
            pl.BlockSpec((1, HEAD_DIM), lambda i, j: (0, 0)),
            pl.BlockSpec((1, HEAD_DIM), lambda i, j: (0, 0)),
        ],
        out_specs=[
            pl.BlockSpec((tm, tn), lambda i, j: (i, 0)),
            pl.BlockSpec((tm, tn), lambda i, j: (i, jnp.maximum(j - 1, 0))),
        ],
        out_shape=[
            jax.ShapeDtypeStruct((rows, tn), BF16),
            jax.ShapeDtypeStruct((rows, d_in - tn), BF16),
        ],
        scratch_shapes=[pltpu.VMEM((tm, tn), F32)],
        compiler_params=_params("parallel", "arbitrary"),
        name="in_proj",
    )(h1, w_in, cosf, sinf, q_gain.reshape(1, HEAD_DIM), k_gain.reshape(1, HEAD_DIM))


def _attn_kernel(q_ref, *refs, latent):
    if latent:
        kl_ref, kc_ref, vl_ref, vc_ref, o_ref, vl_aug, vc_aug = refs
    else:
        kc_ref, vc_ref, o_ref, vc_aug = refs

    @pl.when(pl.program_id(2) == 0)
    def _():
        for v_ref, aug in ((vl_ref, vl_aug), (vc_ref, vc_aug)) if latent else ((vc_ref, vc_aug),):
            lane = lax.broadcasted_iota(jnp.int32, v_ref.shape, 1)
            aug[:, :HEAD_DIM] = v_ref[...]
            aug[:, HEAD_DIM:] = jnp.where(lane == 0, 1.0, 0.0).astype(BF16)

    kc = kc_ref[...]
    sub = min(q_ref.shape[0], ATTN_SUB_ROWS)
    for r0 in range(0, q_ref.shape[0], sub):
        for g in range(Q_PER_KV):
            cols = slice(g * HEAD_DIM, (g + 1) * HEAD_DIM)
            q = q_ref[r0:r0 + sub, cols]
            sc = _dot_nt(q, kc)
            m = jnp.max(sc, axis=-1, keepdims=True)
            if latent:
                sl = _dot_nt(q, kl_ref[...])
                m = jnp.maximum(m, jnp.max(sl, axis=-1, keepdims=True))
            o = _dot(jnp.exp2((sc - m).astype(BF16)), vc_aug[...])
            if latent:
                o = o + _dot(jnp.exp2((sl - m).astype(BF16)), vl_aug[...])
            o_ref[r0:r0 + sub, cols] = (
                o[:, :HEAD_DIM] / o[:, HEAD_DIM:HEAD_DIM + 1]).astype(BF16)


def _attention(qkv, *, cfg, latent):
    b, t, tc = cfg["b"], cfg["t"], cfg["tc"]
    tq = min(ATTN_Q_TILE, t) if latent else tc
    assert t % tq == 0
    n_q_tiles = t // tq if latent else 1
    q_blk0 = 0 if latent else b * t // tc
    n_kv = cfg["n_kv_heads"]
    gw = Q_PER_KV * HEAD_DIM
    k_col = cfg["n_q_heads"]
    v_col = k_col + n_kv
    ctx_blk0 = b * t // tc

    q_spec = pl.BlockSpec((tq, gw), lambda bi, h, qi: (q_blk0 + bi * n_q_tiles + qi, h))
    lat_spec = lambda col: pl.BlockSpec((t, HEAD_DIM), lambda bi, h, qi: (bi, col + h))
    ctx_spec = lambda col: pl.BlockSpec((tc, HEAD_DIM), lambda bi, h, qi: (ctx_blk0 + bi, col + h))
    aug = lambda n: pltpu.VMEM((n, 2 * HEAD_DIM), BF16)
    if latent:
        in_specs = [q_spec, lat_spec(k_col), ctx_spec(k_col), lat_spec(v_col), ctx_spec(v_col)]
        scratch = [aug(t), aug(tc)]
    else:
        in_specs = [q_spec, ctx_spec(k_col), ctx_spec(v_col)]
        scratch = [aug(tc)]
    return pl.pallas_call(
        functools.partial(_attn_kernel, latent=latent),
        grid=(b, n_kv, n_q_tiles),
        in_specs=in_specs,
        out_specs=pl.BlockSpec((tq, gw), lambda bi, h, qi: (bi * n_q_tiles + qi, h)),
        out_shape=jax.ShapeDtypeStruct((b * (t if latent else tc), cfg["n_q_heads"] * HEAD_DIM),
                                       BF16),
        scratch_shapes=scratch,
        compiler_params=_params("parallel", "parallel", "arbitrary"),
        name="attention",
    )(*[qkv] * len(in_specs))


def _lru_kernel(ux_ref, ug_ref, cw_ref, cb_ref, w_ref, br_ref, bi_ref, lam_ref, o_ref,
                u_scr, af_scr, bf_scr, yf_scr, ab_scr, bb_scr, yb_scr, *, b, t, tc):
    ch, gap = LRU_CHUNK, LRU_GAP
    n_lat = b * t
    conv_width = cw_ref.shape[0]
    conv_left = conv_width // 2

    lat = (gap, 0, t + gap, t)
    ctx = (gap + b * (t + gap), n_lat, tc + gap, tc)

    zeros_gap = jnp.zeros((gap, LANES), F32)
    for kind, n in ((lat, t), (ctx, tc)):
        for bi in range(b):
            u0 = kind[0] + bi * kind[2]
            r0 = kind[1] + bi * kind[3]
            u_scr[u0 - gap:u0, :] = zeros_gap
            u_scr[u0:u0 + n, :] = ux_ref[r0:r0 + n, :].astype(F32)
    u_end = ctx[0] + b * ctx[2] - gap
    u_scr[u_end:u_end + gap, :] = zeros_gap

    def conv_chunk(s0):
        u = cb_ref[...]
        for j in range(conv_width):
            tap0 = s0 - conv_left + j
            u = u + cw_ref[j:j + 1, :] * u_scr[tap0:tap0 + ch, :]
        return u

    for kind, n in ((lat, t), (ctx, tc)):
        for bi in range(b):
            u0 = kind[0] + bi * kind[2]
            held = None
            for c in range(n // ch):
                cur = conv_chunk(u0 + c * ch)
                if held is not None:
                    u_scr[u0 + (c - 1) * ch:u0 + c * ch, :] = held
                held = cur
            u_scr[u0 + n - ch:u0 + n, :] = held

    def slab_rows(kind, bi, off):
        start = kind[1] + bi * kind[3] + off
        if not isinstance(start, int):
            start = pl.multiple_of(start, ch)
        return pl.ds(start, ch)

    def coefficients(d, kind, off, a_scr, b_scr):
        lam = lam_ref[d:d + 1, :]
        softplus_neg_lam = jnp.maximum(-lam, 0.0) + jnp.log1p(jnp.exp(-jnp.abs(lam)))
        half_rate = -0.5 * LRU_C * softplus_neg_lam
        w_half = w_ref[:, d * 2 * LANES:(d + 1) * 2 * LANES] * 0.5
        br_half = 0.5 * br_ref[d:d + 1, :]
        bi_half = 0.5 * bi_ref[d:d + 1, :]
        for bi in range(b):
            s0 = kind[0] + bi * kind[2] + off
            if not isinstance(s0, int):
                s0 = pl.multiple_of(s0, SUBLANES)
            u = u_scr[pl.ds(s0, ch), :]
            gates = _dot(u.astype(BF16), w_half)
            log_a = jnp.tanh(gates[:, :LANES] + br_half) * half_rate + half_rate
            i_gate = 0.5 * jnp.tanh(gates[:, LANES:] + bi_half) + 0.5
            a = jnp.exp(log_a)
            one_minus_a2 = -jnp.tanh(log_a) * (a * a + 1.0)
            root = jnp.where(one_minus_a2 > 0.0, one_minus_a2 * lax.rsqrt(one_minus_a2), 0.0)
            a_scr[pl.ds(bi, ch, stride=b), :] = a
            b_scr[pl.ds(bi, ch, stride=b), :] = root * i_gate * u

    def emit(rows, y):
        gate = jax.nn.gelu(ug_ref[rows, :].astype(F32), approximate=True)
        o_ref[rows, :] = (y * gate).astype(BF16)

    def pair(kind, f_off, b_off, carry, mode):
        coefficients(0, kind, f_off, af_scr, bf_scr)
        coefficients(1, kind, b_off, ab_scr, bb_scr)

        def two_steps(h, a_scr, b_scr, y_scr, base, s0, s1):
            r0 = pl.ds(pl.multiple_of(base + s0 * b, b), b)
            r1 = pl.ds(pl.multiple_of(base + s1 * b, b), b)
            a0, b0, a1, b1 = a_scr[r0, :], b_scr[r0, :], a_scr[r1, :], b_scr[r1, :]
            y_scr[r0, :] = a0 * h + b0
            h = (a1 * a0) * h + (a1 * b0 + b1)
            y_scr[r1, :] = h
            return h

        span = 32

        def trip(i, hs):
            hf, hb = hs
            base_f = pl.multiple_of(i * (span * b), span * b)
            base_b = pl.multiple_of((ch - span) * b - i * (span * b), span * b)
            for k in range(0, span, 2):
                hf = two_steps(hf, af_scr, bf_scr, yf_scr, base_f, k, k + 1)
                hb = two_steps(hb, ab_scr, bb_scr, yb_scr, base_b, span - 1 - k, span - 2 - k)
            return hf, hb

        carry = lax.fori_loop(0, ch // span, trip, carry)

        for bi in range(b):
            blk = pl.ds(bi, ch, stride=b)
            rf = slab_rows(kind, bi, f_off)
            rb = slab_rows(kind, bi, b_off)
            if mode == "same":
                emit(rf, yf_scr[blk, :] + yb_scr[blk, :])
            elif mode == "first":
                o_ref[rf, :] = yf_scr[blk, :].astype(BF16)
                o_ref[rb, :] = yb_scr[blk, :].astype(BF16)
            else:
                emit(rf, o_ref[rf, :].astype(F32) + yf_scr[blk, :])
                emit(rb, o_ref[rb, :].astype(F32) + yb_scr[blk, :])
        return carry

    n_lat_chunks = t // ch
    half = n_lat_chunks // 2
    zero = jnp.zeros((b, LANES), F32)
    carry = pair(ctx, 0, 0, (zero, zero), "same")
    carry = lax.fori_loop(
        0, half,
        lambda q, hs: pair(lat, q * ch, (n_lat_chunks - 1 - q) * ch, hs, "first"), carry)
    lax.fori_loop(
        0, half,
        lambda q, hs: pair(lat, (half + q) * ch, (half - 1 - q) * ch, hs, "second"), carry)


def _lru(rest, conv_w, conv_b, w_bd, b_r, b_i, lam, *, cfg):
    rows = rest.shape[0]
    d_lru = conv_w.shape[1]
    ux_blk = cfg["ux_col"] // LANES
    ug_blk = cfg["ug_col"] // LANES
    nb = cfg["b"]
    assert cfg["tc"] == LRU_CHUNK and (cfg["t"] // LRU_CHUNK) % 2 == 0
    kern = functools.partial(_lru_kernel, b=nb, t=cfg["t"], tc=cfg["tc"])
    vec = lambda n: pl.BlockSpec((n, LANES), lambda c: (0, c))
    return pl.pallas_call(
        kern,
        grid=(d_lru // LANES,),
        in_specs=[
            pl.BlockSpec((rows, LANES), lambda c: (0, ux_blk + c)),
            pl.BlockSpec((rows, LANES), lambda c: (0, ug_blk + c)),
            vec(conv_w.shape[0]), vec(1),
            pl.BlockSpec((None, LANES, 4 * LANES), lambda c: (c, 0, 0)),
            vec(2), vec(2), vec(2),
        ],
        out_specs=pl.BlockSpec((rows, LANES), lambda c: (0, c)),
        out_shape=jax.ShapeDtypeStruct((rows, d_lru), BF16),
        scratch_shapes=[
            pltpu.VMEM((rows + (2 * nb + 1) * LRU_GAP, LANES), F32),
        ] + [pltpu.VMEM((nb * LRU_CHUNK, LANES), F32)] * 6,
        compiler_params=_params("parallel"),
        name="rglru",
    )(rest, rest, conv_w, conv_b.reshape(1, d_lru), w_bd, b_r, b_i, lam)


def _lru_block_diag(w_r, w_i):
    _, nb, bw, _ = w_r.shape
    per = LANES // bw
    eye = jnp.eye(per, dtype=w_r.dtype)

    def bd(w):
        w = w.reshape(nb // per, per, bw, bw)
        return jnp.einsum("cipq,ij->cipjq", w, eye).reshape(nb // per, LANES, LANES)

    return jnp.concatenate([bd(w_r[0]), bd(w_i[0]), bd(w_r[1]), bd(w_i[1])], axis=-1).astype(BF16)


def _fourier_kernel(u_ref, cst_ref, rev_ref, w2_ref, o_ref, p_scr, q_scr, *, t, n_groups, scale):
    half = t // 2
    for g in range(n_groups):
        cols = slice(g * FOURIER_GROUP, (g + 1) * FOURIER_GROUP)
        pq = _dot(u_ref[:, cols], w2_ref[...])
        p_scr[:, cols] = pq[:, :FOURIER_GROUP].astype(BF16)
        q_scr[:, cols] = pq[:, FOURIER_GROUP:].astype(BF16)
    a = _dot(cst_ref[:, :t], p_scr[...])
    bs = _dot(cst_ref[:, t:], q_scr[...])
    o_ref[0:half, :] = ((a - bs) * scale).astype(BF16)
    mirrored = _dot(rev_ref[...], ((a + bs) * scale).astype(BF16))
    p = p_scr[...].astype(F32)
    odd_row = (lax.broadcasted_iota(jnp.int32, p.shape, 0) & 1) == 1
    mid = jnp.sum(jnp.where(odd_row, -p, p), axis=0, keepdims=True) * scale
    first_row = lax.broadcasted_iota(jnp.int32, mirrored.shape, 0) == 0
    o_ref[half:t, :] = jnp.where(first_row, mid, mirrored).astype(BF16)


def _dft_tables(t):
    half = t // 2
    k = jnp.arange(half, dtype=jnp.int32)[:, None]
    s = jnp.arange(t, dtype=jnp.int32)[None, :]
    at = ((k * s) % t).astype(F32) * (2.0 * math.pi / t)
    cst = jnp.concatenate([jnp.cos(at), jnp.sin(at)], axis=1).astype(BF16)
    r = jnp.arange(half, dtype=jnp.int32)
    rev = (r[:, None] + r[None, :] == half).astype(BF16)
    c = jnp.arange(FOURIER_GROUP, dtype=jnp.int32)
    ac = ((c[:, None] * c[None, :]) % FOURIER_GROUP).astype(F32) * (2.0 * math.pi / FOURIER_GROUP)
    w2 = jnp.concatenate([jnp.cos(ac), jnp.sin(ac)], axis=1).astype(BF16)
    return cst, rev, w2


def _fourier(rest, cst, rev, w2, *, cfg, seq_len, row_blk0):
    d_f = cfg["d_branch"]
    wcols = min(d_f, COL_TILE)
    uf_blk = cfg["uf_col"] // wcols
    kern = functools.partial(_fourier_kernel, t=seq_len, n_groups=wcols // FOURIER_GROUP,
                             scale=(seq_len * FOURIER_GROUP) ** -0.5)
    const = lambda shape: pl.BlockSpec(shape, lambda bi, hf: (0, 0), pipeline_mode=pl.Buffered(1))
    return pl.pallas_call(
        kern,
        grid=(cfg["b"], d_f // wcols),
        in_specs=[
            pl.BlockSpec((seq_len, wcols), lambda bi, hf: (row_blk0 + bi, uf_blk + hf)),
            const(cst.shape), const(rev.shape), const(w2.shape),
        ],
        out_specs=pl.BlockSpec((seq_len, wcols), lambda bi, hf: (bi, hf)),
        out_shape=jax.ShapeDtypeStruct((cfg["b"] * seq_len, d_f), BF16),
        scratch_shapes=[pltpu.VMEM((seq_len, wcols), BF16)] * 2,
        compiler_params=_params("parallel", "parallel"),
        name="fourier",
    )(rest, cst, rev, w2)


def _merge_kernel(*refs, n_parts, n_lat_tiles):
    h_ref = refs[0]
    att_refs = refs[1:1 + n_parts]
    rec_ref = refs[1 + n_parts]
    fou_refs = refs[2 + n_parts:2 + 2 * n_parts]
    wgs, bgs, wbs = (refs[2 + 2 * n_parts + 3 * k:5 + 2 * n_parts + 3 * k] for k in range(3))
    o_ref = refs[-1]
    is_latent = pl.program_id(0) < n_lat_tiles

    def rows_of(parts):
        if len(parts) == 1:
            return parts[0][...]
        return jnp.where(is_latent, parts[0][...], parts[1][...])

    h = h_ref[...]
    acc = None
    for y, wg, bg, wb in zip((rows_of(att_refs), rec_ref[...], rows_of(fou_refs)), wgs, bgs, wbs):
        gate = jax.nn.sigmoid(_dot(h, wg[...]) + bg[...])
        term = gate * _dot(y, wb[...])
        acc = term if acc is None else acc + term
    o_ref[...] = acc.astype(BF16)


def _merge(h, att_parts, rec, fou_parts, w_gate, b_gate, w_branch, l, *, rows):
    d = h.shape[1]
    d_b = rec.shape[1]
    tm, tn = MERGE_ROW_TILE, COL_TILE
    nj = d // tn
    n_parts = len(att_parts)
    n_lat_tiles = att_parts[0].shape[0] // tm
    y_spec = pl.BlockSpec((tm, d_b), lambda i, j: (i, 0))
    part_specs = [pl.BlockSpec((tm, d_b), lambda i, j: (jnp.minimum(i, n_lat_tiles - 1), 0)),
                  pl.BlockSpec((tm, d_b), lambda i, j: (jnp.maximum(i - n_lat_tiles, 0), 0),
                               pipeline_mode=pl.Buffered(1))][:n_parts]
    wg_spec = lambda k: pl.BlockSpec((None, d, tn), lambda i, j: (l, 0, k * nj + j))
    bg_spec = lambda k: pl.BlockSpec((None, 1, tn), lambda i, j: (l, 0, k * nj + j))
    wb_spec = lambda k: pl.BlockSpec((None, None, d_b, tn), lambda i, j: (l, k, 0, j))
    bg = b_gate.reshape(b_gate.shape[0], 1, -1)
    return pl.pallas_call(
        functools.partial(_merge_kernel, n_parts=n_parts, n_lat_tiles=n_lat_tiles),
        grid=(rows // tm, nj),
        in_specs=[pl.BlockSpec((tm, d), lambda i, j: (i, 0))] + part_specs + [y_spec] + part_specs
                 + [wg_spec(0), wg_spec(1), wg_spec(2), bg_spec(0), bg_spec(1), bg_spec(2),
                    wb_spec(0), wb_spec(1), wb_spec(2)],
        out_specs=pl.BlockSpec((tm, tn), lambda i, j: (i, j)),
        out_shape=jax.ShapeDtypeStruct((rows, d), BF16),
        compiler_params=_params("parallel", "arbitrary"),
        name="gated_merge",
    )(h, *att_parts, rec, *fou_parts, w_gate, w_gate, w_gate, bg, bg, bg,
      w_branch, w_branch, w_branch)


def _out_proj_kernel(*refs, n_lat_tiles):
    x_refs, (m_ref, w_ref, mod_ref, g_ref, o_ref, h_ref) = refs[:-6], refs[-6:]
    delta = mod_ref[2:3, :] * _dot(m_ref[...], w_ref[...])

    def emit(r):
        y = r[...] + delta
        o_ref[...] = y
        h_ref[...] = _ada_norm(y, g_ref[...], mod_ref[3:4, :], mod_ref[4:5, :])

    if len(x_refs) == 2:
        _for_row_source(*x_refs, n_lat_tiles, emit)
    else:
        emit(x_refs[0])


def _out_proj(x_parts, m, w_out, l, mod_l, g_norm2, *, cfg, rows):
    d = m.shape[1]
    tm = ROW_TILE
    row_spec = pl.BlockSpec((tm, d), lambda i: (i, 0))
    n_lat_tiles = x_parts[0].shape[0] // tm
    x_specs = _split_row_specs(n_lat_tiles, tm, d) if len(x_parts) == 2 else [row_spec]
    return pl.pallas_call(
        functools.partial(_out_proj_kernel, n_lat_tiles=n_lat_tiles),
        grid=(rows // tm,),
        in_specs=x_specs + [
            row_spec,
            pl.BlockSpec((None, d, d), lambda i: (l, 0, 0), pipeline_mode=pl.Buffered(1)),
            pl.BlockSpec((None, 6, d), lambda i: (cfg["mod_row"](i), 0, 0)),
            pl.BlockSpec((1, d), lambda i: (0, 0)),
        ],
        out_specs=[row_spec, row_spec],
        out_shape=[jax.ShapeDtypeStruct((rows, d), F32), jax.ShapeDtypeStruct((rows, d), BF16)],
        compiler_params=_params("parallel"),
        name="out_proj",
    )(*x_parts, m, w_out, mod_l, g_norm2.reshape(1, d))


def _ffn_up_kernel(h_ref, wg_ref, wu_ref, *refs):
    n_cast = (len(refs) - 1) // 2
    a_ref = refs[n_cast]
    h = h_ref[...]
    gt = _dot(h, wg_ref[...])
    up = _dot(h, wu_ref[...])
    a_ref[...] = (gt * jax.nn.sigmoid(gt) * up).astype(BF16)
    for src, dst in zip(refs[:n_cast], refs[n_cast + 1:]):
        if len(dst.shape) == 2:
            dst[...] = src[...].astype(BF16)
        else:
            for g in range(dst.shape[0]):
                dst[g] = src[:, g * dst.shape[2]:(g + 1) * dst.shape[2]].astype(BF16)


def _col_blocked(w, n_groups):
    l, r, c = w.shape
    return w.reshape(l, r, n_groups, c // n_groups).transpose(0, 2, 1, 3)


def _ffn_up(h2, w_ffn_in, l, *, cfg, rows, cast_srcs=(), cast_layer=None):
    d = h2.shape[1]
    d_ff = w_ffn_in.shape[2] // 2
    tm, tf = FFN_UP_ROW_TILE, cfg["tf"]
    nk = d_ff // tf
    n_steps = (rows // tm) * nk
    cast_in, cast_out, cast_shape = [], [], []
    for w, groups in cast_srcs:
        _, r, c = w.shape
        br = next(x for x in (16, 32, 64, 128, 256) if r % x == 0 and r // x <= n_steps)
        blk = lambda i, k, n=r // br: jnp.minimum(i * nk + k, n - 1)
        cast_in.append(pl.BlockSpec((None, br, c), lambda i, k, blk=blk: (cast_layer, blk(i, k), 0)))
        if groups == 1:
            cast_out.append(pl.BlockSpec((None, br, c), lambda i, k, blk=blk: (0, blk(i, k), 0)))
            cast_shape.append(jax.ShapeDtypeStruct((1, r, c), BF16))
        else:
            cast_out.append(pl.BlockSpec((None, groups, br, c // groups),
                                         lambda i, k, blk=blk: (0, 0, blk(i, k), 0)))
            cast_shape.append(jax.ShapeDtypeStruct((1, groups, r, c // groups), BF16))
    return pl.pallas_call(
        _ffn_up_kernel,
        grid=(rows // tm, nk),
        in_specs=[
            pl.BlockSpec((tm, d), lambda i, k: (i, 0)),
            pl.BlockSpec((None, d, tf), lambda i, k: (l, 0, k)),
            pl.BlockSpec((None, d, tf), lambda i, k: (l, 0, nk + k)),
        ] + cast_in,
        out_specs=[pl.BlockSpec((tm, tf), lambda i, k: (i, k))] + cast_out,
        out_shape=[jax.ShapeDtypeStruct((rows, d_ff), BF16)] + cast_shape,
        compiler_params=_params("arbitrary", "arbitrary"),
        name="ffn_up",
    )(h2, w_ffn_in, w_ffn_in, *[w for w, _ in cast_srcs])


def _ffn_down_kernel(a_ref, x_ref, wo_ref, mod_ref, nmod_ref, g_ref, *refs, nj, d, final):
    out_refs, y_scr = refs[:-1], refs[-1]
    i, j = pl.program_id(0), pl.program_id(1)
    cur = i % 2

    @pl.when((i == 0) & (j == 0))
    def _():
        y_scr[1] = jnp.zeros(y_scr.shape[1:], F32)

    prv = y_scr.at[1 - cur]
    ssq = sum(jnp.sum(prv[jj] * prv[jj], axis=-1, keepdims=True) for jj in range(nj))
    normed = prv[j] * lax.rsqrt(ssq * (1.0 / d) + NORM_EPS) * g_ref[...]
    if final:
        out_refs[0][...] = normed
    else:
        out_refs[1][...] = (normed * (1.0 + nmod_ref[1:2, :]) + nmod_ref[0:1, :]).astype(BF16)

    jw = jnp.where(i == pl.num_programs(0) - 1, nj - 1, j)
    y = x_ref[...] + mod_ref[5:6, :] * _dot(a_ref[...], wo_ref[jw])
    if not final:
        out_refs[0][...] = y
    y_scr[cur, j] = y


def _ffn_down(a, xmid, w_ffn_out, l, mod_l, next_mod, gain, *, cfg, rows, final):
    d = xmid.shape[1]
    d_ff = a.shape[1]
    tm = ROW_TILE
    _, nj, _, tn = w_ffn_out.shape
    n_tiles = rows // tm
    cur = lambda i: jnp.minimum(i, n_tiles - 1)
    prev = lambda i: jnp.maximum(i - 1, 0)
    cur_j = lambda i, j: jnp.where(i == n_tiles, nj - 1, j)
    prev_j = lambda i, j: jnp.where(i == 0, 0, j)
    cur_spec = pl.BlockSpec((tm, tn), lambda i, j: (cur(i), cur_j(i, j)))
    prev_spec = pl.BlockSpec((tm, tn), lambda i, j: (prev(i), prev_j(i, j)))
    if final:
        out_specs, out_shape = [prev_spec], [jax.ShapeDtypeStruct((rows, d), F32)]
    else:
        out_specs = [cur_spec, prev_spec]
        out_shape = [jax.ShapeDtypeStruct((rows, d), F32), jax.ShapeDtypeStruct((rows, d), BF16)]
    return pl.pallas_call(
        functools.partial(_ffn_down_kernel, nj=nj, d=d, final=final),
        grid=(n_tiles + 1, nj),
        in_specs=[
            pl.BlockSpec((tm, d_ff), lambda i, j: (cur(i), 0)),
            cur_spec,
            pl.BlockSpec((None, nj, d_ff, tn), lambda i, j: (l, 0, 0, 0),
                         pipeline_mode=pl.Buffered(1)),
            pl.BlockSpec((None, 6, tn), lambda i, j: (cfg["mod_row"](cur(i)), 0, cur_j(i, j))),
            pl.BlockSpec((None, 6, tn), lambda i, j: (cfg["mod_row"](prev(i)), 0, j)),
            pl.BlockSpec((1, tn), lambda i, j: (0, j)),
        ],
        out_specs=out_specs,
        out_shape=out_shape,
        scratch_shapes=[pltpu.VMEM((2, nj, tm, tn), F32)],
        compiler_params=_params("arbitrary", "arbitrary"),
        name="ffn_down",
    )(a, xmid, w_ffn_out, mod_l, next_mod, gain.reshape(1, d))


def _rope_tables(t, tile_rows):
    pairs = HEAD_DIM // 4
    rows = t // GRID_W
    row = jnp.repeat(jnp.arange(rows, dtype=F32), GRID_W)
    col = jnp.tile(jnp.arange(GRID_W, dtype=F32), rows)
    inv = ROPE_THETA ** (-jnp.arange(pairs, dtype=F32) / pairs)
    ang = jnp.concatenate([row[:, None] * inv, col[:, None] * inv], axis=-1)
    cos, sin = jnp.cos(ang), jnp.sin(ang)
    cosf = jnp.concatenate([cos, cos], axis=-1)
    sinf = jnp.concatenate([-sin, sin], axis=-1)
    cosf = jnp.concatenate([cosf, jnp.ones((tile_rows, HEAD_DIM), F32)], axis=0)
    sinf = jnp.concatenate([sinf, jnp.zeros((tile_rows, HEAD_DIM), F32)], axis=0)
    return cosf, sinf


def kernel(x, c, ctx, c_ctx, w_mod, b_mod, g_norm1, g_norm2, w_in, q_gain, k_gain, conv_w, conv_b, lru_w_r, lru_b_r, lru_w_i, lru_b_i, lru_lambda, w_branch, w_gate, b_gate, w_out, w_ffn_in, w_ffn_out, g_final):
    b, t, d = x.shape
    tc = ctx.shape[1]
    depth = w_in.shape[0]
    d_branch = w_branch.shape[2]
    n_q_heads = d_branch // HEAD_DIM
    n_kv_heads = n_q_heads // Q_PER_KV
    d_kv = n_kv_heads * HEAD_DIM
    d_ff = w_ffn_out.shape[1]
    n_lat, n_ctx = b * t, b * tc
    assert b == SUBLANES and t % IN_PROJ_ROW_TILE == 0 and n_ctx % FFN_UP_ROW_TILE == 0
    assert tc % LRU_CHUNK == 0 and t % LRU_CHUNK == 0
    tiles_per_batch = t // ROW_TILE

    cfg = dict(
        b=b, t=t, tc=tc, d_branch=d_branch, n_q_heads=n_q_heads, n_kv_heads=n_kv_heads,
        tn_in=d_branch + 2 * d_kv,
        ux_col=0, ug_col=d_branch, uf_col=2 * d_branch,
        tf=COL_TILE if d_ff % COL_TILE == 0 else COL_TILE // 2,
        mod_row=lambda i: jnp.minimum(i // tiles_per_batch, b),
        rope_block=lambda i: jnp.where(i < n_lat // IN_PROJ_ROW_TILE, i % (t // IN_PROJ_ROW_TILE),
                                       t // IN_PROJ_ROW_TILE),
    )

    x_parts = (x.reshape(n_lat, d), ctx.reshape(n_ctx, d))
    cc = jnp.concatenate([c, c_ctx[None, :], jnp.zeros((2 * SUBLANES - b - 1, d), F32)], axis=0)
    mod = _modulation(cc, w_mod, b_mod).reshape(depth, 2 * SUBLANES, 6, d)

    w_f32 = ((w_in, 1), (w_gate, 1), (w_branch.reshape(depth, -1, d), 1), (w_out, 1),
             (w_ffn_in, 1), (w_ffn_out, d // COL_TILE))
    wb = [(w[0:1] if g == 1 else _col_blocked(w[0:1], g)).astype(BF16) for w, g in w_f32]

    cosf, sinf = _rope_tables(t, IN_PROJ_ROW_TILE)
    dft_lat = _dft_tables(t)
    dft_ctx = _dft_tables(tc)

    h1 = _prenorm(*x_parts, mod[0], g_norm1[0], cfg=cfg)
    out = None
    for l in range(depth):
        last = l == depth - 1
        rows = n_lat if last else n_lat + n_ctx
        w_in_b, w_gate_b, w_branch_b, w_out_b, w_ffn_in_b, w_ffn_out_b = wb
        w_branch_b = w_branch_b.reshape((1,) + w_branch.shape[1:])
        qkv, rest = _in_proj(h1, w_in_b, 0, cosf, sinf, q_gain[l], k_gain[l], cfg=cfg)
        att = (_attention(qkv, cfg=cfg, latent=True),)
        rec = _lru(rest, conv_w[l], conv_b[l], _lru_block_diag(lru_w_r[l], lru_w_i[l]),
                   lru_b_r[l], lru_b_i[l], lru_lambda[l], cfg=cfg)
        fou = (_fourier(rest, *dft_lat, cfg=cfg, seq_len=t, row_blk0=0),)
        if not last:
            att += (_attention(qkv, cfg=cfg, latent=False),)
            fou += (_fourier(rest, *dft_ctx, cfg=cfg, seq_len=tc, row_blk0=n_lat // tc),)
        m = _merge(h1, att, rec, fou, w_gate_b, b_gate[l:l + 1], w_branch_b, 0, rows=rows)
        xmid, h2 = _out_proj(x_parts, m, w_out_b, 0, mod[l], g_norm2[l], cfg=cfg, rows=rows)
        if last:
            (a,) = _ffn_up(h2, w_ffn_in_b, 0, cfg=cfg, rows=rows)
            (out,) = _ffn_down(a, xmid, w_ffn_out_b, 0, mod[l], mod[l], g_final,
                               cfg=cfg, rows=rows, final=True)
        else:
            a, *wb = _ffn_up(h2, w_ffn_in_b, 0, cfg=cfg, rows=rows, cast_srcs=w_f32,
                             cast_layer=l + 1)
            xall, h1 = _ffn_down(a, xmid, w_ffn_out_b, 0, mod[l], mod[l + 1], g_norm1[l + 1],
                                 cfg=cfg, rows=rows, final=False)
            x_parts = (xall,)
    return out.reshape(b, t, d)
```
